```python
import math
import jax, jax.numpy as jnp
from jax import lax
import numpy as np

D_MODEL = 1024
BATCH = 4
SEQ = 4096
DEPTH = 2

HEAD_DIM = 64
A_HEADS = 4
A_PATTERNS = ((128, 1), (512, 4), (2048, 16))
B_HEADS = 4
B_Q_RANK = 384
B_KV_RANK = 256
B_NOPE = 64
B_ROPE = 32
B_V = 64
C_Q_HEADS = 8
C_KV_HEADS = 2
GRID_W = 64
ROPE_THETA = 10000.0
REL_BUCKETS = 32
REL_MAX_DIST = 1024
D_FF = 4 * D_MODEL
Q_BLOCK = 128
EPS = 1e-6
NEG_BIG = -1e30
ALPHA = (2 * DEPTH) ** 0.25
BETA = (8 * DEPTH) ** -0.25
ADA_INIT = 0.1

A_W = A_HEADS * HEAD_DIM
IN_SIZES = (A_W, A_W, A_W,
            B_Q_RANK, B_KV_RANK, B_ROPE,
            C_Q_HEADS * HEAD_DIM, C_KV_HEADS * HEAD_DIM, C_KV_HEADS * HEAD_DIM)
IN_COLS = sum(IN_SIZES)
IN_SPLITS = tuple(int(v) for v in np.cumsum(IN_SIZES)[:-1])
MIX_OUT = A_W + B_HEADS * B_V + C_Q_HEADS * HEAD_DIM

kernel_name = 'hybrid_dilated_mla_axialgqa_encoder'


def layer_norm(x, g, b):
    xf = x.astype(jnp.float32)
    mu = xf.mean(-1, keepdims=True)
    var = jnp.square(xf - mu).mean(-1, keepdims=True)
    return ((xf - mu) * lax.rsqrt(var + EPS) * g + b).astype(x.dtype)


def rms_norm(x, g):
    xf = x.astype(jnp.float32)
    return (xf * lax.rsqrt(jnp.square(xf).mean(-1, keepdims=True) + EPS) * g).astype(x.dtype)


def rope_angles(pos, dim):
    inv = ROPE_THETA ** (-jnp.arange(0, dim, 2, dtype=jnp.float32) / dim)
    return pos.astype(jnp.float32)[:, None] * inv[None, :]


def apply_rope(x, ang):
    xf = x.astype(jnp.float32)
    x1, x2 = jnp.split(xf, 2, axis=-1)
    cos, sin = jnp.cos(ang), jnp.sin(ang)
    return jnp.concatenate([x1 * cos - x2 * sin, x1 * sin + x2 * cos], -1).astype(x.dtype)


def axial_rope(x, ang_row, ang_col):
    half = x.shape[-1] // 2
    return jnp.concatenate([apply_rope(x[..., :half], ang_row),
                            apply_rope(x[..., half:], ang_col)], -1)


def t5_bucket(rel):
    nb = REL_BUCKETS // 2
    max_exact = nb // 2
    sign = jnp.where(rel > 0, nb, 0)
    n = jnp.abs(rel)
    nf = jnp.maximum(n, 1).astype(jnp.float32)
    large = max_exact + (jnp.log(nf / max_exact) / math.log(REL_MAX_DIST / max_exact)
                         * (nb - max_exact)).astype(jnp.int32)
    large = jnp.minimum(large, nb - 1)
    return sign + jnp.where(n < max_exact, n, large)


def dilated_window_attn(q, k, v, rel_bias, window, dilation):
    B, H, S, Dh = q.shape
    half = window // (2 * dilation)
    blk = half
    L = S // dilation
    nb = -(-L // blk)
    Lp = nb * blk

    def to_sub(t):
        return t.reshape(B, H, L, dilation, Dh).transpose(0, 1, 3, 2, 4)

    qs = jnp.pad(to_sub(q), ((0, 0), (0, 0), (0, 0), (0, Lp - L), (0, 0)))
    qb = qs.reshape(B, H, dilation, nb, blk, Dh)

    def key_blocks(t):
        tp = jnp.pad(to_sub(t), ((0, 0), (0, 0), (0, 0), (blk, Lp - L + blk), (0, 0)))
        tb = tp.reshape(B, H, dilation, nb + 2, blk, Dh)
        return jnp.concatenate([tb[:, :, :, :-2], tb[:, :, :, 1:-1], tb[:, :, :, 2:]], axis=4)

    kb, vb = key_blocks(k), key_blocks(v)
    s = jnp.einsum('bhrnqd,bhrnkd->bhrnqk', qb, kb,
                   preferred_element_type=jnp.float32) * (Dh ** -0.5)
    off = jnp.arange(3 * blk)[None, :] - blk - jnp.arange(blk)[:, None]
    bias = jnp.moveaxis(rel_bias[t5_bucket(off * dilation)], -1, 0)
    qpos = jnp.arange(nb)[:, None, None] * blk + jnp.arange(blk)[None, :, None]
    kpos = qpos + off[None]
    valid = (jnp.abs(off) <= half)[None] & (kpos >= 0) & (kpos < L)
    s = jnp.where(valid, s + bias[None, :, None, None].astype(jnp.float32), NEG_BIG)
    m = s.max(-1, keepdims=True)
    p = jnp.exp(s - m)
    l = p.sum(-1)
    o = jnp.einsum('bhrnqk,bhrnkd->bhrnqd', p, vb.astype(jnp.float32)) / l[..., None]

    def back(t):
        t = t.reshape((B, H, dilation, Lp) + t.shape[5:])[:, :, :, :L]
        t = jnp.swapaxes(t, 2, 3)
        return t.reshape((B, H, S) + t.shape[4:])

    return back(m[..., 0]), back(l), back(o)


def dilated_mixture(q, k, v, rel_bias):
    res = [dilated_window_attn(q, k, v, rel_bias, w, d) for (w, d) in A_PATTERNS]
    m_all = jnp.stack([r[0] for r in res])
    l_all = jnp.stack([r[1] for r in res])
    o_all = jnp.stack([r[2] for r in res])
    wts = l_all * jnp.exp(m_all - m_all.max(0, keepdims=True))
    o = (wts[..., None] * o_all).sum(0) / wts.sum(0)[..., None]
    return o.astype(q.dtype)


def blocked_attention(q, k, v, scale):
    B, Hq, S, Dk = q.shape
    Hkv, Dv = k.shape[1], v.shape[-1]
    G = Hq // Hkv
    qb = q.reshape(B, Hkv, G, S // Q_BLOCK, Q_BLOCK, Dk).transpose(3, 0, 1, 2, 4, 5)

    def one_block(qblk):
        s = jnp.einsum('bhgqd,bhkd->bhgqk', qblk, k, preferred_element_type=jnp.float32) * scale
        p = jax.nn.softmax(s, axis=-1)
        return jnp.einsum('bhgqk,bhkd->bhgqd', p.astype(v.dtype), v)

    o = lax.map(one_block, qb)
    return o.transpose(1, 2, 3, 0, 4, 5).reshape(B, Hq, S, Dv)


def mla_attention(cq, ckv, kr, q_norm_g, w_uq, kv_norm_g, w_ukv, ang):
    B, S, _ = cq.shape
    q = (rms_norm(cq, q_norm_g) @ w_uq).reshape(B, S, B_HEADS, B_NOPE + B_ROPE).transpose(0, 2, 1, 3)
    q = jnp.concatenate([q[..., :B_NOPE], apply_rope(q[..., B_NOPE:], ang)], -1)
    kv = (rms_norm(ckv, kv_norm_g) @ w_ukv).reshape(B, S, B_HEADS, B_NOPE + B_V).transpose(0, 2, 1, 3)
    k_rope = apply_rope(kr, ang)[:, None]
    k = jnp.concatenate([kv[..., :B_NOPE],
                         jnp.broadcast_to(k_rope, (B, B_HEADS, S, B_ROPE))], -1)
    v = kv[..., B_NOPE:]
    return blocked_attention(q, k, v, (B_NOPE + B_ROPE) ** -0.5)


def axial_gqa(q, k, v, q_norm_g, k_norm_g, ang_row, ang_col):
    B, S, _ = q.shape
    q = q.reshape(B, S, C_Q_HEADS, HEAD_DIM).transpose(0, 2, 1, 3)
    k = k.reshape(B, S, C_KV_HEADS, HEAD_DIM).transpose(0, 2, 1, 3)
    v = v.reshape(B, S, C_KV_HEADS, HEAD_DIM).transpose(0, 2, 1, 3)
    q = axial_rope(rms_norm(q, q_norm_g), ang_row, ang_col)
    k = axial_rope(rms_norm(k, k_norm_g), ang_row, ang_col)
    return blocked_attention(q, k, v, HEAD_DIM ** -0.5)


def heads_to_seq(o):
    B, H, S, D = o.shape
    return o.transpose(0, 2, 1, 3).reshape(B, S, H * D)


def setup_inputs(seed: int = 0) -> dict:
    key = jax.random.key(seed)
    ks = jax.random.split(key, 20)
    f32 = jnp.float32
    nrm = lambda k, shape, s: jax.random.normal(k, shape, f32) * s
    gain = lambda k, shape: 1.0 + 0.01 * jax.random.normal(k, shape, f32)
    return {
        'x': nrm(ks[0], (BATCH, SEQ, D_MODEL), 1.0),
        'c': nrm(ks[1], (BATCH, D_MODEL), 1.0),
        'w_ada': nrm(ks[2], (DEPTH, D_MODEL, 6 * D_MODEL), ADA_INIT * D_MODEL ** -0.5),
        'b_ada': nrm(ks[3], (DEPTH, 6 * D_MODEL), 0.01),
        'w_in': nrm(ks[4], (DEPTH, D_MODEL, IN_COLS), D_MODEL ** -0.5),
        'mla_q_norm': gain(ks[5], (DEPTH, B_Q_RANK)),
        'mla_w_uq': nrm(ks[6], (DEPTH, B_Q_RANK, B_HEADS * (B_NOPE + B_ROPE)), B_Q_RANK ** -0.5),
        'mla_kv_norm': gain(ks[7], (DEPTH, B_KV_RANK)),
        'mla_w_ukv': nrm(ks[8], (DEPTH, B_KV_RANK, B_HEADS * (B_NOPE + B_V)), B_KV_RANK ** -0.5),
        'gqa_q_norm': gain(ks[9], (DEPTH, HEAD_DIM)),
        'gqa_k_norm': gain(ks[10], (DEPTH, HEAD_DIM)),
        'rel_bias': nrm(ks[11], (REL_BUCKETS, A_HEADS), 0.5),
        'w_o': nrm(ks[12], (DEPTH, MIX_OUT, D_MODEL), BETA * MIX_OUT ** -0.5),
        'ln1_g': gain(ks[13], (DEPTH, D_MODEL)),
        'ln1_b': nrm(ks[14], (DEPTH, D_MODEL), 0.01),
        'w1': nrm(ks[15], (DEPTH, D_MODEL, D_FF), D_MODEL ** -0.5),
        'w2': nrm(ks[16], (DEPTH, D_FF, D_MODEL), BETA * D_FF ** -0.5),
        'ln2_g': gain(ks[17], (DEPTH, D_MODEL)),
        'ln2_b': nrm(ks[18], (DEPTH, D_MODEL), 0.01),
    }


def reference(x, c, w_ada, b_ada, w_in, mla_q_norm, mla_w_uq, mla_kv_norm, mla_w_ukv,
              gqa_q_norm, gqa_k_norm, rel_bias, w_o, ln1_g, ln1_b, w1, w2, ln2_g, ln2_b):
    B, S, D = x.shape
    ROWS = S // GRID_W
    t = jnp.arange(S)
    rows = jnp.repeat(jnp.arange(ROWS), GRID_W)
    cols = jnp.tile(jnp.arange(GRID_W), ROWS)
    ang_t = rope_angles(t, B_ROPE)
    ang_row = rope_angles(rows, HEAD_DIM // 2)
    ang_col = rope_angles(cols, HEAD_DIM // 2)
    c_act = jax.nn.silu(c)

    for l in range(DEPTH):
        mod = (c_act @ w_ada[l] + b_ada[l])[:, None, :]
        sh_a, sc_a, g_a, sh_m, sc_m, g_m = jnp.split(mod, 6, axis=-1)

        h = x * (1.0 + sc_a) + sh_a
        proj = h @ w_in[l]
        a_q, a_k, a_v, b_cq, b_ckv, b_kr, c_q, c_k, c_v = jnp.split(proj, IN_SPLITS, axis=-1)
        to_heads = lambda z: z.reshape(B, S, A_HEADS, HEAD_DIM).transpose(0, 2, 1, 3)
        y_a = dilated_mixture(to_heads(a_q), to_heads(a_k), to_heads(a_v), rel_bias)
        y_b = mla_attention(b_cq, b_ckv, b_kr, mla_q_norm[l], mla_w_uq[l],
                            mla_kv_norm[l], mla_w_ukv[l], ang_t)
        y_c = axial_gqa(c_q, c_k, c_v, gqa_q_norm[l], gqa_k_norm[l], ang_row, ang_col)
        y = jnp.concatenate([heads_to_seq(y_a), heads_to_seq(y_b), heads_to_seq(y_c)], -1) @ w_o[l]
        x = layer_norm(ALPHA * x + (1.0 + g_a) * y, ln1_g[l], ln1_b[l])

        h = x * (1.0 + sc_m) + sh_m
        y = jnp.square(jax.nn.relu(h @ w1[l])) @ w2[l]
        x = layer_norm(ALPHA * x + (1.0 + g_m) * y, ln2_g[l], ln2_b[l])
    return x
```

```python
import functools
import math

import numpy as np
import jax
import jax.numpy as jnp
from jax import lax
from jax.experimental import pallas as pl
from jax.experimental.pallas import tpu as pltpu

D_MODEL = 1024
BATCH = 4
SEQ = 4096
DEPTH = 2
HEAD_DIM = 64
A_HEADS = 4
A_PATTERNS = ((128, 1), (512, 4), (2048, 16))
B_HEADS = 4
B_Q_RANK = 384
B_KV_RANK = 256
B_NOPE = 64
B_ROPE = 32
B_V = 64
C_Q_HEADS = 8
C_KV_HEADS = 2
GRID_W = 64
ROPE_THETA = 10000.0
REL_BUCKETS = 32
REL_MAX_DIST = 1024
D_FF = 4 * D_MODEL
EPS = 1e-6
NEG_BIG = -1e30
ALPHA = (2 * DEPTH) ** 0.25
A_W = A_HEADS * HEAD_DIM
IN_COLS = 2208
LOG2E = 1.4426950408889634

F32 = jnp.float32
BF16 = jnp.bfloat16

LANES = 128
VMEM_LIMIT = 56 * 1024 * 1024

OFF_AQ, OFF_AK, OFF_AV = 0, 256, 512
OFF_BCQ, OFF_BCKV, OFF_BKR = 768, 1152, 1408
OFF_CQ, OFF_CK, OFF_CV, OFF_END = 1440, 1952, 2080, 2208

TOK_TILE = 512
A_TILE = 256
A_NCHUNK = 9
A_CENTER = 4
KV_TILE = 512


def _nt_dot(a, b):
    return lax.dot_general(a, b, (((1,), (1,)), ((), ())), preferred_element_type=F32)


def _ada_kernel(c_ref, w_ref, b_ref, o_ref):
    c = c_ref[...]
    ca = (c / (1.0 + jnp.exp(-c))).astype(BF16)
    w = w_ref[0].astype(BF16)
    o_ref[0] = jnp.dot(ca, w, preferred_element_type=F32) + b_ref[0]


def _ada_call(c_pad, w_ada, b_ada):
    tn = 1536
    n = 6 * D_MODEL
    return pl.pallas_call(
        _ada_kernel,
        out_shape=jax.ShapeDtypeStruct((DEPTH, 8, n), F32),
        grid=(DEPTH, n // tn),
        in_specs=[
            pl.BlockSpec((8, D_MODEL), lambda l, j: (0, 0)),
            pl.BlockSpec((1, D_MODEL, tn), lambda l, j: (l, 0, j)),
            pl.BlockSpec((1, 1, tn), lambda l, j: (l, 0, j)),
        ],
        out_specs=pl.BlockSpec((1, 8, tn), lambda l, j: (l, 0, j)),
        compiler_params=pltpu.CompilerParams(vmem_limit_bytes=VMEM_LIMIT),
        name="ada_mod",
    )(c_pad, w_ada, b_ada.reshape(DEPTH, 1, n))


def _proj_kernel(x_ref, mod_ref, winT_ref, wuqT_ref, gq_ref, wukT_ref, wuvT_ref, gkv_ref, tab_ref,
                 aqT_ref, ak_ref, avT_ref, bqT_ref, bk_ref, bvT_ref, cqT_ref, ck_ref, cvT_ref):
    tm = x_ref.shape[0]
    sh = mod_ref[0, 0:1, :]
    sc = mod_ref[0, 1:2, :]
    h = (x_ref[...] * (1.0 + sc) + sh).astype(BF16)

    def seg(r0, r1):
        return _nt_dot(winT_ref[r0:r1, :], h)

    z64 = jnp.zeros((64, tm), F32)
    z64b = jnp.zeros((64, tm), BF16)
    z32b = jnp.zeros((32, tm), BF16)

    aq = seg(OFF_AQ, OFF_AK) * (HEAD_DIM ** -0.5 * LOG2E)
    for hd in range(A_HEADS):
        aqT_ref[0, 128 * hd:128 * hd + 64, :] = aq[64 * hd:64 * hd + 64].astype(BF16)
        aqT_ref[0, 128 * hd + 64:128 * hd + 128, :] = z64b
    ak = seg(OFF_AK, OFF_AV)
    for hd in range(A_HEADS):
        kt = jnp.concatenate([ak[64 * hd:64 * hd + 64], z64], axis=0)
        ak_ref[0, hd] = kt.T.astype(BF16)
    av = seg(OFF_AV, OFF_BCQ).astype(BF16)
    for j in range(tm // A_TILE):
        avT_ref[0, j] = av[:, A_TILE * j:A_TILE * (j + 1)]

    cos_t = tab_ref[0:16, :]
    sin_t = tab_ref[16:32, :]
    cq = seg(OFF_BCQ, OFF_BCKV)
    rq = lax.rsqrt(jnp.mean(cq * cq, axis=0, keepdims=True) + EPS)
    wq = (wuqT_ref[...] * gq_ref[...]).astype(BF16)
    uq = jnp.dot(wq, cq.astype(BF16), preferred_element_type=F32)
    uq = uq * (rq * ((B_NOPE + B_ROPE) ** -0.5 * LOG2E))
    for hd in range(B_HEADS):
        b0 = (B_NOPE + B_ROPE) * hd
        x1 = uq[b0 + 64:b0 + 80]
        x2 = uq[b0 + 80:b0 + 96]
        bqT_ref[0, 128 * hd:128 * hd + 64, :] = uq[b0:b0 + 64].astype(BF16)
        bqT_ref[0, 128 * hd + 64:128 * hd + 80, :] = (x1 * cos_t - x2 * sin_t).astype(BF16)
        bqT_ref[0, 128 * hd + 80:128 * hd + 96, :] = (x1 * sin_t + x2 * cos_t).astype(BF16)
        bqT_ref[0, 128 * hd + 96:128 * hd + 128, :] = z32b
    ckv = seg(OFF_BCKV, OFF_BKR)
    rkv = lax.rsqrt(jnp.mean(ckv * ckv, axis=0, keepdims=True) + EPS)
    ckv_b = ckv.astype(BF16)
    gkv = gkv_ref[...]
    kn = jnp.dot((wukT_ref[...] * gkv).astype(BF16), ckv_b, preferred_element_type=F32) * rkv
    vv = jnp.dot((wuvT_ref[...] * gkv).astype(BF16), ckv_b, preferred_element_type=F32) * rkv
    bvT_ref[0, 0] = vv.astype(BF16)
    kr = seg(OFF_BKR, OFF_CQ)
    kr1 = kr[0:16]
    kr2 = kr[16:32]
    kro = jnp.concatenate([kr1 * cos_t - kr2 * sin_t, kr1 * sin_t + kr2 * cos_t,
                           jnp.zeros((32, tm), F32)], axis=0)
    for hd in range(B_HEADS):
        kt = jnp.concatenate([kn[64 * hd:64 * hd + 64], kro], axis=0)
        bk_ref[0, hd] = kt.T.astype(BF16)

    def normrope(xh, c, s):
        r = lax.rsqrt(jnp.mean(xh * xh, axis=0, keepdims=True) + EPS)
        rot = jnp.concatenate([xh[16:32], xh[0:16], xh[48:64], xh[32:48]], axis=0)
        return (xh * c + rot * s) * r

    cqc = tab_ref[32:96, :]
    cqs = tab_ref[96:160, :]
    cq2 = seg(OFF_CQ, OFF_CK)
    for hd in range(C_Q_HEADS):
        cqT_ref[0, 128 * hd:128 * hd + 64, :] = normrope(cq2[64 * hd:64 * hd + 64], cqc, cqs).astype(BF16)
        cqT_ref[0, 128 * hd + 64:128 * hd + 128, :] = z64b
    ckc = tab_ref[160:224, :]
    cks = tab_ref[224:288, :]
    ck2 = seg(OFF_CK, OFF_CV)
    for hd in range(C_KV_HEADS):
        kt = jnp.concatenate([normrope(ck2[64 * hd:64 * hd + 64], ckc, cks), z64], axis=0)
        ck_ref[0, hd] = kt.T.astype(BF16)
    cvT_ref[0, 0] = seg(OFF_CV, OFF_END).astype(BF16)


def _proj_call(x2d, mod, winT, wuqT, gq, wukT, wuvT, gkv, tab):
    tm = TOK_TILE
    T = x2d.shape[0]
    nj = SEQ // tm
    const2 = lambda i: (0, 0)
    bj3 = lambda i: (i // nj, 0, i % nj)
    bj4 = lambda i: (i // nj, 0, i % nj, 0)
    cj4 = lambda i: (i // nj, i % nj, 0, 0)
    out_shape = (
        jax.ShapeDtypeStruct((BATCH, 512, SEQ), BF16),
        jax.ShapeDtypeStruct((BATCH, A_HEADS, SEQ, 128), BF16),
        jax.ShapeDtypeStruct((BATCH, SEQ // A_TILE, 256, A_TILE), BF16),
        jax.ShapeDtypeStruct((BATCH, 512, SEQ), BF16),
        jax.ShapeDtypeStruct((BATCH, B_HEADS, SEQ, 128), BF16),
        jax.ShapeDtypeStruct((BATCH, SEQ // KV_TILE, 256, KV_TILE), BF16),
        jax.ShapeDtypeStruct((BATCH, 1024, SEQ), BF16),
        jax.ShapeDtypeStruct((BATCH, C_KV_HEADS, SEQ, 128), BF16),
        jax.ShapeDtypeStruct((BATCH, SEQ // KV_TILE, 128, KV_TILE), BF16),
    )
    out_specs = (
        pl.BlockSpec((1, 512, tm), bj3),
        pl.BlockSpec((1, A_HEADS, tm, 128), bj4),
        pl.BlockSpec((1, tm // A_TILE, 256, A_TILE), cj4),
        pl.BlockSpec((1, 512, tm), bj3),
        pl.BlockSpec((1, B_HEADS, tm, 128), bj4),
        pl.BlockSpec((1, tm // KV_TILE, 256, KV_TILE), cj4),
        pl.BlockSpec((1, 1024, tm), bj3),
        pl.BlockSpec((1, C_KV_HEADS, tm, 128), bj4),
        pl.BlockSpec((1, tm // KV_TILE, 128, KV_TILE), cj4),
    )
    in_specs = [
        pl.BlockSpec((tm, D_MODEL), lambda i: (i, 0)),
        pl.BlockSpec((1, 6, D_MODEL), lambda i: (i // nj, 0, 0)),
        pl.BlockSpec((IN_COLS, D_MODEL), const2),
        pl.BlockSpec((B_Q_RANK, B_Q_RANK), const2),
        pl.BlockSpec((1, B_Q_RANK), const2),
        pl.BlockSpec((256, B_KV_RANK), const2),
        pl.BlockSpec((256, B_KV_RANK), const2),
        pl.BlockSpec((1, B_KV_RANK), const2),
        pl.BlockSpec((288, tm), lambda i: (0, i % nj)),
    ]
    return pl.pallas_call(
        _proj_kernel,
        out_shape=out_shape,
        grid=(T // tm,),
        in_specs=in_specs,
        out_specs=out_specs,
        compiler_params=pltpu.CompilerParams(vmem_limit_bytes=VMEM_LIMIT),
        name="in_proj",
    )(x2d, mod, winT, wuqT, gq, wukT, wuvT, gkv, tab)


def _attn_step(k, q_cat, v, bias, m_ref, acc_ref):
    s = jnp.dot(k, q_cat, preferred_element_type=F32)
    if bias is not None:
        s = s + bias
    m_old = m_ref[...]
    m_new = jnp.maximum(m_old, jnp.max(s, axis=0, keepdims=True))
    p = jnp.exp2(s - m_new).astype(BF16)
    alpha = jnp.exp2(m_old - m_new)
    v_aug = jnp.concatenate([v, jnp.ones((16, v.shape[1]), BF16)], axis=0)
    pv = jnp.dot(v_aug, p, preferred_element_type=F32)
    acc_ref[...] = alpha * acc_ref[...] + pv
    m_ref[...] = m_new


def _attn_finish(o_ref, acc_ref, groups, tq):
    acc = acc_ref[...]
    o = acc[0:64] / acc[64:65]
    for g in range(groups):
        o_ref[0, 64 * g:64 * g + 64, :] = o[:, tq * g:tq * (g + 1)].astype(o_ref.dtype)


def _full_attn_kernel(qT_ref, k_ref, vT_ref, o_ref, m_ref, acc_ref, *, groups, tq, tk):
    q_cat = jnp.concatenate([qT_ref[0, 128 * g:128 * (g + 1), :] for g in range(groups)], axis=1)
    m_ref[...] = jnp.full(m_ref.shape, NEG_BIG, F32)
    acc_ref[...] = jnp.zeros(acc_ref.shape, F32)

    def body(c, carry):
        k = k_ref[0, 0, pl.ds(pl.multiple_of(c * tk, tk), tk), :]
        _attn_step(k, q_cat, vT_ref[0, c], None, m_ref, acc_ref)
        return carry

    lax.fori_loop(0, SEQ // tk, body, 0)
    _attn_finish(o_ref, acc_ref, groups, tq)


def _full_attn_call(qT, k, vT, *, groups, tq, name):
    tk = KV_TILE
    hkv = k.shape[1]
    n = groups * tq
    kern = functools.partial(_full_attn_kernel, groups=groups, tq=tq, tk=tk)
    return pl.pallas_call(
        kern,
        out_shape=jax.ShapeDtypeStruct((BATCH, hkv * groups * 64, SEQ), BF16),
        grid=(BATCH, hkv, SEQ // tq),
        in_specs=[
            pl.BlockSpec((1, groups * 128, tq), lambda b, g, i: (b, g, i)),
            pl.BlockSpec((1, 1, SEQ, 128), lambda b, g, i: (b, g, 0, 0)),
            pl.BlockSpec((1, SEQ // tk, 64, tk), lambda b, g, i: (b, 0, g, 0)),
        ],
        out_specs=pl.BlockSpec((1, groups * 64, tq), lambda b, g, i: (b, g, i)),
        scratch_shapes=[pltpu.VMEM((1, n), F32), pltpu.VMEM((80, n), F32)],
        compiler_params=pltpu.CompilerParams(vmem_limit_bytes=VMEM_LIMIT),
        name=name,
    )(qT, k, vT)


def _dil_attn_kernel(qT_ref, k_ref, vT_ref, bias_ref, o_ref, m_ref, acc_ref):
    t = A_TILE
    nq = SEQ // t
    i = pl.program_id(2)
    q_cat = qT_ref[0]
    m_ref[...] = jnp.full(m_ref.shape, NEG_BIG, F32)
    acc_ref[...] = jnp.zeros(acc_ref.shape, F32)
    c_lo = jnp.maximum(0, A_CENTER - i)
    c_hi = jnp.minimum(A_NCHUNK, A_CENTER + nq - i)

    def body(c, carry):
        kc = i + c - A_CENTER
        k = k_ref[0, 0, pl.ds(pl.multiple_of(kc * t, t), t), :]
        _attn_step(k, q_cat, vT_ref[0, kc], bias_ref[0, c], m_ref, acc_ref)
        return carry

    lax.fori_loop(c_lo, c_hi, body, 0)
    _attn_finish(o_ref, acc_ref, 1, t)


def _dil_attn_call(qT, k, vT, bias):
    t = A_TILE
    return pl.pallas_call(
        _dil_attn_kernel,
        out_shape=jax.ShapeDtypeStruct((BATCH, A_HEADS * 64, SEQ), BF16),
        grid=(BATCH, A_HEADS, SEQ // t),
        in_specs=[
            pl.BlockSpec((1, 128, t), lambda b, h, i: (b, h, i)),
            pl.BlockSpec((1, 1, SEQ, 128), lambda b, h, i: (b, h, 0, 0)),
            pl.BlockSpec((1, SEQ // t, 64, t), lambda b, h, i: (b, 0, h, 0)),
            pl.BlockSpec((1, A_NCHUNK, t, t), lambda b, h, i: (h, 0, 0, 0)),
        ],
        out_specs=pl.BlockSpec((1, 64, t), lambda b, h, i: (b, h, i)),
        scratch_shapes=[pltpu.VMEM((1, t), F32), pltpu.VMEM((80, t), F32)],
        compiler_params=pltpu.CompilerParams(vmem_limit_bytes=VMEM_LIMIT),
        name="dilated_attn",
    )(qT, k, vT, bias)


def _layer_norm(z, g, b):
    mu = jnp.mean(z, axis=-1, keepdims=True)
    d = z - mu
    var = jnp.mean(d * d, axis=-1, keepdims=True)
    return d * lax.rsqrt(var + EPS) * g + b


def _mlp_kernel(x_ref, mod_ref, aoT_ref, boT_ref, coT_ref, wo_ref, ln1g_ref, ln1b_ref,
                w1_ref, w2_ref, ln2g_ref, ln2b_ref, o_ref):
    g_a = mod_ref[0, 2:3, :]
    sh_m = mod_ref[0, 3:4, :]
    sc_m = mod_ref[0, 4:5, :]
    g_m = mod_ref[0, 5:6, :]
    catT = jnp.concatenate([aoT_ref[0], boT_ref[0], coT_ref[0]], axis=0)
    cat = catT.astype(F32).T.astype(BF16)
    y = jnp.dot(cat, wo_ref[...], preferred_element_type=F32)
    x1 = _layer_norm(ALPHA * x_ref[...] + (1.0 + g_a) * y, ln1g_ref[...], ln1b_ref[...])
    h = (x1 * (1.0 + sc_m) + sh_m).astype(BF16)
    y2 = None
    fc = 1024
    for j in range(D_FF // fc):
        u = jnp.dot(h, w1_ref[:, fc * j:fc * (j + 1)], preferred_element_type=F32)
        u = jnp.maximum(u, 0.0)
        u = (u * u).astype(BF16)
        part = jnp.dot(u, w2_ref[fc * j:fc * (j + 1), :], preferred_element_type=F32)
        y2 = part if y2 is None else y2 + part
    o_ref[...] = _layer_norm(ALPHA * x1 + (1.0 + g_m) * y2, ln2g_ref[...], ln2b_ref[...])


def _mlp_call(x2d, mod, aoT, boT, coT, wo, ln1g, ln1b, w1, w2, ln2g, ln2b):
    tm = TOK_TILE
    T = x2d.shape[0]
    nj = SEQ // tm
    const2 = lambda i: (0, 0)
    bj3 = lambda i: (i // nj, 0, i % nj)
    one = pl.Buffered(1)
    in_specs = [
        pl.BlockSpec((tm, D_MODEL), lambda i: (i, 0)),
        pl.BlockSpec((1, 6, D_MODEL), lambda i: (i // nj, 0, 0)),
        pl.BlockSpec((1, 256, tm), bj3),
        pl.BlockSpec((1, 256, tm), bj3),
        pl.BlockSpec((1, 512, tm), bj3),
        pl.BlockSpec((D_MODEL, D_MODEL), const2, pipeline_mode=one),
        pl.BlockSpec((1, D_MODEL), const2),
        pl.BlockSpec((1, D_MODEL), const2),
        pl.BlockSpec((D_MODEL, D_FF), const2, pipeline_mode=one),
        pl.BlockSpec((D_FF, D_MODEL), const2, pipeline_mode=one),
        pl.BlockSpec((1, D_MODEL), const2),
        pl.BlockSpec((1, D_MODEL), const2),
    ]
    return pl.pallas_call(
        _mlp_kernel,
        out_shape=jax.ShapeDtypeStruct((T, D_MODEL), F32),
        grid=(T // tm,),
        in_specs=in_specs,
        out_specs=pl.BlockSpec((tm, D_MODEL), lambda i: (i, 0)),
        compiler_params=pltpu.CompilerParams(vmem_limit_bytes=VMEM_LIMIT),
        name="out_mlp",
    )(x2d, mod, aoT, boT, coT, wo, ln1g, ln1b, w1, w2, ln2g, ln2b)


def _rope_angles(pos, dim):
    inv = ROPE_THETA ** (-jnp.arange(0, dim, 2, dtype=F32) / dim)
    return pos.astype(F32)[:, None] * inv[None, :]


def _t5_bucket(rel):
    nb = REL_BUCKETS // 2
    max_exact = nb // 2
    sign = jnp.where(rel > 0, nb, 0)
    n = jnp.abs(rel)
    nf = jnp.maximum(n, 1).astype(F32)
    large = max_exact + (jnp.log(nf / max_exact) / math.log(REL_MAX_DIST / max_exact)
                         * (nb - max_exact)).astype(jnp.int32)
    large = jnp.minimum(large, nb - 1)
    return sign + jnp.where(n < max_exact, n, large)


def _dilated_bias_tiles(rel_bias):
    t = A_TILE
    span = (A_CENTER + 1) * t
    deltas = np.arange(-span + 1, span)
    mult = np.zeros(deltas.shape, np.int32)
    for (w, d) in A_PATTERNS:
        half = w // (2 * d)
        mult += ((deltas % d == 0) & (np.abs(deltas) <= half * d)).astype(np.int32)
    bucket = _t5_bucket(jnp.asarray(deltas, jnp.int32))
    logm = jnp.asarray(np.log(np.maximum(mult, 1)), F32)
    tab = jnp.where(jnp.asarray(mult > 0)[:, None],
                    (rel_bias[bucket] + logm[:, None]) * LOG2E, NEG_BIG)
    c = np.arange(A_NCHUNK)[:, None, None]
    i = np.arange(t)[None, :, None]
    j = np.arange(t)[None, None, :]
    idx = (c - A_CENTER) * t + i - j + span - 1
    return jnp.moveaxis(tab[jnp.asarray(idx, jnp.int32)], -1, 0)


def _rope_tables(gq, gk):
    t = jnp.arange(SEQ)
    ang_t = _rope_angles(t, B_ROPE)
    ang_row = _rope_angles(t // GRID_W, HEAD_DIM // 2)
    ang_col = _rope_angles(t % GRID_W, HEAD_DIM // 2)
    cos64 = jnp.concatenate([jnp.cos(ang_row)] * 2 + [jnp.cos(ang_col)] * 2, axis=1)
    sin_r, sin_c = jnp.sin(ang_row), jnp.sin(ang_col)
    sin64 = jnp.concatenate([-sin_r, sin_r, -sin_c, sin_c], axis=1)
    perm = np.concatenate([np.arange(16, 32), np.arange(0, 16), np.arange(48, 64), np.arange(32, 48)])

    def with_gain(g, scale):
        return (cos64 * g[None, :] * scale).T, (sin64 * g[perm][None, :] * scale).T

    qc, qs = with_gain(gq, HEAD_DIM ** -0.5 * LOG2E)
    kc, ks = with_gain(gk, 1.0)
    return jnp.concatenate([jnp.cos(ang_t).T, jnp.sin(ang_t).T, qc, qs, kc, ks], axis=0)


def kernel(x, c, w_ada, b_ada, w_in, mla_q_norm, mla_w_uq, mla_kv_norm, mla_w_ukv,
           gqa_q_norm, gqa_k_norm, rel_bias, w_o, ln1_g, ln1_b, w1, w2, ln2_g, ln2_b):
    B, S, D = x.shape
    c_pad = jnp.zeros((8, D), F32).at[:B].set(c)
    mod_all = _ada_call(c_pad, w_ada, b_ada)[:, :B].reshape(DEPTH, B, 6, D)
    bias = _dilated_bias_tiles(rel_bias)

    x2d = x.reshape(B * S, D)
    for l in range(DEPTH):
        winT = w_in[l].T.astype(BF16)
        wuqT = mla_w_uq[l].T
        wukv = mla_w_ukv[l].reshape(B_KV_RANK, B_HEADS, B_NOPE + B_V)
        wukT = wukv[:, :, :B_NOPE].reshape(B_KV_RANK, B_HEADS * B_NOPE).T
        wuvT = wukv[:, :, B_NOPE:].reshape(B_KV_RANK, B_HEADS * B_V).T
        tab = _rope_tables(gqa_q_norm[l], gqa_k_norm[l])
        aqT, ak, avT, bqT, bk, bvT, cqT, ck, cvT = _proj_call(
            x2d, mod_all[l], winT, wuqT, mla_q_norm[l][None, :], wukT, wuvT,
            mla_kv_norm[l][None, :], tab)
        aoT = _dil_attn_call(aqT, ak, avT, bias)
        boT = _full_attn_call(bqT, bk, bvT, groups=1, tq=1024, name="mla_attn")
        coT = _full_attn_call(cqT, ck, cvT, groups=C_Q_HEADS // C_KV_HEADS, tq=256, name="gqa_attn")
        x2d = _mlp_call(x2d, mod_all[l], aoT, boT, coT, w_o[l].astype(BF16),
                        ln1_g[l][None, :], ln1_b[l][None, :], w1[l].astype(BF16), w2[l].astype(BF16),
                        ln2_g[l][None, :], ln2_b[l][None, :])
    return x2d.reshape(B, S, D)
```

```python
import functools
import math

import numpy as np
import jax
import jax.numpy as jnp
from jax import lax
from jax.experimental import pallas as pl
from jax.experimental.pallas import tpu as pltpu

D_MODEL = 1024
BATCH = 4
SEQ = 4096
DEPTH = 2
HEAD_DIM = 64
A_HEADS = 4
A_PATTERNS = ((128, 1), (512, 4), (2048, 16))
B_HEADS = 4
B_Q_RANK = 384
B_KV_RANK = 256
B_NOPE = 64
B_ROPE = 32
B_V = 64
C_Q_HEADS = 8
C_KV_HEADS = 2
GRID_W = 64
ROPE_THETA = 10000.0
REL_BUCKETS = 32
REL_MAX_DIST = 1024
D_FF = 4 * D_MODEL
EPS = 1e-6
NEG_BIG = -1e30
ALPHA = (2 * DEPTH) ** 0.25
A_W = A_HEADS * HEAD_DIM
IN_COLS = 2208
LOG2E = 1.4426950408889634

F32 = jnp.float32
BF16 = jnp.bfloat16

LANES = 128
VMEM_LIMIT = 56 * 1024 * 1024

OFF_AQ, OFF_AK, OFF_AV = 0, 256, 512
OFF_BCQ, OFF_BCKV, OFF_BKR = 768, 1152, 1408
OFF_CQ, OFF_CK, OFF_CV, OFF_END = 1440, 1952, 2080, 2208

TOK_TILE = 512
A_TILE = 256
A_NCHUNK = 9
A_CENTER = 4
KV_TILE = 512
STRIP = 512


def _nt_dot(a, b):
    return lax.dot_general(a, b, (((1,), (1,)), ((), ())), preferred_element_type=F32)


def _ada_kernel(c_ref, w_ref, b_ref, o_ref):
    c = c_ref[...]
    ca = (c / (1.0 + jnp.exp(-c))).astype(BF16)
    w = w_ref[0].astype(BF16)
    o_ref[0] = jnp.dot(ca, w, preferred_element_type=F32) + b_ref[0]


def _ada_call(c_pad, w_ada, b_ada):
    tn = 1536
    n = 6 * D_MODEL
    return pl.pallas_call(
        _ada_kernel,
        out_shape=jax.ShapeDtypeStruct((DEPTH, 8, n), F32),
        grid=(DEPTH, n // tn),
        in_specs=[
            pl.BlockSpec((8, D_MODEL), lambda l, j: (0, 0)),
            pl.BlockSpec((1, D_MODEL, tn), lambda l, j: (l, 0, j)),
            pl.BlockSpec((1, 1, tn), lambda l, j: (l, 0, j)),
        ],
        out_specs=pl.BlockSpec((1, 8, tn), lambda l, j: (l, 0, j)),
        compiler_params=pltpu.CompilerParams(vmem_limit_bytes=VMEM_LIMIT),
        name="ada_mod",
    )(c_pad, w_ada, b_ada.reshape(DEPTH, 1, n))


def _proj_kernel(x_ref, mod_ref, winT_ref, wuqT_ref, gq_ref, wukT_ref, wuvT_ref, gkv_ref, tab_ref,
                 aqT_ref, ak_ref, avT_ref, bqT_ref, bk_ref, bvT_ref, cqT_ref, ck_ref, cvT_ref):
    tm = x_ref.shape[0]
    sh = mod_ref[0, 0:1, :]
    sc = mod_ref[0, 1:2, :]
    h = (x_ref[...] * (1.0 + sc) + sh).astype(BF16)

    def seg(r0, r1):
        return _nt_dot(winT_ref[r0:r1, :], h)

    z64 = jnp.zeros((64, tm), F32)
    z64b = jnp.zeros((64, tm), BF16)
    z32b = jnp.zeros((32, tm), BF16)

    aq = seg(OFF_AQ, OFF_AK) * (HEAD_DIM ** -0.5 * LOG2E)
    for hd in range(A_HEADS):
        aqT_ref[0, 128 * hd:128 * hd + 64, :] = aq[64 * hd:64 * hd + 64].astype(BF16)
        aqT_ref[0, 128 * hd + 64:128 * hd + 128, :] = z64b
    ak = seg(OFF_AK, OFF_AV)
    for hd in range(A_HEADS):
        kt = jnp.concatenate([ak[64 * hd:64 * hd + 64], z64], axis=0)
        ak_ref[0, hd] = kt.T.astype(BF16)
    av = seg(OFF_AV, OFF_BCQ).astype(BF16)
    for j in range(tm // A_TILE):
        avT_ref[0, j] = av[:, A_TILE * j:A_TILE * (j + 1)]

    cos_t = tab_ref[0:16, :]
    sin_t = tab_ref[16:32, :]
    cq = seg(OFF_BCQ, OFF_BCKV)
    rq = lax.rsqrt(jnp.mean(cq * cq, axis=0, keepdims=True) + EPS)
    wq = (wuqT_ref[...] * gq_ref[...]).astype(BF16)
    uq = jnp.dot(wq, cq.astype(BF16), preferred_element_type=F32)
    uq = uq * (rq * ((B_NOPE + B_ROPE) ** -0.5 * LOG2E))
    for hd in range(B_HEADS):
        b0 = (B_NOPE + B_ROPE) * hd
        x1 = uq[b0 + 64:b0 + 80]
        x2 = uq[b0 + 80:b0 + 96]
        bqT_ref[0, 128 * hd:128 * hd + 64, :] = uq[b0:b0 + 64].astype(BF16)
        bqT_ref[0, 128 * hd + 64:128 * hd + 80, :] = (x1 * cos_t - x2 * sin_t).astype(BF16)
        bqT_ref[0, 128 * hd + 80:128 * hd + 96, :] = (x1 * sin_t + x2 * cos_t).astype(BF16)
        bqT_ref[0, 128 * hd + 96:128 * hd + 128, :] = z32b
    ckv = seg(OFF_BCKV, OFF_BKR)
    rkv = lax.rsqrt(jnp.mean(ckv * ckv, axis=0, keepdims=True) + EPS)
    ckv_b = ckv.astype(BF16)
    gkv = gkv_ref[...]
    kn = jnp.dot((wukT_ref[...] * gkv).astype(BF16), ckv_b, preferred_element_type=F32) * rkv
    vv = jnp.dot((wuvT_ref[...] * gkv).astype(BF16), ckv_b, preferred_element_type=F32) * rkv
    bvT_ref[0, 0] = vv.astype(BF16)
    kr = seg(OFF_BKR, OFF_CQ)
    kr1 = kr[0:16]
    kr2 = kr[16:32]
    kro = jnp.concatenate([kr1 * cos_t - kr2 * sin_t, kr1 * sin_t + kr2 * cos_t,
                           jnp.zeros((32, tm), F32)], axis=0)
    for hd in range(B_HEADS):
        kt = jnp.concatenate([kn[64 * hd:64 * hd + 64], kro], axis=0)
        bk_ref[0, hd] = kt.T.astype(BF16)

    def normrope(xh, c, s):
        r = lax.rsqrt(jnp.mean(xh * xh, axis=0, keepdims=True) + EPS)
        rot = jnp.concatenate([xh[16:32], xh[0:16], xh[48:64], xh[32:48]], axis=0)
        return (xh * c + rot * s) * r

    cqc = tab_ref[32:96, :]
    cqs = tab_ref[96:160, :]
    cq2 = seg(OFF_CQ, OFF_CK)
    for hd in range(C_Q_HEADS):
        cqT_ref[0, 128 * hd:128 * hd + 64, :] = normrope(cq2[64 * hd:64 * hd + 64], cqc, cqs).astype(BF16)
        cqT_ref[0, 128 * hd + 64:128 * hd + 128, :] = z64b
    ckc = tab_ref[160:224, :]
    cks = tab_ref[224:288, :]
    ck2 = seg(OFF_CK, OFF_CV)
    for hd in range(C_KV_HEADS):
        kt = jnp.concatenate([normrope(ck2[64 * hd:64 * hd + 64], ckc, cks), z64], axis=0)
        ck_ref[0, hd] = kt.T.astype(BF16)
    cvT_ref[0, 0] = seg(OFF_CV, OFF_END).astype(BF16)


def _proj_call(x2d, mod, winT, wuqT, gq, wukT, wuvT, gkv, tab):
    tm = TOK_TILE
    T = x2d.shape[0]
    nj = SEQ // tm
    const2 = lambda i: (0, 0)
    bj3 = lambda i: (i // nj, 0, i % nj)
    bj4 = lambda i: (i // nj, 0, i % nj, 0)
    cj4 = lambda i: (i // nj, i % nj, 0, 0)
    out_shape = (
        jax.ShapeDtypeStruct((BATCH, 512, SEQ), BF16),
        jax.ShapeDtypeStruct((BATCH, A_HEADS, SEQ, 128), BF16),
        jax.ShapeDtypeStruct((BATCH, SEQ // A_TILE, 256, A_TILE), BF16),
        jax.ShapeDtypeStruct((BATCH, 512, SEQ), BF16),
        jax.ShapeDtypeStruct((BATCH, B_HEADS, SEQ, 128), BF16),
        jax.ShapeDtypeStruct((BATCH, SEQ // KV_TILE, 256, KV_TILE), BF16),
        jax.ShapeDtypeStruct((BATCH, 1024, SEQ), BF16),
        jax.ShapeDtypeStruct((BATCH, C_KV_HEADS, SEQ, 128), BF16),
        jax.ShapeDtypeStruct((BATCH, SEQ // KV_TILE, 128, KV_TILE), BF16),
    )
    out_specs = (
        pl.BlockSpec((1, 512, tm), bj3),
        pl.BlockSpec((1, A_HEADS, tm, 128), bj4),
        pl.BlockSpec((1, tm // A_TILE, 256, A_TILE), cj4),
        pl.BlockSpec((1, 512, tm), bj3),
        pl.BlockSpec((1, B_HEADS, tm, 128), bj4),
        pl.BlockSpec((1, tm // KV_TILE, 256, KV_TILE), cj4),
        pl.BlockSpec((1, 1024, tm), bj3),
        pl.BlockSpec((1, C_KV_HEADS, tm, 128), bj4),
        pl.BlockSpec((1, tm // KV_TILE, 128, KV_TILE), cj4),
    )
    in_specs = [
        pl.BlockSpec((tm, D_MODEL), lambda i: (i, 0)),
        pl.BlockSpec((1, 6, D_MODEL), lambda i: (i // nj, 0, 0)),
        pl.BlockSpec((IN_COLS, D_MODEL), const2),
        pl.BlockSpec((B_Q_RANK, B_Q_RANK), const2),
        pl.BlockSpec((1, B_Q_RANK), const2),
        pl.BlockSpec((256, B_KV_RANK), const2),
        pl.BlockSpec((256, B_KV_RANK), const2),
        pl.BlockSpec((1, B_KV_RANK), const2),
        pl.BlockSpec((288, tm), lambda i: (0, i % nj)),
    ]
    return pl.pallas_call(
        _proj_kernel,
        out_shape=out_shape,
        grid=(T // tm,),
        in_specs=in_specs,
        out_specs=out_specs,
        compiler_params=pltpu.CompilerParams(vmem_limit_bytes=VMEM_LIMIT),
        name="in_proj",
    )(x2d, mod, winT, wuqT, gq, wukT, wuvT, gkv, tab)


def _softmax_pv(s, v_aug, sl, m_ref, acc_ref):
    m_old = m_ref[:, sl]
    m_new = jnp.maximum(m_old, jnp.max(s, axis=0, keepdims=True))
    p = jnp.exp2(s - m_new).astype(BF16)
    alpha = jnp.exp2(m_old - m_new)
    pv = jnp.dot(v_aug, p, preferred_element_type=F32)
    acc_ref[:, sl] = alpha * acc_ref[:, sl] + pv
    m_ref[:, sl] = m_new


def _with_ones_rows(v):
    return jnp.concatenate([v, jnp.ones((16, v.shape[1]), BF16)], axis=0)


def _attn_finish(o_ref, acc_ref, groups, tq):
    acc = acc_ref[...]
    o = acc[0:64] / acc[64:65]
    for g in range(groups):
        o_ref[0, 64 * g:64 * g + 64, :] = o[:, tq * g:tq * (g + 1)].astype(o_ref.dtype)


def _full_attn_kernel(qT_ref, k_ref, vT_ref, o_ref, m_ref, acc_ref, s_ref, *, groups, tq, tk):
    n = groups * tq
    w = min(n, STRIP)
    nstrip = n // w
    nchunk = SEQ // tk
    q_cat = jnp.concatenate([qT_ref[0, 128 * g:128 * (g + 1), :] for g in range(groups)], axis=1)
    m_ref[...] = jnp.full(m_ref.shape, NEG_BIG, F32)
    acc_ref[...] = jnp.zeros(acc_ref.shape, F32)

    def scores(c, j):
        k = k_ref[0, 0, pl.ds(pl.multiple_of(c * tk, tk), tk), :]
        return jnp.dot(k, q_cat[:, w * j:w * (j + 1)], preferred_element_type=F32)

    s_ref[...] = scores(0, 0)

    def body(c, carry):
        v_aug = _with_ones_rows(vT_ref[0, c])
        s_cur = s_ref[...]
        for j in range(nstrip):
            if j + 1 < nstrip:
                s_next = scores(c, j + 1)
            else:
                s_next = scores(jnp.minimum(c + 1, nchunk - 1), 0)
            _softmax_pv(s_cur, v_aug, slice(w * j, w * (j + 1)), m_ref, acc_ref)
            s_cur = s_next
        s_ref[...] = s_cur
        return carry

    lax.fori_loop(0, nchunk, body, 0, unroll=True)
    _attn_finish(o_ref, acc_ref, groups, tq)


def _full_attn_call(qT, k, vT, *, groups, tq, name):
    tk = KV_TILE
    hkv = k.shape[1]
    n = groups * tq
    kern = functools.partial(_full_attn_kernel, groups=groups, tq=tq, tk=tk)
    return pl.pallas_call(
        kern,
        out_shape=jax.ShapeDtypeStruct((BATCH, hkv * groups * 64, SEQ), BF16),
        grid=(BATCH, hkv, SEQ // tq),
        in_specs=[
            pl.BlockSpec((1, groups * 128, tq), lambda b, g, i: (b, g, i)),
            pl.BlockSpec((1, 1, SEQ, 128), lambda b, g, i: (b, g, 0, 0)),
            pl.BlockSpec((1, SEQ // tk, 64, tk), lambda b, g, i: (b, 0, g, 0)),
        ],
        out_specs=pl.BlockSpec((1, groups * 64, tq), lambda b, g, i: (b, g, i)),
        scratch_shapes=[pltpu.VMEM((1, n), F32), pltpu.VMEM((80, n), F32),
                        pltpu.VMEM((tk, min(n, STRIP)), F32)],
        compiler_params=pltpu.CompilerParams(vmem_limit_bytes=VMEM_LIMIT),
        name=name,
    )(qT, k, vT)


def _dil_attn_kernel(qT_ref, k_ref, vT_ref, bias_ref, o_ref):
    t = A_TILE
    nq = SEQ // t
    i = pl.program_id(1)
    first = jnp.clip(i - A_CENTER, 0, nq - A_NCHUNK)

    def scores(h):
        q = qT_ref[0, 128 * h:128 * (h + 1), :]
        out = []
        for r in range(A_NCHUNK):
            kt = first + r
            d = kt - i + A_CENTER
            bi = jnp.where((d >= 0) & (d < A_NCHUNK), d, A_NCHUNK)
            k = k_ref[0, h, pl.ds(pl.multiple_of(kt * t, t), t), :]
            out.append(jnp.dot(k, q, preferred_element_type=F32) + bias_ref[h, bi])
        return out

    def finish(h, s_list):
        m = functools.reduce(jnp.maximum, s_list)
        m = jnp.max(m, axis=0, keepdims=True)
        acc = None
        for r in range(A_NCHUNK):
            p = jnp.exp2(s_list[r] - m).astype(BF16)
            v_aug = _with_ones_rows(vT_ref[0, first + r, 64 * h:64 * (h + 1), :])
            pv = jnp.dot(v_aug, p, preferred_element_type=F32)
            acc = pv if acc is None else acc + pv
        o_ref[0, 64 * h:64 * (h + 1), :] = (acc[0:64] / acc[64:65]).astype(o_ref.dtype)

    s_next = scores(0)
    for h in range(A_HEADS):
        s_cur = s_next
        if h + 1 < A_HEADS:
            s_next = scores(h + 1)
        finish(h, s_cur)


def _dil_attn_call(qT, k, vT, bias):
    t = A_TILE
    return pl.pallas_call(
        _dil_attn_kernel,
        out_shape=jax.ShapeDtypeStruct((BATCH, A_HEADS * 64, SEQ), BF16),
        grid=(BATCH, SEQ // t),
        in_specs=[
            pl.BlockSpec((1, A_HEADS * 128, t), lambda b, i: (b, 0, i)),
            pl.BlockSpec((1, A_HEADS, SEQ, 128), lambda b, i: (b, 0, 0, 0)),
            pl.BlockSpec((1, SEQ // t, A_HEADS * 64, t), lambda b, i: (b, 0, 0, 0)),
            pl.BlockSpec((A_HEADS, A_NCHUNK + 1, t, t), lambda b, i: (0, 0, 0, 0),
                         pipeline_mode=pl.Buffered(1)),
        ],
        out_specs=pl.BlockSpec((1, A_HEADS * 64, t), lambda b, i: (b, 0, i)),
        compiler_params=pltpu.CompilerParams(vmem_limit_bytes=VMEM_LIMIT),
        name="dilated_attn",
    )(qT, k, vT, bias)


def _layer_norm(z, g, b):
    mu = jnp.mean(z, axis=-1, keepdims=True)
    d = z - mu
    var = jnp.mean(d * d, axis=-1, keepdims=True)
    return d * lax.rsqrt(var + EPS) * g + b


def _mlp_kernel(x_ref, mod_ref, aoT_ref, boT_ref, coT_ref, wo_ref, ln1g_ref, ln1b_ref,
                w1_ref, w2_ref, ln2g_ref, ln2b_ref, o_ref):
    g_a = mod_ref[0, 2:3, :]
    sh_m = mod_ref[0, 3:4, :]
    sc_m = mod_ref[0, 4:5, :]
    g_m = mod_ref[0, 5:6, :]
    catT = jnp.concatenate([aoT_ref[0], boT_ref[0], coT_ref[0]], axis=0)
    cat = catT.astype(F32).T.astype(BF16)
    y = jnp.dot(cat, wo_ref[...], preferred_element_type=F32)
    x1 = _layer_norm(ALPHA * x_ref[...] + (1.0 + g_a) * y, ln1g_ref[...], ln1b_ref[...])
    h = (x1 * (1.0 + sc_m) + sh_m).astype(BF16)
    y2 = None
    fc = 1024
    for j in range(D_FF // fc):
        u = jnp.dot(h, w1_ref[:, fc * j:fc * (j + 1)], preferred_element_type=F32)
        u = jnp.maximum(u, 0.0)
        u = (u * u).astype(BF16)
        part = jnp.dot(u, w2_ref[fc * j:fc * (j + 1), :], preferred_element_type=F32)
        y2 = part if y2 is None else y2 + part
    o_ref[...] = _layer_norm(ALPHA * x1 + (1.0 + g_m) * y2, ln2g_ref[...], ln2b_ref[...])


def _mlp_call(x2d, mod, aoT, boT, coT, wo, ln1g, ln1b, w1, w2, ln2g, ln2b):
    tm = TOK_TILE
    T = x2d.shape[0]
    nj = SEQ // tm
    const2 = lambda i: (0, 0)
    bj3 = lambda i: (i // nj, 0, i % nj)
    one = pl.Buffered(1)
    in_specs = [
        pl.BlockSpec((tm, D_MODEL), lambda i: (i, 0)),
        pl.BlockSpec((1, 6, D_MODEL), lambda i: (i // nj, 0, 0)),
        pl.BlockSpec((1, 256, tm), bj3),
        pl.BlockSpec((1, 256, tm), bj3),
        pl.BlockSpec((1, 512, tm), bj3),
        pl.BlockSpec((D_MODEL, D_MODEL), const2, pipeline_mode=one),
        pl.BlockSpec((1, D_MODEL), const2),
        pl.BlockSpec((1, D_MODEL), const2),
        pl.BlockSpec((D_MODEL, D_FF), const2, pipeline_mode=one),
        pl.BlockSpec((D_FF, D_MODEL), const2, pipeline_mode=one),
        pl.BlockSpec((1, D_MODEL), const2),
        pl.BlockSpec((1, D_MODEL), const2),
    ]
    return pl.pallas_call(
        _mlp_kernel,
        out_shape=jax.ShapeDtypeStruct((T, D_MODEL), F32),
        grid=(T // tm,),
        in_specs=in_specs,
        out_specs=pl.BlockSpec((tm, D_MODEL), lambda i: (i, 0)),
        compiler_params=pltpu.CompilerParams(vmem_limit_bytes=VMEM_LIMIT),
        name="out_mlp",
    )(x2d, mod, aoT, boT, coT, wo, ln1g, ln1b, w1, w2, ln2g, ln2b)


def _rope_angles(pos, dim):
    inv = ROPE_THETA ** (-jnp.arange(0, dim, 2, dtype=F32) / dim)
    return pos.astype(F32)[:, None] * inv[None, :]


def _t5_bucket(rel):
    nb = REL_BUCKETS // 2
    max_exact = nb // 2
    sign = jnp.where(rel > 0, nb, 0)
    n = jnp.abs(rel)
    nf = jnp.maximum(n, 1).astype(F32)
    large = max_exact + (jnp.log(nf / max_exact) / math.log(REL_MAX_DIST / max_exact)
                         * (nb - max_exact)).astype(jnp.int32)
    large = jnp.minimum(large, nb - 1)
    return sign + jnp.where(n < max_exact, n, large)


def _dilated_bias_tiles(rel_bias):
    t = A_TILE
    span = (A_CENTER + 1) * t
    deltas = np.arange(-span + 1, span)
    mult = np.zeros(deltas.shape, np.int32)
    for (w, d) in A_PATTERNS:
        half = w // (2 * d)
        mult += ((deltas % d == 0) & (np.abs(deltas) <= half * d)).astype(np.int32)
    bucket = _t5_bucket(jnp.asarray(deltas, jnp.int32))
    logm = jnp.asarray(np.log(np.maximum(mult, 1)), F32)
    tab = jnp.where(jnp.asarray(mult > 0)[:, None],
                    (rel_bias[bucket] + logm[:, None]) * LOG2E, NEG_BIG)
    x = np.arange(2 * t)
    d = np.where(x <= t - 1, -x, 2 * t - x)
    d[t] = 0
    idx = (np.arange(A_NCHUNK)[:, None] - A_CENTER) * t + d[None, :] + span - 1
    u = jnp.moveaxis(tab[jnp.asarray(idx, jnp.int32)], -1, 0)
    u = jnp.concatenate([u, jnp.full((A_HEADS, 1, 2 * t), NEG_BIG, F32)], axis=1)
    flat = jnp.tile(u, (1, 1, t))[:, :, :t * (2 * t - 1)]
    return flat.reshape(A_HEADS, A_NCHUNK + 1, t, 2 * t - 1)[:, :, :, :t]


def _rope_tables(gq, gk):
    t = jnp.arange(SEQ)
    ang_t = _rope_angles(t, B_ROPE)
    ang_row = _rope_angles(t // GRID_W, HEAD_DIM // 2)
    ang_col = _rope_angles(t % GRID_W, HEAD_DIM // 2)
    cos64 = jnp.concatenate([jnp.cos(ang_row)] * 2 + [jnp.cos(ang_col)] * 2, axis=1)
    sin_r, sin_c = jnp.sin(ang_row), jnp.sin(ang_col)
    sin64 = jnp.concatenate([-sin_r, sin_r, -sin_c, sin_c], axis=1)
    perm = np.concatenate([np.arange(16, 32), np.arange(0, 16), np.arange(48, 64), np.arange(32, 48)])

    def with_gain(g, scale):
        return (cos64 * g[None, :] * scale).T, (sin64 * g[perm][None, :] * scale).T

    qc, qs = with_gain(gq, HEAD_DIM ** -0.5 * LOG2E)
    kc, ks = with_gain(gk, 1.0)
    return jnp.concatenate([jnp.cos(ang_t).T, jnp.sin(ang_t).T, qc, qs, kc, ks], axis=0)


def kernel(x, c, w_ada, b_ada, w_in, mla_q_norm, mla_w_uq, mla_kv_norm, mla_w_ukv,
           gqa_q_norm, gqa_k_norm, rel_bias, w_o, ln1_g, ln1_b, w1, w2, ln2_g, ln2_b):
    B, S, D = x.shape
    c_pad = jnp.zeros((8, D), F32).at[:B].set(c)
    mod_all = _ada_call(c_pad, w_ada, b_ada)[:, :B].reshape(DEPTH, B, 6, D)
    bias = _dilated_bias_tiles(rel_bias)

    x2d = x.reshape(B * S, D)
    for l in range(DEPTH):
        winT = w_in[l].T.astype(BF16)
        wuqT = mla_w_uq[l].T
        wukv = mla_w_ukv[l].reshape(B_KV_RANK, B_HEADS, B_NOPE + B_V)
        wukT = wukv[:, :, :B_NOPE].reshape(B_KV_RANK, B_HEADS * B_NOPE).T
        wuvT = wukv[:, :, B_NOPE:].reshape(B_KV_RANK, B_HEADS * B_V).T
        tab = _rope_tables(gqa_q_norm[l], gqa_k_norm[l])
        aqT, ak, avT, bqT, bk, bvT, cqT, ck, cvT = _proj_call(
            x2d, mod_all[l], winT, wuqT, mla_q_norm[l][None, :], wukT, wuvT,
            mla_kv_norm[l][None, :], tab)
        aoT = _dil_attn_call(aqT, ak, avT, bias)
        boT = _full_attn_call(bqT, bk, bvT, groups=1, tq=1024, name="mla_attn")
        coT = _full_attn_call(cqT, ck, cvT, groups=C_Q_HEADS // C_KV_HEADS, tq=256, name="gqa_attn")
        x2d = _mlp_call(x2d, mod_all[l], aoT, boT, coT, w_o[l].astype(BF16),
                        ln1_g[l][None, :], ln1_b[l][None, :], w1[l].astype(BF16), w2[l].astype(BF16),
                        ln2_g[l][None, :], ln2_b[l][None, :])
    return x2d.reshape(B, S, D)
```

```python
import functools
import math

import numpy as np
import jax
import jax.numpy as jnp
from jax import lax
from jax.experimental import pallas as pl
from jax.experimental.pallas import tpu as pltpu

D_MODEL = 1024
BATCH = 4
SEQ = 4096
DEPTH = 2
HEAD_DIM = 64
A_HEADS = 4
A_PATTERNS = ((128, 1), (512, 4), (2048, 16))
B_HEADS = 4
B_Q_RANK = 384
B_KV_RANK = 256
B_NOPE = 64
B_ROPE = 32
B_V = 64
C_Q_HEADS = 8
C_KV_HEADS = 2
GRID_W = 64
ROPE_THETA = 10000.0
REL_BUCKETS = 32
REL_MAX_DIST = 1024
D_FF = 4 * D_MODEL
EPS = 1e-6
NEG_BIG = -1e30
ALPHA = (2 * DEPTH) ** 0.25
A_W = A_HEADS * HEAD_DIM
IN_COLS = 2208
LOG2E = 1.4426950408889634

F32 = jnp.float32
BF16 = jnp.bfloat16

LANES = 128
VMEM_LIMIT = 56 * 1024 * 1024

OFF_AQ, OFF_AK, OFF_AV = 0, 256, 512
OFF_BCQ, OFF_BCKV, OFF_BKR = 768, 1152, 1408
OFF_CQ, OFF_CK, OFF_CV, OFF_END = 1440, 1952, 2080, 2208

TOK_TILE = 512
A_TILE = 256
A_NCHUNK = 9
A_CENTER = 4
KV_TILE = 512
ROW_BLK = 256
RING = 3
STRIP = 512


def _nt_dot(a, b):
    return lax.dot_general(a, b, (((1,), (1,)), ((), ())), preferred_element_type=F32)


def _ada_kernel(c_ref, w_ref, b_ref, o_ref):
    c = c_ref[...]
    ca = (c / (1.0 + jnp.exp(-c))).astype(BF16)
    w = w_ref[0].astype(BF16)
    o_ref[0] = jnp.dot(ca, w, preferred_element_type=F32) + b_ref[0]


def _ada_call(c_pad, w_ada, b_ada):
    tn = 1536
    n = 6 * D_MODEL
    return pl.pallas_call(
        _ada_kernel,
        out_shape=jax.ShapeDtypeStruct((DEPTH, 8, n), F32),
        grid=(DEPTH, n // tn),
        in_specs=[
            pl.BlockSpec((8, D_MODEL), lambda l, j: (0, 0)),
            pl.BlockSpec((1, D_MODEL, tn), lambda l, j: (l, 0, j)),
            pl.BlockSpec((1, 1, tn), lambda l, j: (l, 0, j)),
        ],
        out_specs=pl.BlockSpec((1, 8, tn), lambda l, j: (l, 0, j)),
        compiler_params=pltpu.CompilerParams(vmem_limit_bytes=VMEM_LIMIT),
        name="ada_mod",
    )(c_pad, w_ada, b_ada.reshape(DEPTH, 1, n))


def _proj_kernel(x_ref, mod_ref, winT_ref, wuqT_ref, gq_ref, wukT_ref, wuvT_ref, gkv_ref, tab_ref,
                 aqT_ref, ak_ref, avT_ref, bqT_ref, bk_ref, bvT_ref, cqT_ref, ck_ref, cvT_ref):
    tm = x_ref.shape[0]
    sh = mod_ref[0, 0:1, :]
    sc = mod_ref[0, 1:2, :]
    h = (x_ref[...] * (1.0 + sc) + sh).astype(BF16)

    def seg(r0, r1):
        return _nt_dot(winT_ref[r0:r1, :], h)

    z64 = jnp.zeros((64, tm), F32)
    z64b = jnp.zeros((64, tm), BF16)
    z32b = jnp.zeros((32, tm), BF16)

    aq = seg(OFF_AQ, OFF_AK) * (HEAD_DIM ** -0.5 * LOG2E)
    for hd in range(A_HEADS):
        aqT_ref[0, 128 * hd:128 * hd + 64, :] = aq[64 * hd:64 * hd + 64].astype(BF16)
        aqT_ref[0, 128 * hd + 64:128 * hd + 128, :] = z64b
    ak = seg(OFF_AK, OFF_AV)
    for hd in range(A_HEADS):
        kt = jnp.concatenate([ak[64 * hd:64 * hd + 64], z64], axis=0)
        ak_ref[0, hd] = kt.T.astype(BF16)
    av = seg(OFF_AV, OFF_BCQ).astype(BF16)
    for j in range(tm // A_TILE):
        avT_ref[0, j] = av[:, A_TILE * j:A_TILE * (j + 1)]

    cos_t = tab_ref[0:16, :]
    sin_t = tab_ref[16:32, :]
    cq = seg(OFF_BCQ, OFF_BCKV)
    rq = lax.rsqrt(jnp.mean(cq * cq, axis=0, keepdims=True) + EPS)
    wq = (wuqT_ref[...] * gq_ref[...]).astype(BF16)
    uq = jnp.dot(wq, cq.astype(BF16), preferred_element_type=F32)
    uq = uq * (rq * ((B_NOPE + B_ROPE) ** -0.5 * LOG2E))
    for hd in range(B_HEADS):
        b0 = (B_NOPE + B_ROPE) * hd
        x1 = uq[b0 + 64:b0 + 80]
        x2 = uq[b0 + 80:b0 + 96]
        bqT_ref[0, 128 * hd:128 * hd + 64, :] = uq[b0:b0 + 64].astype(BF16)
        bqT_ref[0, 128 * hd + 64:128 * hd + 80, :] = (x1 * cos_t - x2 * sin_t).astype(BF16)
        bqT_ref[0, 128 * hd + 80:128 * hd + 96, :] = (x1 * sin_t + x2 * cos_t).astype(BF16)
        bqT_ref[0, 128 * hd + 96:128 * hd + 128, :] = z32b
    ckv = seg(OFF_BCKV, OFF_BKR)
    rkv = lax.rsqrt(jnp.mean(ckv * ckv, axis=0, keepdims=True) + EPS)
    ckv_b = ckv.astype(BF16)
    gkv = gkv_ref[...]
    kn = jnp.dot((wukT_ref[...] * gkv).astype(BF16), ckv_b, preferred_element_type=F32) * rkv
    vv = jnp.dot((wuvT_ref[...] * gkv).astype(BF16), ckv_b, preferred_element_type=F32) * rkv
    bvT_ref[0, 0] = vv.astype(BF16)
    kr = seg(OFF_BKR, OFF_CQ)
    kr1 = kr[0:16]
    kr2 = kr[16:32]
    kro = jnp.concatenate([kr1 * cos_t - kr2 * sin_t, kr1 * sin_t + kr2 * cos_t,
                           jnp.zeros((32, tm), F32)], axis=0)
    for hd in range(B_HEADS):
        kt = jnp.concatenate([kn[64 * hd:64 * hd + 64], kro], axis=0)
        bk_ref[0, hd] = kt.T.astype(BF16)

    def normrope(xh, c, s):
        r = lax.rsqrt(jnp.mean(xh * xh, axis=0, keepdims=True) + EPS)
        rot = jnp.concatenate([xh[16:32], xh[0:16], xh[48:64], xh[32:48]], axis=0)
        return (xh * c + rot * s) * r

    cqc = tab_ref[32:96, :]
    cqs = tab_ref[96:160, :]
    cq2 = seg(OFF_CQ, OFF_CK)
    for hd in range(C_Q_HEADS):
        cqT_ref[0, 128 * hd:128 * hd + 64, :] = normrope(cq2[64 * hd:64 * hd + 64], cqc, cqs).astype(BF16)
        cqT_ref[0, 128 * hd + 64:128 * hd + 128, :] = z64b
    ckc = tab_ref[160:224, :]
    cks = tab_ref[224:288, :]
    ck2 = seg(OFF_CK, OFF_CV)
    for hd in range(C_KV_HEADS):
        kt = jnp.concatenate([normrope(ck2[64 * hd:64 * hd + 64], ckc, cks), z64], axis=0)
        ck_ref[0, hd] = kt.T.astype(BF16)
    cvT_ref[0, 0] = seg(OFF_CV, OFF_END).astype(BF16)


def _proj_call(x2d, mod, winT, wuqT, gq, wukT, wuvT, gkv, tab):
    tm = TOK_TILE
    T = x2d.shape[0]
    nj = SEQ // tm
    const2 = lambda i: (0, 0)
    bj3 = lambda i: (i // nj, 0, i % nj)
    bj4 = lambda i: (i // nj, 0, i % nj, 0)
    cj4 = lambda i: (i // nj, i % nj, 0, 0)
    out_shape = (
        jax.ShapeDtypeStruct((BATCH, 512, SEQ), BF16),
        jax.ShapeDtypeStruct((BATCH, A_HEADS, SEQ, 128), BF16),
        jax.ShapeDtypeStruct((BATCH, SEQ // A_TILE, 256, A_TILE), BF16),
        jax.ShapeDtypeStruct((BATCH, 512, SEQ), BF16),
        jax.ShapeDtypeStruct((BATCH, B_HEADS, SEQ, 128), BF16),
        jax.ShapeDtypeStruct((BATCH, SEQ // KV_TILE, 256, KV_TILE), BF16),
        jax.ShapeDtypeStruct((BATCH, 1024, SEQ), BF16),
        jax.ShapeDtypeStruct((BATCH, C_KV_HEADS, SEQ, 128), BF16),
        jax.ShapeDtypeStruct((BATCH, SEQ // KV_TILE, 128, KV_TILE), BF16),
    )
    out_specs = (
        pl.BlockSpec((1, 512, tm), bj3),
        pl.BlockSpec((1, A_HEADS, tm, 128), bj4),
        pl.BlockSpec((1, tm // A_TILE, 256, A_TILE), cj4),
        pl.BlockSpec((1, 512, tm), bj3),
        pl.BlockSpec((1, B_HEADS, tm, 128), bj4),
        pl.BlockSpec((1, tm // KV_TILE, 256, KV_TILE), cj4),
        pl.BlockSpec((1, 1024, tm), bj3),
        pl.BlockSpec((1, C_KV_HEADS, tm, 128), bj4),
        pl.BlockSpec((1, tm // KV_TILE, 128, KV_TILE), cj4),
    )
    in_specs = [
        pl.BlockSpec((tm, D_MODEL), lambda i: (i, 0)),
        pl.BlockSpec((1, 6, D_MODEL), lambda i: (i // nj, 0, 0)),
        pl.BlockSpec((IN_COLS, D_MODEL), const2),
        pl.BlockSpec((B_Q_RANK, B_Q_RANK), const2),
        pl.BlockSpec((1, B_Q_RANK), const2),
        pl.BlockSpec((256, B_KV_RANK), const2),
        pl.BlockSpec((256, B_KV_RANK), const2),
        pl.BlockSpec((1, B_KV_RANK), const2),
        pl.BlockSpec((288, tm), lambda i: (0, i % nj)),
    ]
    return pl.pallas_call(
        _proj_kernel,
        out_shape=out_shape,
        grid=(T // tm,),
        in_specs=in_specs,
        out_specs=out_specs,
        compiler_params=pltpu.CompilerParams(vmem_limit_bytes=VMEM_LIMIT),
        name="in_proj",
    )(x2d, mod, winT, wuqT, gq, wukT, wuvT, gkv, tab)


def _with_ones_rows(v):
    return jnp.concatenate([v, jnp.ones((16, v.shape[1]), BF16)], axis=0)


def _attn_finish(o_ref, acc_ref, groups, tq):
    acc = acc_ref[...]
    o = acc[0:64] / acc[64:65]
    for g in range(groups):
        o_ref[0, 64 * g:64 * g + 64, :] = o[:, tq * g:tq * (g + 1)].astype(o_ref.dtype)


def _full_attn_kernel(qT_ref, k_ref, vT_ref, o_ref, acc_ref, s_ref, p_ref, *, groups, tq, tk):
    n = groups * tq
    w = min(n, STRIP)
    nstrip = n // w
    nchunk = SEQ // tk
    rb = ROW_BLK
    nb = tk // rb
    nunit = nchunk * nstrip
    cols = [qT_ref[0, 128 * g:128 * (g + 1), :] for g in range(groups)]
    q_cat = cols[0] if groups == 1 else jnp.concatenate(cols, axis=1)
    q_strips = [q_cat[:, w * j:w * (j + 1)] for j in range(nstrip)]
    acc_ref[...] = jnp.zeros(acc_ref.shape, F32)
    m = [jnp.full((1, w), NEG_BIG, F32) for _ in range(nstrip)]

    def qk_block(u, b):
        c, j = divmod(u, nstrip)
        k = k_ref[0, 0, c * tk + rb * b:c * tk + rb * (b + 1), :]
        return jnp.dot(k, q_strips[j], preferred_element_type=F32)

    def pv_block(u, b, p_blk):
        c = u // nstrip
        v_aug = _with_ones_rows(vT_ref[0, c, :, rb * b:rb * (b + 1)])
        return jnp.dot(v_aug, p_blk, preferred_element_type=F32)

    alphas = {}
    for t in range(nunit + 2):
        pv = None
        for b in range(nb):
            rows = slice(rb * b, rb * (b + 1))
            if t < nunit:
                s_ref[t % RING, rows, :] = qk_block(t, b)
            if t >= 2:
                part = pv_block(t - 2, b, p_ref[(t - 2) % RING, rows, :])
                pv = part if pv is None else pv + part
        if 1 <= t <= nunit:
            u = t - 1
            j = u % nstrip
            s_u = s_ref[u % RING]
            m_new = jnp.maximum(m[j], jnp.max(s_u, axis=0, keepdims=True))
            alphas[u] = jnp.exp2(m[j] - m_new)
            p_ref[u % RING] = jnp.exp2(s_u - m_new).astype(BF16)
            m[j] = m_new
        if t >= 2:
            u = t - 2
            sl = slice(w * (u % nstrip), w * (u % nstrip + 1))
            acc_ref[:, sl] = alphas.pop(u) * acc_ref[:, sl] + pv
    _attn_finish(o_ref, acc_ref, groups, tq)


def _full_attn_call(qT, k, vT, *, groups, tq, name):
    tk = KV_TILE
    hkv = k.shape[1]
    n = groups * tq
    kern = functools.partial(_full_attn_kernel, groups=groups, tq=tq, tk=tk)
    return pl.pallas_call(
        kern,
        out_shape=jax.ShapeDtypeStruct((BATCH, hkv * groups * 64, SEQ), BF16),
        grid=(BATCH, hkv, SEQ // tq),
        in_specs=[
            pl.BlockSpec((1, groups * 128, tq), lambda b, g, i: (b, g, i)),
            pl.BlockSpec((1, 1, SEQ, 128), lambda b, g, i: (b, g, 0, 0)),
            pl.BlockSpec((1, SEQ // tk, 64, tk), lambda b, g, i: (b, 0, g, 0)),
        ],
        out_specs=pl.BlockSpec((1, groups * 64, tq), lambda b, g, i: (b, g, i)),
        scratch_shapes=[pltpu.VMEM((80, n), F32),
                        pltpu.VMEM((RING, tk, min(n, STRIP)), F32),
                        pltpu.VMEM((RING, tk, min(n, STRIP)), BF16)],
        compiler_params=pltpu.CompilerParams(vmem_limit_bytes=VMEM_LIMIT),
        name=name,
    )(qT, k, vT)


def _dil_attn_kernel(qT_ref, k_ref, vT_ref, bias_ref, o_ref):
    t = A_TILE
    nq = SEQ // t
    i = pl.program_id(1)
    first = jnp.clip(i - A_CENTER, 0, nq - A_NCHUNK)

    def scores(h):
        q = qT_ref[0, 128 * h:128 * (h + 1), :]
        out = []
        for r in range(A_NCHUNK):
            kt = first + r
            d = kt - i + A_CENTER
            bi = jnp.where((d >= 0) & (d < A_NCHUNK), d, A_NCHUNK)
            k = k_ref[0, h, pl.ds(pl.multiple_of(kt * t, t), t), :]
            out.append(jnp.dot(k, q, preferred_element_type=F32) + bias_ref[h, bi])
        return out

    def finish(h, s_list):
        m = functools.reduce(jnp.maximum, s_list)
        m = jnp.max(m, axis=0, keepdims=True)
        acc = None
        for r in range(A_NCHUNK):
            p = jnp.exp2(s_list[r] - m).astype(BF16)
            v_aug = _with_ones_rows(vT_ref[0, first + r, 64 * h:64 * (h + 1), :])
            pv = jnp.dot(v_aug, p, preferred_element_type=F32)
            acc = pv if acc is None else acc + pv
        o_ref[0, 64 * h:64 * (h + 1), :] = (acc[0:64] / acc[64:65]).astype(o_ref.dtype)

    s_next = scores(0)
    for h in range(A_HEADS):
        s_cur = s_next
        if h + 1 < A_HEADS:
            s_next = scores(h + 1)
        finish(h, s_cur)


def _dil_attn_call(qT, k, vT, bias):
    t = A_TILE
    return pl.pallas_call(
        _dil_attn_kernel,
        out_shape=jax.ShapeDtypeStruct((BATCH, A_HEADS * 64, SEQ), BF16),
        grid=(BATCH, SEQ // t),
        in_specs=[
            pl.BlockSpec((1, A_HEADS * 128, t), lambda b, i: (b, 0, i)),
            pl.BlockSpec((1, A_HEADS, SEQ, 128), lambda b, i: (b, 0, 0, 0)),
            pl.BlockSpec((1, SEQ // t, A_HEADS * 64, t), lambda b, i: (b, 0, 0, 0)),
            pl.BlockSpec((A_HEADS, A_NCHUNK + 1, t, t), lambda b, i: (0, 0, 0, 0),
                         pipeline_mode=pl.Buffered(1)),
        ],
        out_specs=pl.BlockSpec((1, A_HEADS * 64, t), lambda b, i: (b, 0, i)),
        compiler_params=pltpu.CompilerParams(vmem_limit_bytes=VMEM_LIMIT),
        name="dilated_attn",
    )(qT, k, vT, bias)


def _layer_norm(z, g, b):
    mu = jnp.mean(z, axis=-1, keepdims=True)
    d = z - mu
    var = jnp.mean(d * d, axis=-1, keepdims=True)
    return d * lax.rsqrt(var + EPS) * g + b


def _mlp_kernel(x_ref, mod_ref, aoT_ref, boT_ref, coT_ref, wo_ref, ln1g_ref, ln1b_ref,
                w1_ref, w2_ref, ln2g_ref, ln2b_ref, o_ref):
    g_a = mod_ref[0, 2:3, :]
    sh_m = mod_ref[0, 3:4, :]
    sc_m = mod_ref[0, 4:5, :]
    g_m = mod_ref[0, 5:6, :]
    catT = jnp.concatenate([aoT_ref[0], boT_ref[0], coT_ref[0]], axis=0)
    cat = catT.astype(F32).T.astype(BF16)
    y = jnp.dot(cat, wo_ref[...], preferred_element_type=F32)
    x1 = _layer_norm(ALPHA * x_ref[...] + (1.0 + g_a) * y, ln1g_ref[...], ln1b_ref[...])
    h = (x1 * (1.0 + sc_m) + sh_m).astype(BF16)
    y2 = None
    fc = 1024
    for j in range(D_FF // fc):
        u = jnp.dot(h, w1_ref[:, fc * j:fc * (j + 1)], preferred_element_type=F32)
        u = jnp.maximum(u, 0.0)
        u = (u * u).astype(BF16)
        part = jnp.dot(u, w2_ref[fc * j:fc * (j + 1), :], preferred_element_type=F32)
        y2 = part if y2 is None else y2 + part
    o_ref[...] = _layer_norm(ALPHA * x1 + (1.0 + g_m) * y2, ln2g_ref[...], ln2b_ref[...])


def _mlp_call(x2d, mod, aoT, boT, coT, wo, ln1g, ln1b, w1, w2, ln2g, ln2b):
    tm = TOK_TILE
    T = x2d.shape[0]
    nj = SEQ // tm
    const2 = lambda i: (0, 0)
    bj3 = lambda i: (i // nj, 0, i % nj)
    one = pl.Buffered(1)
    in_specs = [
        pl.BlockSpec((tm, D_MODEL), lambda i: (i, 0)),
        pl.BlockSpec((1, 6, D_MODEL), lambda i: (i // nj, 0, 0)),
        pl.BlockSpec((1, 256, tm), bj3),
        pl.BlockSpec((1, 256, tm), bj3),
        pl.BlockSpec((1, 512, tm), bj3),
        pl.BlockSpec((D_MODEL, D_MODEL), const2, pipeline_mode=one),
        pl.BlockSpec((1, D_MODEL), const2),
        pl.BlockSpec((1, D_MODEL), const2),
        pl.BlockSpec((D_MODEL, D_FF), const2, pipeline_mode=one),
        pl.BlockSpec((D_FF, D_MODEL), const2, pipeline_mode=one),
        pl.BlockSpec((1, D_MODEL), const2),
        pl.BlockSpec((1, D_MODEL), const2),
    ]
    return pl.pallas_call(
        _mlp_kernel,
        out_shape=jax.ShapeDtypeStruct((T, D_MODEL), F32),
        grid=(T // tm,),
        in_specs=in_specs,
        out_specs=pl.BlockSpec((tm, D_MODEL), lambda i: (i, 0)),
        compiler_params=pltpu.CompilerParams(vmem_limit_bytes=VMEM_LIMIT),
        name="out_mlp",
    )(x2d, mod, aoT, boT, coT, wo, ln1g, ln1b, w1, w2, ln2g, ln2b)


def _rope_angles(pos, dim):
    inv = ROPE_THETA ** (-jnp.arange(0, dim, 2, dtype=F32) / dim)
    return pos.astype(F32)[:, None] * inv[None, :]


def _t5_bucket(rel):
    nb = REL_BUCKETS // 2
    max_exact = nb // 2
    sign = jnp.where(rel > 0, nb, 0)
    n = jnp.abs(rel)
    nf = jnp.maximum(n, 1).astype(F32)
    large = max_exact + (jnp.log(nf / max_exact) / math.log(REL_MAX_DIST / max_exact)
                         * (nb - max_exact)).astype(jnp.int32)
    large = jnp.minimum(large, nb - 1)
    return sign + jnp.where(n < max_exact, n, large)


def _dilated_bias_tiles(rel_bias):
    t = A_TILE
    span = (A_CENTER + 1) * t
    deltas = np.arange(-span + 1, span)
    mult = np.zeros(deltas.shape, np.int32)
    for (w, d) in A_PATTERNS:
        half = w // (2 * d)
        mult += ((deltas % d == 0) & (np.abs(deltas) <= half * d)).astype(np.int32)
    bucket = _t5_bucket(jnp.asarray(deltas, jnp.int32))
    logm = jnp.asarray(np.log(np.maximum(mult, 1)), F32)
    tab = jnp.where(jnp.asarray(mult > 0)[:, None],
                    (rel_bias[bucket] + logm[:, None]) * LOG2E, NEG_BIG)
    x = np.arange(2 * t)
    d = np.where(x <= t - 1, -x, 2 * t - x)
    d[t] = 0
    idx = (np.arange(A_NCHUNK)[:, None] - A_CENTER) * t + d[None, :] + span - 1
    u = jnp.moveaxis(tab[jnp.asarray(idx, jnp.int32)], -1, 0)
    u = jnp.concatenate([u, jnp.full((A_HEADS, 1, 2 * t), NEG_BIG, F32)], axis=1)
    flat = jnp.tile(u, (1, 1, t))[:, :, :t * (2 * t - 1)]
    return flat.reshape(A_HEADS, A_NCHUNK + 1, t, 2 * t - 1)[:, :, :, :t]


def _rope_tables(gq, gk):
    t = jnp.arange(SEQ)
    ang_t = _rope_angles(t, B_ROPE)
    ang_row = _rope_angles(t // GRID_W, HEAD_DIM // 2)
    ang_col = _rope_angles(t % GRID_W, HEAD_DIM // 2)
    cos64 = jnp.concatenate([jnp.cos(ang_row)] * 2 + [jnp.cos(ang_col)] * 2, axis=1)
    sin_r, sin_c = jnp.sin(ang_row), jnp.sin(ang_col)
    sin64 = jnp.concatenate([-sin_r, sin_r, -sin_c, sin_c], axis=1)
    perm = np.concatenate([np.arange(16, 32), np.arange(0, 16), np.arange(48, 64), np.arange(32, 48)])

    def with_gain(g, scale):
        return (cos64 * g[None, :] * scale).T, (sin64 * g[perm][None, :] * scale).T

    qc, qs = with_gain(gq, HEAD_DIM ** -0.5 * LOG2E)
    kc, ks = with_gain(gk, 1.0)
    return jnp.concatenate([jnp.cos(ang_t).T, jnp.sin(ang_t).T, qc, qs, kc, ks], axis=0)


def kernel(x, c, w_ada, b_ada, w_in, mla_q_norm, mla_w_uq, mla_kv_norm, mla_w_ukv,
           gqa_q_norm, gqa_k_norm, rel_bias, w_o, ln1_g, ln1_b, w1, w2, ln2_g, ln2_b):
    B, S, D = x.shape
    c_pad = jnp.zeros((8, D), F32).at[:B].set(c)
    mod_all = _ada_call(c_pad, w_ada, b_ada)[:, :B].reshape(DEPTH, B, 6, D)
    bias = _dilated_bias_tiles(rel_bias)

    x2d = x.reshape(B * S, D)
    for l in range(DEPTH):
        winT = w_in[l].T.astype(BF16)
        wuqT = mla_w_uq[l].T
        wukv = mla_w_ukv[l].reshape(B_KV_RANK, B_HEADS, B_NOPE + B_V)
        wukT = wukv[:, :, :B_NOPE].reshape(B_KV_RANK, B_HEADS * B_NOPE).T
        wuvT = wukv[:, :, B_NOPE:].reshape(B_KV_RANK, B_HEADS * B_V).T
        tab = _rope_tables(gqa_q_norm[l], gqa_k_norm[l])
        aqT, ak, avT, bqT, bk, bvT, cqT, ck, cvT = _proj_call(
            x2d, mod_all[l], winT, wuqT, mla_q_norm[l][None, :], wukT, wuvT,
            mla_kv_norm[l][None, :], tab)
        aoT = _dil_attn_call(aqT, ak, avT, bias)
        boT = _full_attn_call(bqT, bk, bvT, groups=1, tq=1024, name="mla_attn")
        coT = _full_attn_call(cqT, ck, cvT, groups=C_Q_HEADS // C_KV_HEADS, tq=256, name="gqa_attn")
        x2d = _mlp_call(x2d, mod_all[l], aoT, boT, coT, w_o[l].astype(BF16),
                        ln1_g[l][None, :], ln1_b[l][None, :], w1[l].astype(BF16), w2[l].astype(BF16),
                        ln2_g[l][None, :], ln2_b[l][None, :])
    return x2d.reshape(B, S, D)
```

```python
import functools
import math

import numpy as np
import jax
import jax.numpy as jnp
from jax import lax
from jax.experimental import pallas as pl
from jax.experimental.pallas import tpu as pltpu

D_MODEL = 1024
BATCH = 4
SEQ = 4096
DEPTH = 2
HEAD_DIM = 64
A_HEADS = 4
A_PATTERNS = ((128, 1), (512, 4), (2048, 16))
B_HEADS = 4
B_Q_RANK = 384
B_KV_RANK = 256
B_NOPE = 64
B_ROPE = 32
B_V = 64
C_Q_HEADS = 8
C_KV_HEADS = 2
GRID_W = 64
ROPE_THETA = 10000.0
REL_BUCKETS = 32
REL_MAX_DIST = 1024
D_FF = 4 * D_MODEL
EPS = 1e-6
NEG_BIG = -1e30
ALPHA = (2 * DEPTH) ** 0.25
A_W = A_HEADS * HEAD_DIM
IN_COLS = 2208
LOG2E = 1.4426950408889634

F32 = jnp.float32
BF16 = jnp.bfloat16

LANES = 128
VMEM_LIMIT = 56 * 1024 * 1024

OFF_AQ, OFF_AK, OFF_AV = 0, 256, 512
OFF_BCQ, OFF_BCKV, OFF_BKR = 768, 1152, 1408
OFF_CQ, OFF_CK, OFF_CV, OFF_END = 1440, 1952, 2080, 2208

TOK_TILE = 512
A_TILE = 256
A_NCHUNK = 9
A_CENTER = 4
KV_TILE = 512
ROW_BLK = 256
PV_LAG = 2
RING = 3
STRIP = 512
Q_MARGIN = 1.02
K_MARGIN = 1.01
BOUND_LIMIT = 50.0


def _nt_dot(a, b):
    return lax.dot_general(a, b, (((1,), (1,)), ((), ())), preferred_element_type=F32)


def _ada_kernel(c_ref, w_ref, b_ref, o_ref):
    c = c_ref[...]
    ca = (c / (1.0 + jnp.exp(-c))).astype(BF16)
    w = w_ref[0].astype(BF16)
    o_ref[0] = jnp.dot(ca, w, preferred_element_type=F32) + b_ref[0]


def _ada_call(c_pad, w_ada, b_ada):
    tn = 1536
    n = 6 * D_MODEL
    return pl.pallas_call(
        _ada_kernel,
        out_shape=jax.ShapeDtypeStruct((DEPTH, 8, n), F32),
        grid=(DEPTH, n // tn),
        in_specs=[
            pl.BlockSpec((8, D_MODEL), lambda l, j: (0, 0)),
            pl.BlockSpec((1, D_MODEL, tn), lambda l, j: (l, 0, j)),
            pl.BlockSpec((1, 1, tn), lambda l, j: (l, 0, j)),
        ],
        out_specs=pl.BlockSpec((1, 8, tn), lambda l, j: (l, 0, j)),
        compiler_params=pltpu.CompilerParams(vmem_limit_bytes=VMEM_LIMIT),
        name="ada_mod",
    )(c_pad, w_ada, b_ada.reshape(DEPTH, 1, n))


def _proj_kernel(x_ref, mod_ref, winT_ref, wuqT_ref, gq_ref, wukT_ref, wuvT_ref, gkv_ref, tab_ref,
                 aqT_ref, ak_ref, avT_ref, bqT_ref, bk_ref, bvT_ref, cqT_ref, ck_ref, cvT_ref,
                 qn_ref, kn_ref):
    tm = x_ref.shape[0]
    sh = mod_ref[0, 0:1, :]
    sc = mod_ref[0, 1:2, :]
    h = (x_ref[...] * (1.0 + sc) + sh).astype(BF16)

    def seg(r0, r1):
        return _nt_dot(winT_ref[r0:r1, :], h)

    z64 = jnp.zeros((64, tm), F32)
    z64b = jnp.zeros((64, tm), BF16)

    def first_row(rows, val):
        return jnp.where(lax.broadcasted_iota(jnp.int32, (rows, tm), 0) == 0, val, 0.0)

    def norm(xh):
        return jnp.sqrt(jnp.sum(xh * xh, axis=0, keepdims=True))

    one_row32 = first_row(32, 1.0)
    one_row64 = first_row(64, 1.0)
    qn_ref[0] = jnp.zeros(qn_ref.shape[1:], F32)
    kn_ref[0] = jnp.zeros(kn_ref.shape[1:], F32)

    aq = seg(OFF_AQ, OFF_AK) * (HEAD_DIM ** -0.5 * LOG2E)
    for hd in range(A_HEADS):
        aqT_ref[0, 128 * hd:128 * hd + 64, :] = aq[64 * hd:64 * hd + 64].astype(BF16)
        aqT_ref[0, 128 * hd + 64:128 * hd + 128, :] = z64b
    ak = seg(OFF_AK, OFF_AV)
    for hd in range(A_HEADS):
        kt = jnp.concatenate([ak[64 * hd:64 * hd + 64], z64], axis=0)
        ak_ref[0, hd] = kt.T.astype(BF16)
    av = seg(OFF_AV, OFF_BCQ).astype(BF16)
    for j in range(tm // A_TILE):
        avT_ref[0, j] = av[:, A_TILE * j:A_TILE * (j + 1)]

    cos_t = tab_ref[0:16, :]
    sin_t = tab_ref[16:32, :]
    cq = seg(OFF_BCQ, OFF_BCKV)
    rq = lax.rsqrt(jnp.mean(cq * cq, axis=0, keepdims=True) + EPS)
    wq = (wuqT_ref[...] * gq_ref[...]).astype(BF16)
    uq = jnp.dot(wq, cq.astype(BF16), preferred_element_type=F32)
    uq = uq * (rq * ((B_NOPE + B_ROPE) ** -0.5 * LOG2E))
    for hd in range(B_HEADS):
        b0 = (B_NOPE + B_ROPE) * hd
        x1 = uq[b0 + 64:b0 + 80]
        x2 = uq[b0 + 80:b0 + 96]
        qb = Q_MARGIN * norm(uq[b0:b0 + 96])
        qn_ref[0, hd:hd + 1, :] = qb
        bqT_ref[0, 128 * hd:128 * hd + 64, :] = uq[b0:b0 + 64].astype(BF16)
        bqT_ref[0, 128 * hd + 64:128 * hd + 80, :] = (x1 * cos_t - x2 * sin_t).astype(BF16)
        bqT_ref[0, 128 * hd + 80:128 * hd + 96, :] = (x1 * sin_t + x2 * cos_t).astype(BF16)
        bqT_ref[0, 128 * hd + 96:128 * hd + 128, :] = first_row(32, -qb).astype(BF16)
    ckv = seg(OFF_BCKV, OFF_BKR)
    rkv = lax.rsqrt(jnp.mean(ckv * ckv, axis=0, keepdims=True) + EPS)
    ckv_b = ckv.astype(BF16)
    gkv = gkv_ref[...]
    kn = jnp.dot((wukT_ref[...] * gkv).astype(BF16), ckv_b, preferred_element_type=F32) * rkv
    vv = jnp.dot((wuvT_ref[...] * gkv).astype(BF16), ckv_b, preferred_element_type=F32) * rkv
    bvT_ref[0, 0] = vv.astype(BF16)
    kr = seg(OFF_BKR, OFF_CQ)
    kr1 = kr[0:16]
    kr2 = kr[16:32]
    kro = jnp.concatenate([kr1 * cos_t - kr2 * sin_t, kr1 * sin_t + kr2 * cos_t,
                           one_row32], axis=0)
    kr_sq = jnp.sum(kr * kr, axis=0, keepdims=True)
    for hd in range(B_HEADS):
        knh = kn[64 * hd:64 * hd + 64]
        kn_ref[0, hd:hd + 1, :] = jnp.sqrt(jnp.sum(knh * knh, axis=0, keepdims=True) + kr_sq)
        kt = jnp.concatenate([knh, kro], axis=0)
        bk_ref[0, hd] = kt.T.astype(BF16)

    def normrope(xh, c, s):
        r = lax.rsqrt(jnp.mean(xh * xh, axis=0, keepdims=True) + EPS)
        rot = jnp.concatenate([xh[16:32], xh[0:16], xh[48:64], xh[32:48]], axis=0)
        return (xh * c + rot * s) * r

    cqc = tab_ref[32:96, :]
    cqs = tab_ref[96:160, :]
    cq2 = seg(OFF_CQ, OFF_CK)
    for hd in range(C_Q_HEADS):
        qh = normrope(cq2[64 * hd:64 * hd + 64], cqc, cqs)
        qb = Q_MARGIN * norm(qh)
        qn_ref[0, B_HEADS + hd:B_HEADS + hd + 1, :] = qb
        cqT_ref[0, 128 * hd:128 * hd + 64, :] = qh.astype(BF16)
        cqT_ref[0, 128 * hd + 64:128 * hd + 128, :] = first_row(64, -qb).astype(BF16)
    ckc = tab_ref[160:224, :]
    cks = tab_ref[224:288, :]
    ck2 = seg(OFF_CK, OFF_CV)
    for hd in range(C_KV_HEADS):
        kh = normrope(ck2[64 * hd:64 * hd + 64], ckc, cks)
        kn_ref[0, B_HEADS + hd:B_HEADS + hd + 1, :] = norm(kh)
        kt = jnp.concatenate([kh, one_row64], axis=0)
        ck_ref[0, hd] = kt.T.astype(BF16)
    cvT_ref[0, 0] = seg(OFF_CV, OFF_END).astype(BF16)


def _proj_call(x2d, mod, winT, wuqT, gq, wukT, wuvT, gkv, tab):
    tm = TOK_TILE
    T = x2d.shape[0]
    nj = SEQ // tm
    const2 = lambda i: (0, 0)
    bj3 = lambda i: (i // nj, 0, i % nj)
    bj4 = lambda i: (i // nj, 0, i % nj, 0)
    cj4 = lambda i: (i // nj, i % nj, 0, 0)
    out_shape = (
        jax.ShapeDtypeStruct((BATCH, 512, SEQ), BF16),
        jax.ShapeDtypeStruct((BATCH, A_HEADS, SEQ, 128), BF16),
        jax.ShapeDtypeStruct((BATCH, SEQ // A_TILE, 256, A_TILE), BF16),
        jax.ShapeDtypeStruct((BATCH, 512, SEQ), BF16),
        jax.ShapeDtypeStruct((BATCH, B_HEADS, SEQ, 128), BF16),
        jax.ShapeDtypeStruct((BATCH, SEQ // KV_TILE, 256, KV_TILE), BF16),
        jax.ShapeDtypeStruct((BATCH, 1024, SEQ), BF16),
        jax.ShapeDtypeStruct((BATCH, C_KV_HEADS, SEQ, 128), BF16),
        jax.ShapeDtypeStruct((BATCH, SEQ // KV_TILE, 128, KV_TILE), BF16),
        jax.ShapeDtypeStruct((BATCH, 16, SEQ), F32),
        jax.ShapeDtypeStruct((BATCH, 8, SEQ), F32),
    )
    out_specs = (
        pl.BlockSpec((1, 512, tm), bj3),
        pl.BlockSpec((1, A_HEADS, tm, 128), bj4),
        pl.BlockSpec((1, tm // A_TILE, 256, A_TILE), cj4),
        pl.BlockSpec((1, 512, tm), bj3),
        pl.BlockSpec((1, B_HEADS, tm, 128), bj4),
        pl.BlockSpec((1, tm // KV_TILE, 256, KV_TILE), cj4),
        pl.BlockSpec((1, 1024, tm), bj3),
        pl.BlockSpec((1, C_KV_HEADS, tm, 128), bj4),
        pl.BlockSpec((1, tm // KV_TILE, 128, KV_TILE), cj4),
        pl.BlockSpec((1, 16, tm), bj3),
        pl.BlockSpec((1, 8, tm), bj3),
    )
    in_specs = [
        pl.BlockSpec((tm, D_MODEL), lambda i: (i, 0)),
        pl.BlockSpec((1, 6, D_MODEL), lambda i: (i // nj, 0, 0)),
        pl.BlockSpec((IN_COLS, D_MODEL), const2),
        pl.BlockSpec((B_Q_RANK, B_Q_RANK), const2),
        pl.BlockSpec((1, B_Q_RANK), const2),
        pl.BlockSpec((256, B_KV_RANK), const2),
        pl.BlockSpec((256, B_KV_RANK), const2),
        pl.BlockSpec((1, B_KV_RANK), const2),
        pl.BlockSpec((288, tm), lambda i: (0, i % nj)),
    ]
    return pl.pallas_call(
        _proj_kernel,
        out_shape=out_shape,
        grid=(T // tm,),
        in_specs=in_specs,
        out_specs=out_specs,
        compiler_params=pltpu.CompilerParams(vmem_limit_bytes=VMEM_LIMIT),
        name="in_proj",
    )(x2d, mod, winT, wuqT, gq, wukT, wuvT, gkv, tab)


def _with_ones_rows(v):
    return jnp.concatenate([v, jnp.ones((16, v.shape[1]), BF16)], axis=0)


def _attn_finish(o_ref, acc, groups, tq):
    o = acc[0:64] / acc[64:65]
    for g in range(groups):
        o_ref[0, 64 * g:64 * g + 64, :] = o[:, tq * g:tq * (g + 1)].astype(o_ref.dtype)


def _attn_online(q_strips, k_ref, vT_ref, o_ref, acc_ref, s_ref, p_ref, *, groups, tq, tk):
    nstrip = len(q_strips)
    w = q_strips[0].shape[1]
    nchunk = SEQ // tk
    rb = ROW_BLK
    nb = tk // rb
    nunit = nchunk * nstrip
    acc_ref[...] = jnp.zeros(acc_ref.shape, F32)
    m = [jnp.full((1, w), NEG_BIG, F32) for _ in range(nstrip)]

    def qk_block(u, b):
        c, j = divmod(u, nstrip)
        k = k_ref[0, 0, c * tk + rb * b:c * tk + rb * (b + 1), :]
        return jnp.dot(k, q_strips[j], preferred_element_type=F32)

    def pv_block(u, b, p_blk):
        c = u // nstrip
        v_aug = _with_ones_rows(vT_ref[0, c, :, rb * b:rb * (b + 1)])
        return jnp.dot(v_aug, p_blk, preferred_element_type=F32)

    alphas = {}
    for t in range(nunit + 2):
        pv = None
        for b in range(nb):
            rows = slice(rb * b, rb * (b + 1))
            if t < nunit:
                s_ref[t % RING, rows, :] = qk_block(t, b)
            if t >= 2:
                part = pv_block(t - 2, b, p_ref[(t - 2) % RING, rows, :])
                pv = part if pv is None else pv + part
        if 1 <= t <= nunit:
            u = t - 1
            j = u % nstrip
            s_u = s_ref[u % RING]
            m_new = jnp.maximum(m[j], jnp.max(s_u, axis=0, keepdims=True))
            alphas[u] = jnp.exp2(m[j] - m_new)
            p_ref[u % RING] = jnp.exp2(s_u - m_new).astype(BF16)
            m[j] = m_new
        if t >= 2:
            u = t - 2
            sl = slice(w * (u % nstrip), w * (u % nstrip + 1))
            acc_ref[:, sl] = alphas.pop(u) * acc_ref[:, sl] + pv
    _attn_finish(o_ref, acc_ref[...], groups, tq)


def _attn_bounded(q_strips, k_ref, vT_ref, o_ref, *, groups, tq, tk):
    nstrip = len(q_strips)
    rb = ROW_BLK
    nkb = SEQ // rb
    units = [(kb, j) for kb in range(nkb) for j in range(nstrip)]
    acc = [None] * nstrip
    probs = {}

    def pv(u):
        kb, j = units[u]
        c, b = divmod(kb, tk // rb)
        v_aug = _with_ones_rows(vT_ref[0, c, :, rb * b:rb * (b + 1)])
        part = jnp.dot(v_aug, probs.pop(u), preferred_element_type=F32)
        acc[j] = part if acc[j] is None else acc[j] + part

    for t in range(len(units)):
        kb, j = units[t]
        s = jnp.dot(k_ref[0, 0, rb * kb:rb * (kb + 1), :], q_strips[j], preferred_element_type=F32)
        probs[t] = jnp.exp2(s).astype(BF16)
        if t >= PV_LAG:
            pv(t - PV_LAG)
    for u in range(len(units) - PV_LAG, len(units)):
        pv(u)
    _attn_finish(o_ref, jnp.concatenate(acc, axis=1), groups, tq)


def _full_attn_kernel(scal_ref, qT_ref, k_ref, vT_ref, o_ref, acc_ref, s_ref, p_ref,
                      *, groups, tq, tk, bound_row):
    n = groups * tq
    w = min(n, STRIP)
    nstrip = n // w
    pair = 2 * (pl.program_id(0) * pl.num_programs(1) + pl.program_id(1))
    k_bound = scal_ref[pair]
    bounded_ok = scal_ref[pair + 1] > 0.5
    cols = [qT_ref[0, 128 * g:128 * (g + 1), :] for g in range(groups)]
    q_cat = cols[0] if groups == 1 else jnp.concatenate(cols, axis=1)
    q_lo = q_cat[0:bound_row]
    q_bound = q_cat[bound_row:bound_row + 16]
    q_hi = q_cat[bound_row + 16:]

    def strips(mid):
        q = jnp.concatenate([q_lo, mid, q_hi], axis=0)
        return [q[:, w * j:w * (j + 1)] for j in range(nstrip)]

    @pl.when(bounded_ok)
    def _():
        mid = (q_bound.astype(F32) * k_bound).astype(BF16)
        _attn_bounded(strips(mid), k_ref, vT_ref, o_ref, groups=groups, tq=tq, tk=tk)

    @pl.when(jnp.logical_not(bounded_ok))
    def _():
        _attn_online(strips(jnp.zeros_like(q_bound)), k_ref, vT_ref, o_ref, acc_ref, s_ref, p_ref,
                     groups=groups, tq=tq, tk=tk)


def _bound_scalars(qn, kn, groups):
    k_bound = K_MARGIN * jnp.max(kn, axis=-1)
    q_max = jnp.max(qn, axis=-1).reshape(BATCH, kn.shape[1], groups).max(-1)
    ok = (q_max * k_bound <= BOUND_LIMIT).astype(F32)
    return jnp.stack([k_bound, ok], axis=-1).reshape(-1)


def _full_attn_call(scal, qT, k, vT, *, groups, tq, bound_row, name):
    tk = KV_TILE
    hkv = k.shape[1]
    n = groups * tq
    kern = functools.partial(_full_attn_kernel, groups=groups, tq=tq, tk=tk, bound_row=bound_row)
    return pl.pallas_call(
        kern,
        out_shape=jax.ShapeDtypeStruct((BATCH, hkv * groups * 64, SEQ), BF16),
        grid=(BATCH, hkv, SEQ // tq),
        in_specs=[
            pl.BlockSpec(memory_space=pltpu.SMEM),
            pl.BlockSpec((1, groups * 128, tq), lambda b, g, i: (b, g, i)),
            pl.BlockSpec((1, 1, SEQ, 128), lambda b, g, i: (b, g, 0, 0)),
            pl.BlockSpec((1, SEQ // tk, 64, tk), lambda b, g, i: (b, 0, g, 0)),
        ],
        out_specs=pl.BlockSpec((1, groups * 64, tq), lambda b, g, i: (b, g, i)),
        scratch_shapes=[pltpu.VMEM((80, n), F32),
                        pltpu.VMEM((RING, tk, min(n, STRIP)), F32),
                        pltpu.VMEM((RING, tk, min(n, STRIP)), BF16)],
        compiler_params=pltpu.CompilerParams(vmem_limit_bytes=VMEM_LIMIT),
        name=name,
    )(scal, qT, k, vT)


def _dil_attn_kernel(qT_ref, k_ref, vT_ref, bias_ref, o_ref):
    t = A_TILE
    nq = SEQ // t
    i = pl.program_id(1)
    first = jnp.clip(i - A_CENTER, 0, nq - A_NCHUNK)

    def scores(h):
        q = qT_ref[0, 128 * h:128 * (h + 1), :]
        out = []
        for r in range(A_NCHUNK):
            kt = first + r
            d = kt - i + A_CENTER
            bi = jnp.where((d >= 0) & (d < A_NCHUNK), d, A_NCHUNK)
            k = k_ref[0, h, pl.ds(pl.multiple_of(kt * t, t), t), :]
            out.append(jnp.dot(k, q, preferred_element_type=F32) + bias_ref[h, bi])
        return out

    def finish(h, s_list):
        m = functools.reduce(jnp.maximum, s_list)
        m = jnp.max(m, axis=0, keepdims=True)
        acc = None
        for r in range(A_NCHUNK):
            p = jnp.exp2(s_list[r] - m).astype(BF16)
            v_aug = _with_ones_rows(vT_ref[0, first + r, 64 * h:64 * (h + 1), :])
            pv = jnp.dot(v_aug, p, preferred_element_type=F32)
            acc = pv if acc is None else acc + pv
        o_ref[0, 64 * h:64 * (h + 1), :] = (acc[0:64] / acc[64:65]).astype(o_ref.dtype)

    s_next = scores(0)
    for h in range(A_HEADS):
        s_cur = s_next
        if h + 1 < A_HEADS:
            s_next = scores(h + 1)
        finish(h, s_cur)


def _dil_attn_call(qT, k, vT, bias):
    t = A_TILE
    return pl.pallas_call(
        _dil_attn_kernel,
        out_shape=jax.ShapeDtypeStruct((BATCH, A_HEADS * 64, SEQ), BF16),
        grid=(BATCH, SEQ // t),
        in_specs=[
            pl.BlockSpec((1, A_HEADS * 128, t), lambda b, i: (b, 0, i)),
            pl.BlockSpec((1, A_HEADS, SEQ, 128), lambda b, i: (b, 0, 0, 0)),
            pl.BlockSpec((1, SEQ // t, A_HEADS * 64, t), lambda b, i: (b, 0, 0, 0)),
            pl.BlockSpec((A_HEADS, A_NCHUNK + 1, t, t), lambda b, i: (0, 0, 0, 0),
                         pipeline_mode=pl.Buffered(1)),
        ],
        out_specs=pl.BlockSpec((1, A_HEADS * 64, t), lambda b, i: (b, 0, i)),
        compiler_params=pltpu.CompilerParams(vmem_limit_bytes=VMEM_LIMIT),
        name="dilated_attn",
    )(qT, k, vT, bias)


def _layer_norm(z, g, b):
    mu = jnp.mean(z, axis=-1, keepdims=True)
    d = z - mu
    var = jnp.mean(d * d, axis=-1, keepdims=True)
    return d * lax.rsqrt(var + EPS) * g + b


def _mlp_kernel(x_ref, mod_ref, aoT_ref, boT_ref, coT_ref, wo_ref, ln1g_ref, ln1b_ref,
                w1_ref, w2_ref, ln2g_ref, ln2b_ref, o_ref):
    g_a = mod_ref[0, 2:3, :]
    sh_m = mod_ref[0, 3:4, :]
    sc_m = mod_ref[0, 4:5, :]
    g_m = mod_ref[0, 5:6, :]
    catT = jnp.concatenate([aoT_ref[0], boT_ref[0], coT_ref[0]], axis=0)
    cat = catT.astype(F32).T.astype(BF16)
    y = jnp.dot(cat, wo_ref[...], preferred_element_type=F32)
    x1 = _layer_norm(ALPHA * x_ref[...] + (1.0 + g_a) * y, ln1g_ref[...], ln1b_ref[...])
    h = (x1 * (1.0 + sc_m) + sh_m).astype(BF16)
    y2 = None
    fc = 1024
    for j in range(D_FF // fc):
        u = jnp.dot(h, w1_ref[:, fc * j:fc * (j + 1)], preferred_element_type=F32)
        u = jnp.maximum(u, 0.0)
        u = (u * u).astype(BF16)
        part = jnp.dot(u, w2_ref[fc * j:fc * (j + 1), :], preferred_element_type=F32)
        y2 = part if y2 is None else y2 + part
    o_ref[...] = _layer_norm(ALPHA * x1 + (1.0 + g_m) * y2, ln2g_ref[...], ln2b_ref[...])


def _mlp_call(x2d, mod, aoT, boT, coT, wo, ln1g, ln1b, w1, w2, ln2g, ln2b):
    tm = TOK_TILE
    T = x2d.shape[0]
    nj = SEQ // tm
    const2 = lambda i: (0, 0)
    bj3 = lambda i: (i // nj, 0, i % nj)
    one = pl.Buffered(1)
    in_specs = [
        pl.BlockSpec((tm, D_MODEL), lambda i: (i, 0)),
        pl.BlockSpec((1, 6, D_MODEL), lambda i: (i // nj, 0, 0)),
        pl.BlockSpec((1, 256, tm), bj3),
        pl.BlockSpec((1, 256, tm), bj3),
        pl.BlockSpec((1, 512, tm), bj3),
        pl.BlockSpec((D_MODEL, D_MODEL), const2, pipeline_mode=one),
        pl.BlockSpec((1, D_MODEL), const2),
        pl.BlockSpec((1, D_MODEL), const2),
        pl.BlockSpec((D_MODEL, D_FF), const2, pipeline_mode=one),
        pl.BlockSpec((D_FF, D_MODEL), const2, pipeline_mode=one),
        pl.BlockSpec((1, D_MODEL), const2),
        pl.BlockSpec((1, D_MODEL), const2),
    ]
    return pl.pallas_call(
        _mlp_kernel,
        out_shape=jax.ShapeDtypeStruct((T, D_MODEL), F32),
        grid=(T // tm,),
        in_specs=in_specs,
        out_specs=pl.BlockSpec((tm, D_MODEL), lambda i: (i, 0)),
        compiler_params=pltpu.CompilerParams(vmem_limit_bytes=VMEM_LIMIT),
        name="out_mlp",
    )(x2d, mod, aoT, boT, coT, wo, ln1g, ln1b, w1, w2, ln2g, ln2b)


def _rope_angles(pos, dim):
    inv = ROPE_THETA ** (-jnp.arange(0, dim, 2, dtype=F32) / dim)
    return pos.astype(F32)[:, None] * inv[None, :]


def _t5_bucket(rel):
    nb = REL_BUCKETS // 2
    max_exact = nb // 2
    sign = jnp.where(rel > 0, nb, 0)
    n = jnp.abs(rel)
    nf = jnp.maximum(n, 1).astype(F32)
    large = max_exact + (jnp.log(nf / max_exact) / math.log(REL_MAX_DIST / max_exact)
                         * (nb - max_exact)).astype(jnp.int32)
    large = jnp.minimum(large, nb - 1)
    return sign + jnp.where(n < max_exact, n, large)


def _dilated_bias_tiles(rel_bias):
    t = A_TILE
    span = (A_CENTER + 1) * t
    deltas = np.arange(-span + 1, span)
    mult = np.zeros(deltas.shape, np.int32)
    for (w, d) in A_PATTERNS:
        half = w // (2 * d)
        mult += ((deltas % d == 0) & (np.abs(deltas) <= half * d)).astype(np.int32)
    bucket = _t5_bucket(jnp.asarray(deltas, jnp.int32))
    logm = jnp.asarray(np.log(np.maximum(mult, 1)), F32)
    tab = jnp.where(jnp.asarray(mult > 0)[:, None],
                    (rel_bias[bucket] + logm[:, None]) * LOG2E, NEG_BIG)
    x = np.arange(2 * t)
    d = np.where(x <= t - 1, -x, 2 * t - x)
    d[t] = 0
    idx = (np.arange(A_NCHUNK)[:, None] - A_CENTER) * t + d[None, :] + span - 1
    u = jnp.moveaxis(tab[jnp.asarray(idx, jnp.int32)], -1, 0)
    u = jnp.concatenate([u, jnp.full((A_HEADS, 1, 2 * t), NEG_BIG, F32)], axis=1)
    flat = jnp.tile(u, (1, 1, t))[:, :, :t * (2 * t - 1)]
    return flat.reshape(A_HEADS, A_NCHUNK + 1, t, 2 * t - 1)[:, :, :, :t]


def _rope_tables(gq, gk):
    t = jnp.arange(SEQ)
    ang_t = _rope_angles(t, B_ROPE)
    ang_row = _rope_angles(t // GRID_W, HEAD_DIM // 2)
    ang_col = _rope_angles(t % GRID_W, HEAD_DIM // 2)
    cos64 = jnp.concatenate([jnp.cos(ang_row)] * 2 + [jnp.cos(ang_col)] * 2, axis=1)
    sin_r, sin_c = jnp.sin(ang_row), jnp.sin(ang_col)
    sin64 = jnp.concatenate([-sin_r, sin_r, -sin_c, sin_c], axis=1)
    perm = np.concatenate([np.arange(16, 32), np.arange(0, 16), np.arange(48, 64), np.arange(32, 48)])

    def with_gain(g, scale):
        return (cos64 * g[None, :] * scale).T, (sin64 * g[perm][None, :] * scale).T

    qc, qs = with_gain(gq, HEAD_DIM ** -0.5 * LOG2E)
    kc, ks = with_gain(gk, 1.0)
    return jnp.concatenate([jnp.cos(ang_t).T, jnp.sin(ang_t).T, qc, qs, kc, ks], axis=0)


def kernel(x, c, w_ada, b_ada, w_in, mla_q_norm, mla_w_uq, mla_kv_norm, mla_w_ukv,
           gqa_q_norm, gqa_k_norm, rel_bias, w_o, ln1_g, ln1_b, w1, w2, ln2_g, ln2_b):
    B, S, D = x.shape
    c_pad = jnp.zeros((8, D), F32).at[:B].set(c)
    mod_all = _ada_call(c_pad, w_ada, b_ada)[:, :B].reshape(DEPTH, B, 6, D)
    bias = _dilated_bias_tiles(rel_bias)

    x2d = x.reshape(B * S, D)
    for l in range(DEPTH):
        winT = w_in[l].T.astype(BF16)
        wuqT = mla_w_uq[l].T
        wukv = mla_w_ukv[l].reshape(B_KV_RANK, B_HEADS, B_NOPE + B_V)
        wukT = wukv[:, :, :B_NOPE].reshape(B_KV_RANK, B_HEADS * B_NOPE).T
        wuvT = wukv[:, :, B_NOPE:].reshape(B_KV_RANK, B_HEADS * B_V).T
        tab = _rope_tables(gqa_q_norm[l], gqa_k_norm[l])
        aqT, ak, avT, bqT, bk, bvT, cqT, ck, cvT, qn, kn = _proj_call(
            x2d, mod_all[l], winT, wuqT, mla_q_norm[l][None, :], wukT, wuvT,
            mla_kv_norm[l][None, :], tab)
        aoT = _dil_attn_call(aqT, ak, avT, bias)
        cg = C_Q_HEADS // C_KV_HEADS
        b_scal = _bound_scalars(qn[:, :B_HEADS], kn[:, :B_HEADS], 1)
        c_scal = _bound_scalars(qn[:, B_HEADS:B_HEADS + C_Q_HEADS], kn[:, B_HEADS:B_HEADS + C_KV_HEADS], cg)
        boT = _full_attn_call(b_scal, bqT, bk, bvT, groups=1, tq=1024, bound_row=B_NOPE + B_ROPE,
                              name="mla_attn")
        coT = _full_attn_call(c_scal, cqT, ck, cvT, groups=cg, tq=256, bound_row=HEAD_DIM, name="gqa_attn")
        x2d = _mlp_call(x2d, mod_all[l], aoT, boT, coT, w_o[l].astype(BF16),
                        ln1_g[l][None, :], ln1_b[l][None, :], w1[l].astype(BF16), w2[l].astype(BF16),
                        ln2_g[l][None, :], ln2_b[l][None, :])
    return x2d.reshape(B, S, D)
```

```python
import functools
import math

import numpy as np
import jax
import jax.numpy as jnp
from jax import lax
from jax.experimental import pallas as pl
from jax.experimental.pallas import tpu as pltpu

D_MODEL = 1024
BATCH = 4
SEQ = 4096
DEPTH = 2
HEAD_DIM = 64
A_HEADS = 4
A_PATTERNS = ((128, 1), (512, 4), (2048, 16))
B_HEADS = 4
B_Q_RANK = 384
B_KV_RANK = 256
B_NOPE = 64
B_ROPE = 32
B_V = 64
C_Q_HEADS = 8
C_KV_HEADS = 2
GRID_W = 64
ROPE_THETA = 10000.0
REL_BUCKETS = 32
REL_MAX_DIST = 1024
D_FF = 4 * D_MODEL
EPS = 1e-6
NEG_BIG = -1e30
ALPHA = (2 * DEPTH) ** 0.25
A_W = A_HEADS * HEAD_DIM
IN_COLS = 2208
LOG2E = 1.4426950408889634

F32 = jnp.float32
BF16 = jnp.bfloat16

LANES = 128
VMEM_LIMIT = 56 * 1024 * 1024

OFF_AQ, OFF_AK, OFF_AV = 0, 256, 512
OFF_BCQ, OFF_BCKV, OFF_BKR = 768, 1152, 1408
OFF_CQ, OFF_CK, OFF_CV, OFF_END = 1440, 1952, 2080, 2208

TOK_TILE = 512
A_TILE = 256
A_NCHUNK = 9
A_CENTER = 4
KV_TILE = 512
ROW_BLK = 256
QN_A = 12
KN_A = 8
A_SCAL = 12
A_PV_LAG = 4
PV_LAG = 2
RING = 3
STRIP = 512
Q_MARGIN = 1.02
K_MARGIN = 1.01
BOUND_LIMIT = 50.0


def _nt_dot(a, b):
    return lax.dot_general(a, b, (((1,), (1,)), ((), ())), preferred_element_type=F32)


def _ada_kernel(c_ref, w_ref, b_ref, o_ref):
    c = c_ref[...]
    ca = (c / (1.0 + jnp.exp(-c))).astype(BF16)
    w = w_ref[0].astype(BF16)
    o_ref[0] = jnp.dot(ca, w, preferred_element_type=F32) + b_ref[0]


def _ada_call(c_pad, w_ada, b_ada):
    tn = 1536
    n = 6 * D_MODEL
    return pl.pallas_call(
        _ada_kernel,
        out_shape=jax.ShapeDtypeStruct((DEPTH, 8, n), F32),
        grid=(DEPTH, n // tn),
        in_specs=[
            pl.BlockSpec((8, D_MODEL), lambda l, j: (0, 0)),
            pl.BlockSpec((1, D_MODEL, tn), lambda l, j: (l, 0, j)),
            pl.BlockSpec((1, 1, tn), lambda l, j: (l, 0, j)),
        ],
        out_specs=pl.BlockSpec((1, 8, tn), lambda l, j: (l, 0, j)),
        compiler_params=pltpu.CompilerParams(vmem_limit_bytes=VMEM_LIMIT),
        name="ada_mod",
    )(c_pad, w_ada, b_ada.reshape(DEPTH, 1, n))


def _proj_kernel(x_ref, mod_ref, winT_ref, wuqT_ref, gq_ref, wukT_ref, wuvT_ref, gkv_ref, tab_ref,
                 aqT_ref, ak_ref, avT_ref, bqT_ref, bk_ref, bvT_ref, cqT_ref, ck_ref, cvT_ref,
                 qn_ref, kn_ref):
    tm = x_ref.shape[0]
    sh = mod_ref[0, 0:1, :]
    sc = mod_ref[0, 1:2, :]
    h = (x_ref[...] * (1.0 + sc) + sh).astype(BF16)

    def seg(r0, r1):
        return _nt_dot(winT_ref[r0:r1, :], h)


    def first_row(rows, val):
        return jnp.where(lax.broadcasted_iota(jnp.int32, (rows, tm), 0) == 0, val, 0.0)

    def norm(xh):
        return jnp.sqrt(jnp.sum(xh * xh, axis=0, keepdims=True))

    one_row32 = first_row(32, 1.0)
    one_row64 = first_row(64, 1.0)
    qn_ref[0] = jnp.zeros(qn_ref.shape[1:], F32)
    kn_ref[0] = jnp.zeros(kn_ref.shape[1:], F32)

    aq = seg(OFF_AQ, OFF_AK) * (HEAD_DIM ** -0.5 * LOG2E)
    for hd in range(A_HEADS):
        qh = aq[64 * hd:64 * hd + 64]
        qb = Q_MARGIN * norm(qh)
        qn_ref[0, QN_A + hd:QN_A + hd + 1, :] = qb
        aqT_ref[0, 128 * hd:128 * hd + 64, :] = qh.astype(BF16)
        aqT_ref[0, 128 * hd + 64:128 * hd + 128, :] = first_row(64, -qb).astype(BF16)
    ak = seg(OFF_AK, OFF_AV)
    for hd in range(A_HEADS):
        kh = ak[64 * hd:64 * hd + 64]
        kn_ref[0, KN_A + hd:KN_A + hd + 1, :] = norm(kh)
        kt = jnp.concatenate([kh, one_row64], axis=0)
        ak_ref[0, hd] = kt.T.astype(BF16)
    av = seg(OFF_AV, OFF_BCQ).astype(BF16)
    for j in range(tm // A_TILE):
        avT_ref[0, j] = av[:, A_TILE * j:A_TILE * (j + 1)]

    cos_t = tab_ref[0:16, :]
    sin_t = tab_ref[16:32, :]
    cq = seg(OFF_BCQ, OFF_BCKV)
    rq = lax.rsqrt(jnp.mean(cq * cq, axis=0, keepdims=True) + EPS)
    wq = (wuqT_ref[...] * gq_ref[...]).astype(BF16)
    uq = jnp.dot(wq, cq.astype(BF16), preferred_element_type=F32)
    uq = uq * (rq * ((B_NOPE + B_ROPE) ** -0.5 * LOG2E))
    for hd in range(B_HEADS):
        b0 = (B_NOPE + B_ROPE) * hd
        x1 = uq[b0 + 64:b0 + 80]
        x2 = uq[b0 + 80:b0 + 96]
        qb = Q_MARGIN * norm(uq[b0:b0 + 96])
        qn_ref[0, hd:hd + 1, :] = qb
        bqT_ref[0, 128 * hd:128 * hd + 64, :] = uq[b0:b0 + 64].astype(BF16)
        bqT_ref[0, 128 * hd + 64:128 * hd + 80, :] = (x1 * cos_t - x2 * sin_t).astype(BF16)
        bqT_ref[0, 128 * hd + 80:128 * hd + 96, :] = (x1 * sin_t + x2 * cos_t).astype(BF16)
        bqT_ref[0, 128 * hd + 96:128 * hd + 128, :] = first_row(32, -qb).astype(BF16)
    ckv = seg(OFF_BCKV, OFF_BKR)
    rkv = lax.rsqrt(jnp.mean(ckv * ckv, axis=0, keepdims=True) + EPS)
    ckv_b = ckv.astype(BF16)
    gkv = gkv_ref[...]
    kn = jnp.dot((wukT_ref[...] * gkv).astype(BF16), ckv_b, preferred_element_type=F32) * rkv
    vv = jnp.dot((wuvT_ref[...] * gkv).astype(BF16), ckv_b, preferred_element_type=F32) * rkv
    bvT_ref[0, 0] = vv.astype(BF16)
    kr = seg(OFF_BKR, OFF_CQ)
    kr1 = kr[0:16]
    kr2 = kr[16:32]
    kro = jnp.concatenate([kr1 * cos_t - kr2 * sin_t, kr1 * sin_t + kr2 * cos_t,
                           one_row32], axis=0)
    kr_sq = jnp.sum(kr * kr, axis=0, keepdims=True)
    for hd in range(B_HEADS):
        knh = kn[64 * hd:64 * hd + 64]
        kn_ref[0, hd:hd + 1, :] = jnp.sqrt(jnp.sum(knh * knh, axis=0, keepdims=True) + kr_sq)
        kt = jnp.concatenate([knh, kro], axis=0)
        bk_ref[0, hd] = kt.T.astype(BF16)

    def normrope(xh, c, s):
        r = lax.rsqrt(jnp.mean(xh * xh, axis=0, keepdims=True) + EPS)
        rot = jnp.concatenate([xh[16:32], xh[0:16], xh[48:64], xh[32:48]], axis=0)
        return (xh * c + rot * s) * r

    cqc = tab_ref[32:96, :]
    cqs = tab_ref[96:160, :]
    cq2 = seg(OFF_CQ, OFF_CK)
    for hd in range(C_Q_HEADS):
        qh = normrope(cq2[64 * hd:64 * hd + 64], cqc, cqs)
        qb = Q_MARGIN * norm(qh)
        qn_ref[0, B_HEADS + hd:B_HEADS + hd + 1, :] = qb
        cqT_ref[0, 128 * hd:128 * hd + 64, :] = qh.astype(BF16)
        cqT_ref[0, 128 * hd + 64:128 * hd + 128, :] = first_row(64, -qb).astype(BF16)
    ckc = tab_ref[160:224, :]
    cks = tab_ref[224:288, :]
    ck2 = seg(OFF_CK, OFF_CV)
    for hd in range(C_KV_HEADS):
        kh = normrope(ck2[64 * hd:64 * hd + 64], ckc, cks)
        kn_ref[0, B_HEADS + hd:B_HEADS + hd + 1, :] = norm(kh)
        kt = jnp.concatenate([kh, one_row64], axis=0)
        ck_ref[0, hd] = kt.T.astype(BF16)
    cvT_ref[0, 0] = seg(OFF_CV, OFF_END).astype(BF16)


def _proj_call(x2d, mod, winT, wuqT, gq, wukT, wuvT, gkv, tab):
    tm = TOK_TILE
    T = x2d.shape[0]
    nj = SEQ // tm
    const2 = lambda i: (0, 0)
    bj3 = lambda i: (i // nj, 0, i % nj)
    bj4 = lambda i: (i // nj, 0, i % nj, 0)
    cj4 = lambda i: (i // nj, i % nj, 0, 0)
    out_shape = (
        jax.ShapeDtypeStruct((BATCH, 512, SEQ), BF16),
        jax.ShapeDtypeStruct((BATCH, A_HEADS, SEQ, 128), BF16),
        jax.ShapeDtypeStruct((BATCH, SEQ // A_TILE, 256, A_TILE), BF16),
        jax.ShapeDtypeStruct((BATCH, 512, SEQ), BF16),
        jax.ShapeDtypeStruct((BATCH, B_HEADS, SEQ, 128), BF16),
        jax.ShapeDtypeStruct((BATCH, SEQ // KV_TILE, 256, KV_TILE), BF16),
        jax.ShapeDtypeStruct((BATCH, 1024, SEQ), BF16),
        jax.ShapeDtypeStruct((BATCH, C_KV_HEADS, SEQ, 128), BF16),
        jax.ShapeDtypeStruct((BATCH, SEQ // KV_TILE, 128, KV_TILE), BF16),
        jax.ShapeDtypeStruct((BATCH, 16, SEQ), F32),
        jax.ShapeDtypeStruct((BATCH, 16, SEQ), F32),
    )
    out_specs = (
        pl.BlockSpec((1, 512, tm), bj3),
        pl.BlockSpec((1, A_HEADS, tm, 128), bj4),
        pl.BlockSpec((1, tm // A_TILE, 256, A_TILE), cj4),
        pl.BlockSpec((1, 512, tm), bj3),
        pl.BlockSpec((1, B_HEADS, tm, 128), bj4),
        pl.BlockSpec((1, tm // KV_TILE, 256, KV_TILE), cj4),
        pl.BlockSpec((1, 1024, tm), bj3),
        pl.BlockSpec((1, C_KV_HEADS, tm, 128), bj4),
        pl.BlockSpec((1, tm // KV_TILE, 128, KV_TILE), cj4),
        pl.BlockSpec((1, 16, tm), bj3),
        pl.BlockSpec((1, 16, tm), bj3),
    )
    in_specs = [
        pl.BlockSpec((tm, D_MODEL), lambda i: (i, 0)),
        pl.BlockSpec((1, 6, D_MODEL), lambda i: (i // nj, 0, 0)),
        pl.BlockSpec((IN_COLS, D_MODEL), const2),
        pl.BlockSpec((B_Q_RANK, B_Q_RANK), const2),
        pl.BlockSpec((1, B_Q_RANK), const2),
        pl.BlockSpec((256, B_KV_RANK), const2),
        pl.BlockSpec((256, B_KV_RANK), const2),
        pl.BlockSpec((1, B_KV_RANK), const2),
        pl.BlockSpec((288, tm), lambda i: (0, i % nj)),
    ]
    return pl.pallas_call(
        _proj_kernel,
        out_shape=out_shape,
        grid=(T // tm,),
        in_specs=in_specs,
        out_specs=out_specs,
        compiler_params=pltpu.CompilerParams(vmem_limit_bytes=VMEM_LIMIT),
        name="in_proj",
    )(x2d, mod, winT, wuqT, gq, wukT, wuvT, gkv, tab)


def _with_ones_rows(v):
    return jnp.concatenate([v, jnp.ones((16, v.shape[1]), BF16)], axis=0)


def _attn_finish(o_ref, acc, groups, tq):
    o = acc[0:64] / acc[64:65]
    for g in range(groups):
        o_ref[0, 64 * g:64 * g + 64, :] = o[:, tq * g:tq * (g + 1)].astype(o_ref.dtype)


def _attn_online(q_strips, k_ref, vT_ref, o_ref, acc_ref, s_ref, p_ref, *, groups, tq, tk):
    nstrip = len(q_strips)
    w = q_strips[0].shape[1]
    nchunk = SEQ // tk
    rb = ROW_BLK
    nb = tk // rb
    nunit = nchunk * nstrip
    acc_ref[...] = jnp.zeros(acc_ref.shape, F32)
    m = [jnp.full((1, w), NEG_BIG, F32) for _ in range(nstrip)]

    def qk_block(u, b):
        c, j = divmod(u, nstrip)
        k = k_ref[0, 0, c * tk + rb * b:c * tk + rb * (b + 1), :]
        return jnp.dot(k, q_strips[j], preferred_element_type=F32)

    def pv_block(u, b, p_blk):
        c = u // nstrip
        v_aug = _with_ones_rows(vT_ref[0, c, :, rb * b:rb * (b + 1)])
        return jnp.dot(v_aug, p_blk, preferred_element_type=F32)

    alphas = {}
    for t in range(nunit + 2):
        pv = None
        for b in range(nb):
            rows = slice(rb * b, rb * (b + 1))
            if t < nunit:
                s_ref[t % RING, rows, :] = qk_block(t, b)
            if t >= 2:
                part = pv_block(t - 2, b, p_ref[(t - 2) % RING, rows, :])
                pv = part if pv is None else pv + part
        if 1 <= t <= nunit:
            u = t - 1
            j = u % nstrip
            s_u = s_ref[u % RING]
            m_new = jnp.maximum(m[j], jnp.max(s_u, axis=0, keepdims=True))
            alphas[u] = jnp.exp2(m[j] - m_new)
            p_ref[u % RING] = jnp.exp2(s_u - m_new).astype(BF16)
            m[j] = m_new
        if t >= 2:
            u = t - 2
            sl = slice(w * (u % nstrip), w * (u % nstrip + 1))
            acc_ref[:, sl] = alphas.pop(u) * acc_ref[:, sl] + pv
    _attn_finish(o_ref, acc_ref[...], groups, tq)


def _attn_bounded(q_strips, k_ref, vT_ref, o_ref, *, groups, tq, tk):
    nstrip = len(q_strips)
    rb = ROW_BLK
    nkb = SEQ // rb
    units = [(kb, j) for kb in range(nkb) for j in range(nstrip)]
    acc = [None] * nstrip
    probs = {}

    def pv(u):
        kb, j = units[u]
        c, b = divmod(kb, tk // rb)
        v_aug = _with_ones_rows(vT_ref[0, c, :, rb * b:rb * (b + 1)])
        part = jnp.dot(v_aug, probs.pop(u), preferred_element_type=F32)
        acc[j] = part if acc[j] is None else acc[j] + part

    for t in range(len(units)):
        kb, j = units[t]
        s = jnp.dot(k_ref[0, 0, rb * kb:rb * (kb + 1), :], q_strips[j], preferred_element_type=F32)
        probs[t] = jnp.exp2(s).astype(BF16)
        if t >= PV_LAG:
            pv(t - PV_LAG)
    for u in range(len(units) - PV_LAG, len(units)):
        pv(u)
    _attn_finish(o_ref, jnp.concatenate(acc, axis=1), groups, tq)


def _full_attn_kernel(scal_ref, qT_ref, k_ref, vT_ref, o_ref, acc_ref, s_ref, p_ref,
                      *, groups, tq, tk, bound_row):
    n = groups * tq
    w = min(n, STRIP)
    nstrip = n // w
    pair = 2 * (pl.program_id(0) * pl.num_programs(1) + pl.program_id(1))
    k_bound = scal_ref[pair]
    bounded_ok = scal_ref[pair + 1] > 0.5
    cols = [qT_ref[0, 128 * g:128 * (g + 1), :] for g in range(groups)]
    q_cat = cols[0] if groups == 1 else jnp.concatenate(cols, axis=1)
    q_lo = q_cat[0:bound_row]
    q_bound = q_cat[bound_row:bound_row + 16]
    q_hi = q_cat[bound_row + 16:]

    def strips(mid):
        q = jnp.concatenate([q_lo, mid, q_hi], axis=0)
        return [q[:, w * j:w * (j + 1)] for j in range(nstrip)]

    @pl.when(bounded_ok)
    def _():
        mid = (q_bound.astype(F32) * k_bound).astype(BF16)
        _attn_bounded(strips(mid), k_ref, vT_ref, o_ref, groups=groups, tq=tq, tk=tk)

    @pl.when(jnp.logical_not(bounded_ok))
    def _():
        _attn_online(strips(jnp.zeros_like(q_bound)), k_ref, vT_ref, o_ref, acc_ref, s_ref, p_ref,
                     groups=groups, tq=tq, tk=tk)


def _bound_scalars(qn, kn, groups):
    k_bound = K_MARGIN * jnp.max(kn, axis=-1)
    q_max = jnp.max(qn, axis=-1).reshape(BATCH, kn.shape[1], groups).max(-1)
    ok = (q_max * k_bound <= BOUND_LIMIT).astype(F32)
    return jnp.stack([k_bound, ok], axis=-1).reshape(-1)


def _full_attn_call(scal, qT, k, vT, *, groups, tq, bound_row, name):
    tk = KV_TILE
    hkv = k.shape[1]
    n = groups * tq
    kern = functools.partial(_full_attn_kernel, groups=groups, tq=tq, tk=tk, bound_row=bound_row)
    return pl.pallas_call(
        kern,
        out_shape=jax.ShapeDtypeStruct((BATCH, hkv * groups * 64, SEQ), BF16),
        grid=(BATCH, hkv, SEQ // tq),
        in_specs=[
            pl.BlockSpec(memory_space=pltpu.SMEM),
            pl.BlockSpec((1, groups * 128, tq), lambda b, g, i: (b, g, i)),
            pl.BlockSpec((1, 1, SEQ, 128), lambda b, g, i: (b, g, 0, 0)),
            pl.BlockSpec((1, SEQ // tk, 64, tk), lambda b, g, i: (b, 0, g, 0)),
        ],
        out_specs=pl.BlockSpec((1, groups * 64, tq), lambda b, g, i: (b, g, i)),
        scratch_shapes=[pltpu.VMEM((80, n), F32),
                        pltpu.VMEM((RING, tk, min(n, STRIP)), F32),
                        pltpu.VMEM((RING, tk, min(n, STRIP)), BF16)],
        compiler_params=pltpu.CompilerParams(vmem_limit_bytes=VMEM_LIMIT),
        name=name,
    )(scal, qT, k, vT)


def _dil_attn_kernel(scal_ref, qT_ref, k_ref, vT_ref, bias_ref, o_ref):
    t = A_TILE
    nq = SEQ // t
    i = pl.program_id(1)
    first = jnp.clip(i - A_CENTER, 0, nq - A_NCHUNK)
    base = A_SCAL * pl.program_id(0)
    bounded_ok = scal_ref[base] > 0.5
    top_row = lax.broadcasted_iota(jnp.int32, (16, t), 0) == 0

    def q_head(h, mid_fn):
        q = qT_ref[0, 128 * h:128 * (h + 1), :]
        return jnp.concatenate([q[0:HEAD_DIM], mid_fn(h, q[HEAD_DIM:HEAD_DIM + 16]), q[HEAD_DIM + 16:]], axis=0)

    def score_tile(h, r, q):
        kt = first + r
        d = kt - i + A_CENTER
        bi = jnp.where((d >= 0) & (d < A_NCHUNK), d, A_NCHUNK)
        k = k_ref[0, h, pl.ds(pl.multiple_of(kt * t, t), t), :]
        return jnp.dot(k, q, preferred_element_type=F32) + bias_ref[h, bi]

    def pv_tile(h, r, p):
        v_aug = _with_ones_rows(vT_ref[0, first + r, 64 * h:64 * (h + 1), :])
        return jnp.dot(v_aug, p, preferred_element_type=F32)

    def store(h, acc):
        o_ref[0, 64 * h:64 * (h + 1), :] = (acc[0:64] / acc[64:65]).astype(o_ref.dtype)

    @pl.when(bounded_ok)
    def _():
        def mid(h, q_bound):
            shift = q_bound.astype(F32) * scal_ref[base + 1 + h]
            return (shift - jnp.where(top_row, scal_ref[base + 1 + A_HEADS + h], 0.0)).astype(BF16)

        qs = [q_head(h, mid) for h in range(A_HEADS)]
        units = [(h, r) for r in range(A_NCHUNK) for h in range(A_HEADS)]
        acc = [None] * A_HEADS
        probs = {}

        def pv(u):
            h, r = units[u]
            part = pv_tile(h, r, probs.pop(u))
            acc[h] = part if acc[h] is None else acc[h] + part

        for u, (h, r) in enumerate(units):
            probs[u] = jnp.exp2(score_tile(h, r, qs[h])).astype(BF16)
            if u >= A_PV_LAG:
                pv(u - A_PV_LAG)
        for u in range(len(units) - A_PV_LAG, len(units)):
            pv(u)
        for h in range(A_HEADS):
            store(h, acc[h])

    @pl.when(jnp.logical_not(bounded_ok))
    def _():
        def scores(h):
            q = q_head(h, lambda _, q_bound: jnp.zeros_like(q_bound))
            return [score_tile(h, r, q) for r in range(A_NCHUNK)]

        def finish(h, s_list):
            m = jnp.max(functools.reduce(jnp.maximum, s_list), axis=0, keepdims=True)
            acc = None
            for r in range(A_NCHUNK):
                part = pv_tile(h, r, jnp.exp2(s_list[r] - m).astype(BF16))
                acc = part if acc is None else acc + part
            store(h, acc)

        s_next = scores(0)
        for h in range(A_HEADS):
            s_cur = s_next
            if h + 1 < A_HEADS:
                s_next = scores(h + 1)
            finish(h, s_cur)


def _dil_scalars(qn, kn, b_max, b_min):
    k_bound = K_MARGIN * jnp.max(kn, axis=-1)
    q_max = jnp.max(qn, axis=-1)
    spread = 2.0 * q_max * k_bound + (b_max - b_min)[None, :]
    ok = jnp.all(spread <= 2.0 * BOUND_LIMIT, axis=-1, keepdims=True).astype(F32)
    pad = jnp.zeros((BATCH, A_SCAL - 1 - 2 * A_HEADS), F32)
    return jnp.concatenate([ok, k_bound, jnp.broadcast_to(b_max[None, :], (BATCH, A_HEADS)), pad], axis=1).reshape(-1)


def _dil_attn_call(scal, qT, k, vT, bias):
    t = A_TILE
    return pl.pallas_call(
        _dil_attn_kernel,
        out_shape=jax.ShapeDtypeStruct((BATCH, A_HEADS * 64, SEQ), BF16),
        grid=(BATCH, SEQ // t),
        in_specs=[
            pl.BlockSpec(memory_space=pltpu.SMEM),
            pl.BlockSpec((1, A_HEADS * 128, t), lambda b, i: (b, 0, i)),
            pl.BlockSpec((1, A_HEADS, SEQ, 128), lambda b, i: (b, 0, 0, 0)),
            pl.BlockSpec((1, SEQ // t, A_HEADS * 64, t), lambda b, i: (b, 0, 0, 0)),
            pl.BlockSpec((A_HEADS, A_NCHUNK + 1, t, t), lambda b, i: (0, 0, 0, 0),
                         pipeline_mode=pl.Buffered(1)),
        ],
        out_specs=pl.BlockSpec((1, A_HEADS * 64, t), lambda b, i: (b, 0, i)),
        compiler_params=pltpu.CompilerParams(vmem_limit_bytes=VMEM_LIMIT),
        name="dilated_attn",
    )(scal, qT, k, vT, bias)


def _layer_norm(z, g, b):
    mu = jnp.mean(z, axis=-1, keepdims=True)
    d = z - mu
    var = jnp.mean(d * d, axis=-1, keepdims=True)
    return d * lax.rsqrt(var + EPS) * g + b


def _mlp_kernel(x_ref, mod_ref, aoT_ref, boT_ref, coT_ref, wo_ref, ln1g_ref, ln1b_ref,
                w1_ref, w2_ref, ln2g_ref, ln2b_ref, o_ref):
    g_a = mod_ref[0, 2:3, :]
    sh_m = mod_ref[0, 3:4, :]
    sc_m = mod_ref[0, 4:5, :]
    g_m = mod_ref[0, 5:6, :]
    catT = jnp.concatenate([aoT_ref[0], boT_ref[0], coT_ref[0]], axis=0)
    cat = catT.astype(F32).T.astype(BF16)
    y = jnp.dot(cat, wo_ref[...], preferred_element_type=F32)
    x1 = _layer_norm(ALPHA * x_ref[...] + (1.0 + g_a) * y, ln1g_ref[...], ln1b_ref[...])
    h = (x1 * (1.0 + sc_m) + sh_m).astype(BF16)
    y2 = None
    fc = 1024
    for j in range(D_FF // fc):
        u = jnp.dot(h, w1_ref[:, fc * j:fc * (j + 1)], preferred_element_type=F32)
        u = jnp.maximum(u, 0.0)
        u = (u * u).astype(BF16)
        part = jnp.dot(u, w2_ref[fc * j:fc * (j + 1), :], preferred_element_type=F32)
        y2 = part if y2 is None else y2 + part
    o_ref[...] = _layer_norm(ALPHA * x1 + (1.0 + g_m) * y2, ln2g_ref[...], ln2b_ref[...])


def _mlp_call(x2d, mod, aoT, boT, coT, wo, ln1g, ln1b, w1, w2, ln2g, ln2b):
    tm = TOK_TILE
    T = x2d.shape[0]
    nj = SEQ // tm
    const2 = lambda i: (0, 0)
    bj3 = lambda i: (i // nj, 0, i % nj)
    one = pl.Buffered(1)
    in_specs = [
        pl.BlockSpec((tm, D_MODEL), lambda i: (i, 0)),
        pl.BlockSpec((1, 6, D_MODEL), lambda i: (i // nj, 0, 0)),
        pl.BlockSpec((1, 256, tm), bj3),
        pl.BlockSpec((1, 256, tm), bj3),
        pl.BlockSpec((1, 512, tm), bj3),
        pl.BlockSpec((D_MODEL, D_MODEL), const2, pipeline_mode=one),
        pl.BlockSpec((1, D_MODEL), const2),
        pl.BlockSpec((1, D_MODEL), const2),
        pl.BlockSpec((D_MODEL, D_FF), const2, pipeline_mode=one),
        pl.BlockSpec((D_FF, D_MODEL), const2, pipeline_mode=one),
        pl.BlockSpec((1, D_MODEL), const2),
        pl.BlockSpec((1, D_MODEL), const2),
    ]
    return pl.pallas_call(
        _mlp_kernel,
        out_shape=jax.ShapeDtypeStruct((T, D_MODEL), F32),
        grid=(T // tm,),
        in_specs=in_specs,
        out_specs=pl.BlockSpec((tm, D_MODEL), lambda i: (i, 0)),
        compiler_params=pltpu.CompilerParams(vmem_limit_bytes=VMEM_LIMIT),
        name="out_mlp",
    )(x2d, mod, aoT, boT, coT, wo, ln1g, ln1b, w1, w2, ln2g, ln2b)


def _rope_angles(pos, dim):
    inv = ROPE_THETA ** (-jnp.arange(0, dim, 2, dtype=F32) / dim)
    return pos.astype(F32)[:, None] * inv[None, :]


def _t5_bucket(rel):
    nb = REL_BUCKETS // 2
    max_exact = nb // 2
    sign = jnp.where(rel > 0, nb, 0)
    n = jnp.abs(rel)
    nf = jnp.maximum(n, 1).astype(F32)
    large = max_exact + (jnp.log(nf / max_exact) / math.log(REL_MAX_DIST / max_exact)
                         * (nb - max_exact)).astype(jnp.int32)
    large = jnp.minimum(large, nb - 1)
    return sign + jnp.where(n < max_exact, n, large)


def _dilated_bias_tiles(rel_bias):
    t = A_TILE
    span = (A_CENTER + 1) * t
    deltas = np.arange(-span + 1, span)
    mult = np.zeros(deltas.shape, np.int32)
    for (w, d) in A_PATTERNS:
        half = w // (2 * d)
        mult += ((deltas % d == 0) & (np.abs(deltas) <= half * d)).astype(np.int32)
    bucket = _t5_bucket(jnp.asarray(deltas, jnp.int32))
    logm = jnp.asarray(np.log(np.maximum(mult, 1)), F32)
    tab = jnp.where(jnp.asarray(mult > 0)[:, None],
                    (rel_bias[bucket] + logm[:, None]) * LOG2E, NEG_BIG)
    valid = jnp.asarray(mult > 0)[:, None]
    b_max = jnp.max(jnp.where(valid, tab, -jnp.inf), axis=0)
    b_min = jnp.min(jnp.where(valid, tab, jnp.inf), axis=0)
    x = np.arange(2 * t)
    d = np.where(x <= t - 1, -x, 2 * t - x)
    d[t] = 0
    idx = (np.arange(A_NCHUNK)[:, None] - A_CENTER) * t + d[None, :] + span - 1
    u = jnp.moveaxis(tab[jnp.asarray(idx, jnp.int32)], -1, 0)
    u = jnp.concatenate([u, jnp.full((A_HEADS, 1, 2 * t), NEG_BIG, F32)], axis=1)
    return _toeplitz_call(u.reshape(A_HEADS * (A_NCHUNK + 1), 1, 2 * t)), b_max, b_min


def _toeplitz_kernel(u_ref, o_ref):
    t = A_TILE
    rows = jnp.broadcast_to(u_ref[0], (t, 2 * t))
    o_ref[0] = pltpu.roll(rows, 0, 1, stride=1, stride_axis=0)[:, :t]


def _toeplitz_call(u):
    t = A_TILE
    n = u.shape[0]
    tiles = pl.pallas_call(
        _toeplitz_kernel,
        out_shape=jax.ShapeDtypeStruct((n, t, t), F32),
        grid=(n,),
        in_specs=[pl.BlockSpec((1, 1, 2 * t), lambda i: (i, 0, 0))],
        out_specs=pl.BlockSpec((1, t, t), lambda i: (i, 0, 0)),
        name="bias_tiles",
    )(u)
    return tiles.reshape(A_HEADS, A_NCHUNK + 1, t, t)


def _rope_tables(gq, gk):
    t = jnp.arange(SEQ)
    ang_t = _rope_angles(t, B_ROPE)
    ang_row = _rope_angles(t // GRID_W, HEAD_DIM // 2)
    ang_col = _rope_angles(t % GRID_W, HEAD_DIM // 2)
    cos64 = jnp.concatenate([jnp.cos(ang_row)] * 2 + [jnp.cos(ang_col)] * 2, axis=1)
    sin_r, sin_c = jnp.sin(ang_row), jnp.sin(ang_col)
    sin64 = jnp.concatenate([-sin_r, sin_r, -sin_c, sin_c], axis=1)
    perm = np.concatenate([np.arange(16, 32), np.arange(0, 16), np.arange(48, 64), np.arange(32, 48)])

    def with_gain(g, scale):
        return (cos64 * g[None, :] * scale).T, (sin64 * g[perm][None, :] * scale).T

    qc, qs = with_gain(gq, HEAD_DIM ** -0.5 * LOG2E)
    kc, ks = with_gain(gk, 1.0)
    return jnp.concatenate([jnp.cos(ang_t).T, jnp.sin(ang_t).T, qc, qs, kc, ks], axis=0)


def kernel(x, c, w_ada, b_ada, w_in, mla_q_norm, mla_w_uq, mla_kv_norm, mla_w_ukv,
           gqa_q_norm, gqa_k_norm, rel_bias, w_o, ln1_g, ln1_b, w1, w2, ln2_g, ln2_b):
    B, S, D = x.shape
    c_pad = jnp.zeros((8, D), F32).at[:B].set(c)
    mod_all = _ada_call(c_pad, w_ada, b_ada)[:, :B].reshape(DEPTH, B, 6, D)
    bias, b_max, b_min = _dilated_bias_tiles(rel_bias)

    x2d = x.reshape(B * S, D)
    for l in range(DEPTH):
        winT = w_in[l].T.astype(BF16)
        wuqT = mla_w_uq[l].T
        wukv = mla_w_ukv[l].reshape(B_KV_RANK, B_HEADS, B_NOPE + B_V)
        wukT = wukv[:, :, :B_NOPE].reshape(B_KV_RANK, B_HEADS * B_NOPE).T
        wuvT = wukv[:, :, B_NOPE:].reshape(B_KV_RANK, B_HEADS * B_V).T
        tab = _rope_tables(gqa_q_norm[l], gqa_k_norm[l])
        aqT, ak, avT, bqT, bk, bvT, cqT, ck, cvT, qn, kn = _proj_call(
            x2d, mod_all[l], winT, wuqT, mla_q_norm[l][None, :], wukT, wuvT,
            mla_kv_norm[l][None, :], tab)
        a_scal = _dil_scalars(qn[:, QN_A:QN_A + A_HEADS], kn[:, KN_A:KN_A + A_HEADS], b_max, b_min)
        aoT = _dil_attn_call(a_scal, aqT, ak, avT, bias)
        cg = C_Q_HEADS // C_KV_HEADS
        b_scal = _bound_scalars(qn[:, :B_HEADS], kn[:, :B_HEADS], 1)
        c_scal = _bound_scalars(qn[:, B_HEADS:B_HEADS + C_Q_HEADS], kn[:, B_HEADS:B_HEADS + C_KV_HEADS], cg)
        boT = _full_attn_call(b_scal, bqT, bk, bvT, groups=1, tq=1024, bound_row=B_NOPE + B_ROPE,
                              name="mla_attn")
        coT = _full_attn_call(c_scal, cqT, ck, cvT, groups=cg, tq=256, bound_row=HEAD_DIM, name="gqa_attn")
        x2d = _mlp_call(x2d, mod_all[l], aoT, boT, coT, w_o[l].astype(BF16),
                        ln1_g[l][None, :], ln1_b[l][None, :], w1[l].astype(BF16), w2[l].astype(BF16),
                        ln2_g[l][None, :], ln2_b[l][None, :])
    return x2d.reshape(B, S, D)
```

```python
import functools
import math

import numpy as np
import jax
import jax.numpy as jnp
from jax import lax
from jax.experimental import pallas as pl
from jax.experimental.pallas import tpu as pltpu

D_MODEL = 1024
BATCH = 4
SEQ = 4096
DEPTH = 2
HEAD_DIM = 64
A_HEADS = 4
A_PATTERNS = ((128, 1), (512, 4), (2048, 16))
B_HEADS = 4
B_Q_RANK = 384
B_KV_RANK = 256
B_NOPE = 64
B_ROPE = 32
B_V = 64
C_Q_HEADS = 8
C_KV_HEADS = 2
GRID_W = 64
ROPE_THETA = 10000.0
REL_BUCKETS = 32
REL_MAX_DIST = 1024
D_FF = 4 * D_MODEL
EPS = 1e-6
NEG_BIG = -1e30
ALPHA = (2 * DEPTH) ** 0.25
A_W = A_HEADS * HEAD_DIM
IN_COLS = 2208
LOG2E = 1.4426950408889634

F32 = jnp.float32
BF16 = jnp.bfloat16

LANES = 128
VMEM_LIMIT = 56 * 1024 * 1024

OFF_AQ, OFF_AK, OFF_AV = 0, 256, 512
OFF_BCQ, OFF_BCKV, OFF_BKR = 768, 1152, 1408
OFF_CQ, OFF_CK, OFF_CV, OFF_END = 1440, 1952, 2080, 2208

TOK_TILE = 512
A_TILE = 256
A_NCHUNK = 9
A_CENTER = 4
KV_TILE = 512
ROW_BLK = 256
QN_A = 12
KN_A = 8
A_SCAL = 12
A_PV_LAG = 4
PV_LAG = 2
RING = 3
STRIP = 512
Q_MARGIN = 1.02
K_MARGIN = 1.01
BOUND_LIMIT = 50.0


def _nt_dot(a, b):
    return lax.dot_general(a, b, (((1,), (1,)), ((), ())), preferred_element_type=F32)


def _ada_kernel(c_ref, w_ref, b_ref, o_ref):
    c = c_ref[...]
    ca = (c / (1.0 + jnp.exp(-c))).astype(BF16)
    w = w_ref[0].astype(BF16)
    o_ref[0] = jnp.dot(ca, w, preferred_element_type=F32) + b_ref[0]


def _ada_call(c_pad, w_ada, b_ada):
    tn = 1536
    n = 6 * D_MODEL
    return pl.pallas_call(
        _ada_kernel,
        out_shape=jax.ShapeDtypeStruct((DEPTH, 8, n), F32),
        grid=(DEPTH, n // tn),
        in_specs=[
            pl.BlockSpec((8, D_MODEL), lambda l, j: (0, 0)),
            pl.BlockSpec((1, D_MODEL, tn), lambda l, j: (l, 0, j)),
            pl.BlockSpec((1, 1, tn), lambda l, j: (l, 0, j)),
        ],
        out_specs=pl.BlockSpec((1, 8, tn), lambda l, j: (l, 0, j)),
        compiler_params=pltpu.CompilerParams(vmem_limit_bytes=VMEM_LIMIT),
        name="ada_mod",
    )(c_pad, w_ada, b_ada.reshape(DEPTH, 1, n))


def _proj_kernel(x_ref, mod_ref, winT_ref, wuqT_ref, gq_ref, wukT_ref, wuvT_ref, gkv_ref, tab_ref,
                 aqT_ref, ak_ref, avT_ref, bqT_ref, bk_ref, bvT_ref, cqT_ref, ck_ref, cvT_ref,
                 qn_ref, kn_ref):
    tm = x_ref.shape[0]
    sh = mod_ref[0, 0:1, :]
    sc = mod_ref[0, 1:2, :]
    h = (x_ref[...] * (1.0 + sc) + sh).astype(BF16)

    proj_t = _nt_dot(winT_ref[...], h)

    def seg(r0, r1):
        return proj_t[r0:r1]

    def first_row(rows, val):
        return jnp.where(lax.broadcasted_iota(jnp.int32, (rows, tm), 0) == 0, val, 0.0)

    def norm(xh):
        return jnp.sqrt(jnp.sum(xh * xh, axis=0, keepdims=True))

    one_row32 = first_row(32, 1.0)
    one_row64 = first_row(64, 1.0)
    qn_ref[0] = jnp.zeros(qn_ref.shape[1:], F32)
    kn_ref[0] = jnp.zeros(kn_ref.shape[1:], F32)

    aq = seg(OFF_AQ, OFF_AK) * (HEAD_DIM ** -0.5 * LOG2E)
    for hd in range(A_HEADS):
        qh = aq[64 * hd:64 * hd + 64]
        qb = Q_MARGIN * norm(qh)
        qn_ref[0, QN_A + hd:QN_A + hd + 1, :] = qb
        aqT_ref[0, 128 * hd:128 * hd + 64, :] = qh.astype(BF16)
        aqT_ref[0, 128 * hd + 64:128 * hd + 128, :] = first_row(64, -qb).astype(BF16)
    ak = seg(OFF_AK, OFF_AV)
    for hd in range(A_HEADS):
        kh = ak[64 * hd:64 * hd + 64]
        kn_ref[0, KN_A + hd:KN_A + hd + 1, :] = norm(kh)
        kt = jnp.concatenate([kh, one_row64], axis=0)
        ak_ref[0, hd] = kt.T.astype(BF16)
    av = seg(OFF_AV, OFF_BCQ).astype(BF16)
    for j in range(tm // A_TILE):
        avT_ref[0, j] = av[:, A_TILE * j:A_TILE * (j + 1)]

    def normrope(xh, c, s):
        r = lax.rsqrt(jnp.mean(xh * xh, axis=0, keepdims=True) + EPS)
        rot = jnp.concatenate([xh[16:32], xh[0:16], xh[48:64], xh[32:48]], axis=0)
        return (xh * c + rot * s) * r

    cqc = tab_ref[32:96, :]
    cqs = tab_ref[96:160, :]
    cq2 = seg(OFF_CQ, OFF_CK)
    for hd in range(C_Q_HEADS):
        qh = normrope(cq2[64 * hd:64 * hd + 64], cqc, cqs)
        qb = Q_MARGIN * norm(qh)
        qn_ref[0, B_HEADS + hd:B_HEADS + hd + 1, :] = qb
        cqT_ref[0, 128 * hd:128 * hd + 64, :] = qh.astype(BF16)
        cqT_ref[0, 128 * hd + 64:128 * hd + 128, :] = first_row(64, -qb).astype(BF16)
    ckc = tab_ref[160:224, :]
    cks = tab_ref[224:288, :]
    ck2 = seg(OFF_CK, OFF_CV)
    for hd in range(C_KV_HEADS):
        kh = normrope(ck2[64 * hd:64 * hd + 64], ckc, cks)
        kn_ref[0, B_HEADS + hd:B_HEADS + hd + 1, :] = norm(kh)
        kt = jnp.concatenate([kh, one_row64], axis=0)
        ck_ref[0, hd] = kt.T.astype(BF16)
    cvT_ref[0, 0] = seg(OFF_CV, OFF_END).astype(BF16)

    cos_t = tab_ref[0:16, :]
    sin_t = tab_ref[16:32, :]
    cq = seg(OFF_BCQ, OFF_BCKV)
    rq = lax.rsqrt(jnp.mean(cq * cq, axis=0, keepdims=True) + EPS)
    wq = (wuqT_ref[...] * gq_ref[...]).astype(BF16)
    uq = jnp.dot(wq, cq.astype(BF16), preferred_element_type=F32)
    uq = uq * (rq * ((B_NOPE + B_ROPE) ** -0.5 * LOG2E))
    for hd in range(B_HEADS):
        b0 = (B_NOPE + B_ROPE) * hd
        x1 = uq[b0 + 64:b0 + 80]
        x2 = uq[b0 + 80:b0 + 96]
        qb = Q_MARGIN * norm(uq[b0:b0 + 96])
        qn_ref[0, hd:hd + 1, :] = qb
        bqT_ref[0, 128 * hd:128 * hd + 64, :] = uq[b0:b0 + 64].astype(BF16)
        bqT_ref[0, 128 * hd + 64:128 * hd + 80, :] = (x1 * cos_t - x2 * sin_t).astype(BF16)
        bqT_ref[0, 128 * hd + 80:128 * hd + 96, :] = (x1 * sin_t + x2 * cos_t).astype(BF16)
        bqT_ref[0, 128 * hd + 96:128 * hd + 128, :] = first_row(32, -qb).astype(BF16)
    ckv = seg(OFF_BCKV, OFF_BKR)
    rkv = lax.rsqrt(jnp.mean(ckv * ckv, axis=0, keepdims=True) + EPS)
    ckv_b = ckv.astype(BF16)
    gkv = gkv_ref[...]
    kn = jnp.dot((wukT_ref[...] * gkv).astype(BF16), ckv_b, preferred_element_type=F32) * rkv
    vv = jnp.dot((wuvT_ref[...] * gkv).astype(BF16), ckv_b, preferred_element_type=F32) * rkv
    bvT_ref[0, 0] = vv.astype(BF16)
    kr = seg(OFF_BKR, OFF_CQ)
    kr1 = kr[0:16]
    kr2 = kr[16:32]
    kro = jnp.concatenate([kr1 * cos_t - kr2 * sin_t, kr1 * sin_t + kr2 * cos_t,
                           one_row32], axis=0)
    kr_sq = jnp.sum(kr * kr, axis=0, keepdims=True)
    for hd in range(B_HEADS):
        knh = kn[64 * hd:64 * hd + 64]
        kn_ref[0, hd:hd + 1, :] = jnp.sqrt(jnp.sum(knh * knh, axis=0, keepdims=True) + kr_sq)
        kt = jnp.concatenate([knh, kro], axis=0)
        bk_ref[0, hd] = kt.T.astype(BF16)


def _proj_call(x2d, mod, winT, wuqT, gq, wukT, wuvT, gkv, tab):
    tm = TOK_TILE
    T = x2d.shape[0]
    nj = SEQ // tm
    const2 = lambda i: (0, 0)
    bj3 = lambda i: (i // nj, 0, i % nj)
    bj4 = lambda i: (i // nj, 0, i % nj, 0)
    cj4 = lambda i: (i // nj, i % nj, 0, 0)
    out_shape = (
        jax.ShapeDtypeStruct((BATCH, 512, SEQ), BF16),
        jax.ShapeDtypeStruct((BATCH, A_HEADS, SEQ, 128), BF16),
        jax.ShapeDtypeStruct((BATCH, SEQ // A_TILE, 256, A_TILE), BF16),
        jax.ShapeDtypeStruct((BATCH, 512, SEQ), BF16),
        jax.ShapeDtypeStruct((BATCH, B_HEADS, SEQ, 128), BF16),
        jax.ShapeDtypeStruct((BATCH, SEQ // KV_TILE, 256, KV_TILE), BF16),
        jax.ShapeDtypeStruct((BATCH, 1024, SEQ), BF16),
        jax.ShapeDtypeStruct((BATCH, C_KV_HEADS, SEQ, 128), BF16),
        jax.ShapeDtypeStruct((BATCH, SEQ // KV_TILE, 128, KV_TILE), BF16),
        jax.ShapeDtypeStruct((BATCH, 16, SEQ), F32),
        jax.ShapeDtypeStruct((BATCH, 16, SEQ), F32),
    )
    out_specs = (
        pl.BlockSpec((1, 512, tm), bj3),
        pl.BlockSpec((1, A_HEADS, tm, 128), bj4),
        pl.BlockSpec((1, tm // A_TILE, 256, A_TILE), cj4),
        pl.BlockSpec((1, 512, tm), bj3),
        pl.BlockSpec((1, B_HEADS, tm, 128), bj4),
        pl.BlockSpec((1, tm // KV_TILE, 256, KV_TILE), cj4),
        pl.BlockSpec((1, 1024, tm), bj3),
        pl.BlockSpec((1, C_KV_HEADS, tm, 128), bj4),
        pl.BlockSpec((1, tm // KV_TILE, 128, KV_TILE), cj4),
        pl.BlockSpec((1, 16, tm), bj3),
        pl.BlockSpec((1, 16, tm), bj3),
    )
    in_specs = [
        pl.BlockSpec((tm, D_MODEL), lambda i: (i, 0)),
        pl.BlockSpec((1, 6, D_MODEL), lambda i: (i // nj, 0, 0)),
        pl.BlockSpec((IN_COLS, D_MODEL), const2),
        pl.BlockSpec((B_Q_RANK, B_Q_RANK), const2),
        pl.BlockSpec((1, B_Q_RANK), const2),
        pl.BlockSpec((256, B_KV_RANK), const2),
        pl.BlockSpec((256, B_KV_RANK), const2),
        pl.BlockSpec((1, B_KV_RANK), const2),
        pl.BlockSpec((288, tm), lambda i: (0, i % nj)),
    ]
    return pl.pallas_call(
        _proj_kernel,
        out_shape=out_shape,
        grid=(T // tm,),
        in_specs=in_specs,
        out_specs=out_specs,
        compiler_params=pltpu.CompilerParams(vmem_limit_bytes=VMEM_LIMIT),
        name="in_proj",
    )(x2d, mod, winT, wuqT, gq, wukT, wuvT, gkv, tab)


def _with_ones_rows(v):
    return jnp.concatenate([v, jnp.ones((16, v.shape[1]), BF16)], axis=0)


def _sublane_partial_sum(p):
    return functools.reduce(lambda a, b: a + b, [p[8 * i:8 * (i + 1)] for i in range(p.shape[0] // 8)])


def _attn_finish(o_ref, num, den, groups, tq):
    o = num / den
    for g in range(groups):
        o_ref[0, 64 * g:64 * g + 64, :] = o[:, tq * g:tq * (g + 1)].astype(o_ref.dtype)


def _attn_online(q_strips, k_ref, vT_ref, o_ref, acc_ref, s_ref, p_ref, *, groups, tq, tk):
    nstrip = len(q_strips)
    w = q_strips[0].shape[1]
    nchunk = SEQ // tk
    rb = ROW_BLK
    nb = tk // rb
    nunit = nchunk * nstrip
    acc_ref[...] = jnp.zeros(acc_ref.shape, F32)
    m = [jnp.full((1, w), NEG_BIG, F32) for _ in range(nstrip)]

    def qk_block(u, b):
        c, j = divmod(u, nstrip)
        k = k_ref[0, 0, c * tk + rb * b:c * tk + rb * (b + 1), :]
        return jnp.dot(k, q_strips[j], preferred_element_type=F32)

    def pv_block(u, b, p_blk):
        c = u // nstrip
        v_aug = _with_ones_rows(vT_ref[0, c, :, rb * b:rb * (b + 1)])
        return jnp.dot(v_aug, p_blk, preferred_element_type=F32)

    alphas = {}
    for t in range(nunit + 2):
        pv = None
        for b in range(nb):
            rows = slice(rb * b, rb * (b + 1))
            if t < nunit:
                s_ref[t % RING, rows, :] = qk_block(t, b)
            if t >= 2:
                part = pv_block(t - 2, b, p_ref[(t - 2) % RING, rows, :])
                pv = part if pv is None else pv + part
        if 1 <= t <= nunit:
            u = t - 1
            j = u % nstrip
            s_u = s_ref[u % RING]
            m_new = jnp.maximum(m[j], jnp.max(s_u, axis=0, keepdims=True))
            alphas[u] = jnp.exp2(m[j] - m_new)
            p_ref[u % RING] = jnp.exp2(s_u - m_new).astype(BF16)
            m[j] = m_new
        if t >= 2:
            u = t - 2
            sl = slice(w * (u % nstrip), w * (u % nstrip + 1))
            acc_ref[:, sl] = alphas.pop(u) * acc_ref[:, sl] + pv
    _attn_finish(o_ref, acc_ref[0:64, :], acc_ref[64:65, :], groups, tq)


def _attn_bounded(q_strips, k_ref, vT_ref, o_ref, *, groups, tq, tk):
    nstrip = len(q_strips)
    rb = ROW_BLK
    nkb = SEQ // rb
    units = [(kb, j) for kb in range(nkb) for j in range(nstrip)]
    acc = [None] * nstrip
    den = [None] * nstrip
    probs = {}

    def pv(u):
        kb, j = units[u]
        c, b = divmod(kb, tk // rb)
        part = jnp.dot(vT_ref[0, c, :, rb * b:rb * (b + 1)], probs.pop(u), preferred_element_type=F32)
        acc[j] = part if acc[j] is None else acc[j] + part

    for t in range(len(units)):
        kb, j = units[t]
        s = jnp.dot(k_ref[0, 0, rb * kb:rb * (kb + 1), :], q_strips[j], preferred_element_type=F32)
        p = jnp.exp2(s)
        part = _sublane_partial_sum(p)
        den[j] = part if den[j] is None else den[j] + part
        probs[t] = p.astype(BF16)
        if t >= PV_LAG:
            pv(t - PV_LAG)
    for u in range(len(units) - PV_LAG, len(units)):
        pv(u)
    den = [jnp.sum(d, axis=0, keepdims=True) for d in den]
    _attn_finish(o_ref, jnp.concatenate(acc, axis=1), jnp.concatenate(den, axis=1), groups, tq)


def _full_attn_kernel(scal_ref, qT_ref, k_ref, vT_ref, o_ref, acc_ref, s_ref, p_ref,
                      *, groups, tq, tk, bound_row):
    n = groups * tq
    w = min(n, STRIP)
    nstrip = n // w
    pair = 2 * (pl.program_id(0) * pl.num_programs(1) + pl.program_id(1))
    k_bound = scal_ref[pair]
    bounded_ok = scal_ref[pair + 1] > 0.5
    cols = [qT_ref[0, 128 * g:128 * (g + 1), :] for g in range(groups)]
    q_cat = cols[0] if groups == 1 else jnp.concatenate(cols, axis=1)
    q_lo = q_cat[0:bound_row]
    q_bound = q_cat[bound_row:bound_row + 16]
    q_hi = q_cat[bound_row + 16:]

    def strips(mid):
        q = jnp.concatenate([q_lo, mid, q_hi], axis=0)
        return [q[:, w * j:w * (j + 1)] for j in range(nstrip)]

    @pl.when(bounded_ok)
    def _():
        mid = (q_bound.astype(F32) * k_bound).astype(BF16)
        _attn_bounded(strips(mid), k_ref, vT_ref, o_ref, groups=groups, tq=tq, tk=tk)

    @pl.when(jnp.logical_not(bounded_ok))
    def _():
        _attn_online(strips(jnp.zeros_like(q_bound)), k_ref, vT_ref, o_ref, acc_ref, s_ref, p_ref,
                     groups=groups, tq=tq, tk=tk)


def _bound_scalars(qn, kn, groups):
    k_bound = K_MARGIN * jnp.max(kn, axis=-1)
    q_max = jnp.max(qn, axis=-1).reshape(BATCH, kn.shape[1], groups).max(-1)
    ok = (q_max * k_bound <= BOUND_LIMIT).astype(F32)
    return jnp.stack([k_bound, ok], axis=-1).reshape(-1)


def _full_attn_call(scal, qT, k, vT, *, groups, tq, bound_row, name):
    tk = KV_TILE
    hkv = k.shape[1]
    n = groups * tq
    kern = functools.partial(_full_attn_kernel, groups=groups, tq=tq, tk=tk, bound_row=bound_row)
    return pl.pallas_call(
        kern,
        out_shape=jax.ShapeDtypeStruct((BATCH, hkv * groups * 64, SEQ), BF16),
        grid=(BATCH, hkv, SEQ // tq),
        in_specs=[
            pl.BlockSpec(memory_space=pltpu.SMEM),
            pl.BlockSpec((1, groups * 128, tq), lambda b, g, i: (b, g, i)),
            pl.BlockSpec((1, 1, SEQ, 128), lambda b, g, i: (b, g, 0, 0)),
            pl.BlockSpec((1, SEQ // tk, 64, tk), lambda b, g, i: (b, 0, g, 0)),
        ],
        out_specs=pl.BlockSpec((1, groups * 64, tq), lambda b, g, i: (b, g, i)),
        scratch_shapes=[pltpu.VMEM((80, n), F32),
                        pltpu.VMEM((RING, tk, min(n, STRIP)), F32),
                        pltpu.VMEM((RING, tk, min(n, STRIP)), BF16)],
        compiler_params=pltpu.CompilerParams(vmem_limit_bytes=VMEM_LIMIT),
        name=name,
    )(scal, qT, k, vT)


def _dil_attn_kernel(scal_ref, qT_ref, k_ref, vT_ref, bias_ref, o_ref):
    t = A_TILE
    nq = SEQ // t
    i = pl.program_id(1)
    first = jnp.clip(i - A_CENTER, 0, nq - A_NCHUNK)
    base = A_SCAL * pl.program_id(0)
    bounded_ok = scal_ref[base] > 0.5
    top_row = lax.broadcasted_iota(jnp.int32, (16, t), 0) == 0

    def q_head(h, mid_fn):
        q = qT_ref[0, 128 * h:128 * (h + 1), :]
        return jnp.concatenate([q[0:HEAD_DIM], mid_fn(h, q[HEAD_DIM:HEAD_DIM + 16]), q[HEAD_DIM + 16:]], axis=0)

    def score_tile(h, r, q):
        kt = first + r
        d = kt - i + A_CENTER
        bi = jnp.where((d >= 0) & (d < A_NCHUNK), d, A_NCHUNK)
        k = k_ref[0, h, pl.ds(pl.multiple_of(kt * t, t), t), :]
        return jnp.dot(k, q, preferred_element_type=F32) + bias_ref[h, bi]

    def pv_tile(h, r, p, ones_rows):
        v = vT_ref[0, first + r, 64 * h:64 * (h + 1), :]
        return jnp.dot(_with_ones_rows(v) if ones_rows else v, p, preferred_element_type=F32)

    def store(h, num, den):
        o_ref[0, 64 * h:64 * (h + 1), :] = (num / den).astype(o_ref.dtype)

    @pl.when(bounded_ok)
    def _():
        def mid(h, q_bound):
            shift = q_bound.astype(F32) * scal_ref[base + 1 + h]
            return (shift - jnp.where(top_row, scal_ref[base + 1 + A_HEADS + h], 0.0)).astype(BF16)

        qs = [q_head(h, mid) for h in range(A_HEADS)]
        units = [(h, r) for r in range(A_NCHUNK) for h in range(A_HEADS)]
        acc = [None] * A_HEADS
        den = [None] * A_HEADS
        probs = {}

        def pv(u):
            h, r = units[u]
            part = pv_tile(h, r, probs.pop(u), False)
            acc[h] = part if acc[h] is None else acc[h] + part

        for u, (h, r) in enumerate(units):
            p = jnp.exp2(score_tile(h, r, qs[h]))
            part = _sublane_partial_sum(p)
            den[h] = part if den[h] is None else den[h] + part
            probs[u] = p.astype(BF16)
            if u >= A_PV_LAG:
                pv(u - A_PV_LAG)
        for u in range(len(units) - A_PV_LAG, len(units)):
            pv(u)
        for h in range(A_HEADS):
            store(h, acc[h], jnp.sum(den[h], axis=0, keepdims=True))

    @pl.when(jnp.logical_not(bounded_ok))
    def _():
        def scores(h):
            q = q_head(h, lambda _, q_bound: jnp.zeros_like(q_bound))
            return [score_tile(h, r, q) for r in range(A_NCHUNK)]

        def finish(h, s_list):
            m = jnp.max(functools.reduce(jnp.maximum, s_list), axis=0, keepdims=True)
            acc = None
            for r in range(A_NCHUNK):
                part = pv_tile(h, r, jnp.exp2(s_list[r] - m).astype(BF16), True)
                acc = part if acc is None else acc + part
            store(h, acc[0:64], acc[64:65])

        s_next = scores(0)
        for h in range(A_HEADS):
            s_cur = s_next
            if h + 1 < A_HEADS:
                s_next = scores(h + 1)
            finish(h, s_cur)


def _dil_scalars(qn, kn, b_max, b_min):
    k_bound = K_MARGIN * jnp.max(kn, axis=-1)
    q_max = jnp.max(qn, axis=-1)
    spread = 2.0 * q_max * k_bound + (b_max - b_min)[None, :]
    ok = jnp.all(spread <= 2.0 * BOUND_LIMIT, axis=-1, keepdims=True).astype(F32)
    pad = jnp.zeros((BATCH, A_SCAL - 1 - 2 * A_HEADS), F32)
    return jnp.concatenate([ok, k_bound, jnp.broadcast_to(b_max[None, :], (BATCH, A_HEADS)), pad], axis=1).reshape(-1)


def _dil_attn_call(scal, qT, k, vT, bias):
    t = A_TILE
    return pl.pallas_call(
        _dil_attn_kernel,
        out_shape=jax.ShapeDtypeStruct((BATCH, A_HEADS * 64, SEQ), BF16),
        grid=(BATCH, SEQ // t),
        in_specs=[
            pl.BlockSpec(memory_space=pltpu.SMEM),
            pl.BlockSpec((1, A_HEADS * 128, t), lambda b, i: (b, 0, i)),
            pl.BlockSpec((1, A_HEADS, SEQ, 128), lambda b, i: (b, 0, 0, 0)),
            pl.BlockSpec((1, SEQ // t, A_HEADS * 64, t), lambda b, i: (b, 0, 0, 0)),
            pl.BlockSpec((A_HEADS, A_NCHUNK + 1, t, t), lambda b, i: (0, 0, 0, 0),
                         pipeline_mode=pl.Buffered(1)),
        ],
        out_specs=pl.BlockSpec((1, A_HEADS * 64, t), lambda b, i: (b, 0, i)),
        compiler_params=pltpu.CompilerParams(vmem_limit_bytes=VMEM_LIMIT),
        name="dilated_attn",
    )(scal, qT, k, vT, bias)


def _layer_norm(z, g, b):
    mu = jnp.mean(z, axis=-1, keepdims=True)
    d = z - mu
    var = jnp.mean(d * d, axis=-1, keepdims=True)
    return d * lax.rsqrt(var + EPS) * g + b


def _mlp_kernel(x_ref, mod_ref, aoT_ref, boT_ref, coT_ref, wo_ref, ln1g_ref, ln1b_ref,
                w1_ref, w2_ref, ln2g_ref, ln2b_ref, o_ref):
    g_a = mod_ref[0, 2:3, :]
    sh_m = mod_ref[0, 3:4, :]
    sc_m = mod_ref[0, 4:5, :]
    g_m = mod_ref[0, 5:6, :]
    catT = jnp.concatenate([aoT_ref[0], boT_ref[0], coT_ref[0]], axis=0)
    cat = catT.astype(F32).T.astype(BF16)
    y = jnp.dot(cat, wo_ref[...], preferred_element_type=F32)
    x1 = _layer_norm(ALPHA * x_ref[...] + (1.0 + g_a) * y, ln1g_ref[...], ln1b_ref[...])
    h = (x1 * (1.0 + sc_m) + sh_m).astype(BF16)
    y2 = None
    fc = 1024
    for j in range(D_FF // fc):
        u = jnp.dot(h, w1_ref[:, fc * j:fc * (j + 1)], preferred_element_type=F32)
        u = jnp.maximum(u, 0.0)
        u = (u * u).astype(BF16)
        part = jnp.dot(u, w2_ref[fc * j:fc * (j + 1), :], preferred_element_type=F32)
        y2 = part if y2 is None else y2 + part
    o_ref[...] = _layer_norm(ALPHA * x1 + (1.0 + g_m) * y2, ln2g_ref[...], ln2b_ref[...])


def _mlp_call(x2d, mod, aoT, boT, coT, wo, ln1g, ln1b, w1, w2, ln2g, ln2b):
    tm = TOK_TILE
    T = x2d.shape[0]
    nj = SEQ // tm
    const2 = lambda i: (0, 0)
    bj3 = lambda i: (i // nj, 0, i % nj)
    one = pl.Buffered(1)
    in_specs = [
        pl.BlockSpec((tm, D_MODEL), lambda i: (i, 0)),
        pl.BlockSpec((1, 6, D_MODEL), lambda i: (i // nj, 0, 0)),
        pl.BlockSpec((1, 256, tm), bj3),
        pl.BlockSpec((1, 256, tm), bj3),
        pl.BlockSpec((1, 512, tm), bj3),
        pl.BlockSpec((D_MODEL, D_MODEL), const2, pipeline_mode=one),
        pl.BlockSpec((1, D_MODEL), const2),
        pl.BlockSpec((1, D_MODEL), const2),
        pl.BlockSpec((D_MODEL, D_FF), const2, pipeline_mode=one),
        pl.BlockSpec((D_FF, D_MODEL), const2, pipeline_mode=one),
        pl.BlockSpec((1, D_MODEL), const2),
        pl.BlockSpec((1, D_MODEL), const2),
    ]
    return pl.pallas_call(
        _mlp_kernel,
        out_shape=jax.ShapeDtypeStruct((T, D_MODEL), F32),
        grid=(T // tm,),
        in_specs=in_specs,
        out_specs=pl.BlockSpec((tm, D_MODEL), lambda i: (i, 0)),
        compiler_params=pltpu.CompilerParams(vmem_limit_bytes=VMEM_LIMIT),
        name="out_mlp",
    )(x2d, mod, aoT, boT, coT, wo, ln1g, ln1b, w1, w2, ln2g, ln2b)


def _rope_angles(pos, dim):
    inv = ROPE_THETA ** (-jnp.arange(0, dim, 2, dtype=F32) / dim)
    return pos.astype(F32)[:, None] * inv[None, :]


def _t5_bucket(rel):
    nb = REL_BUCKETS // 2
    max_exact = nb // 2
    sign = jnp.where(rel > 0, nb, 0)
    n = jnp.abs(rel)
    nf = jnp.maximum(n, 1).astype(F32)
    large = max_exact + (jnp.log(nf / max_exact) / math.log(REL_MAX_DIST / max_exact)
                         * (nb - max_exact)).astype(jnp.int32)
    large = jnp.minimum(large, nb - 1)
    return sign + jnp.where(n < max_exact, n, large)


def _dilated_bias_tiles(rel_bias):
    t = A_TILE
    span = (A_CENTER + 1) * t
    deltas = np.arange(-span + 1, span)
    mult = np.zeros(deltas.shape, np.int32)
    for (w, d) in A_PATTERNS:
        half = w // (2 * d)
        mult += ((deltas % d == 0) & (np.abs(deltas) <= half * d)).astype(np.int32)
    bucket = _t5_bucket(jnp.asarray(deltas, jnp.int32))
    logm = jnp.asarray(np.log(np.maximum(mult, 1)), F32)
    tab = jnp.where(jnp.asarray(mult > 0)[:, None],
                    (rel_bias[bucket] + logm[:, None]) * LOG2E, NEG_BIG)
    valid = jnp.asarray(mult > 0)[:, None]
    b_max = jnp.max(jnp.where(valid, tab, -jnp.inf), axis=0)
    b_min = jnp.min(jnp.where(valid, tab, jnp.inf), axis=0)
    x = np.arange(2 * t)
    d = np.where(x <= t - 1, -x, 2 * t - x)
    d[t] = 0
    idx = (np.arange(A_NCHUNK)[:, None] - A_CENTER) * t + d[None, :] + span - 1
    u = jnp.moveaxis(tab[jnp.asarray(idx, jnp.int32)], -1, 0)
    u = jnp.concatenate([u, jnp.full((A_HEADS, 1, 2 * t), NEG_BIG, F32)], axis=1)
    return _toeplitz_call(u.reshape(A_HEADS * (A_NCHUNK + 1), 1, 2 * t)), b_max, b_min


def _toeplitz_kernel(u_ref, o_ref):
    t = A_TILE
    for c in range(u_ref.shape[0]):
        rows = jnp.broadcast_to(u_ref[c], (t, 2 * t))
        o_ref[c] = pltpu.roll(rows, 0, 1, stride=1, stride_axis=0)[:, :t]


def _toeplitz_call(u):
    t = A_TILE
    n = u.shape[0]
    per = A_NCHUNK + 1
    tiles = pl.pallas_call(
        _toeplitz_kernel,
        out_shape=jax.ShapeDtypeStruct((n, t, t), F32),
        grid=(n // per,),
        in_specs=[pl.BlockSpec((per, 1, 2 * t), lambda i: (i, 0, 0))],
        out_specs=pl.BlockSpec((per, t, t), lambda i: (i, 0, 0)),
        name="bias_tiles",
    )(u)
    return tiles.reshape(A_HEADS, A_NCHUNK + 1, t, t)


def _rope_tables(gq, gk):
    t = jnp.arange(SEQ)
    ang_t = _rope_angles(t, B_ROPE)
    ang_row = _rope_angles(t // GRID_W, HEAD_DIM // 2)
    ang_col = _rope_angles(t % GRID_W, HEAD_DIM // 2)
    cos64 = jnp.concatenate([jnp.cos(ang_row)] * 2 + [jnp.cos(ang_col)] * 2, axis=1)
    sin_r, sin_c = jnp.sin(ang_row), jnp.sin(ang_col)
    sin64 = jnp.concatenate([-sin_r, sin_r, -sin_c, sin_c], axis=1)
    perm = np.concatenate([np.arange(16, 32), np.arange(0, 16), np.arange(48, 64), np.arange(32, 48)])

    def with_gain(g, scale):
        return (cos64 * g[None, :] * scale).T, (sin64 * g[perm][None, :] * scale).T

    qc, qs = with_gain(gq, HEAD_DIM ** -0.5 * LOG2E)
    kc, ks = with_gain(gk, 1.0)
    return jnp.concatenate([jnp.cos(ang_t).T, jnp.sin(ang_t).T, qc, qs, kc, ks], axis=0)


def kernel(x, c, w_ada, b_ada, w_in, mla_q_norm, mla_w_uq, mla_kv_norm, mla_w_ukv,
           gqa_q_norm, gqa_k_norm, rel_bias, w_o, ln1_g, ln1_b, w1, w2, ln2_g, ln2_b):
    B, S, D = x.shape
    c_pad = jnp.zeros((8, D), F32).at[:B].set(c)
    mod_all = _ada_call(c_pad, w_ada, b_ada)[:, :B].reshape(DEPTH, B, 6, D)
    bias, b_max, b_min = _dilated_bias_tiles(rel_bias)

    x2d = x.reshape(B * S, D)
    for l in range(DEPTH):
        winT = w_in[l].T.astype(BF16)
        wuqT = mla_w_uq[l].T
        wukv = mla_w_ukv[l].reshape(B_KV_RANK, B_HEADS, B_NOPE + B_V)
        wukT = wukv[:, :, :B_NOPE].reshape(B_KV_RANK, B_HEADS * B_NOPE).T
        wuvT = wukv[:, :, B_NOPE:].reshape(B_KV_RANK, B_HEADS * B_V).T
        tab = _rope_tables(gqa_q_norm[l], gqa_k_norm[l])
        aqT, ak, avT, bqT, bk, bvT, cqT, ck, cvT, qn, kn = _proj_call(
            x2d, mod_all[l], winT, wuqT, mla_q_norm[l][None, :], wukT, wuvT,
            mla_kv_norm[l][None, :], tab)
        a_scal = _dil_scalars(qn[:, QN_A:QN_A + A_HEADS], kn[:, KN_A:KN_A + A_HEADS], b_max, b_min)
        aoT = _dil_attn_call(a_scal, aqT, ak, avT, bias)
        cg = C_Q_HEADS // C_KV_HEADS
        b_scal = _bound_scalars(qn[:, :B_HEADS], kn[:, :B_HEADS], 1)
        c_scal = _bound_scalars(qn[:, B_HEADS:B_HEADS + C_Q_HEADS], kn[:, B_HEADS:B_HEADS + C_KV_HEADS], cg)
        boT = _full_attn_call(b_scal, bqT, bk, bvT, groups=1, tq=1024, bound_row=B_NOPE + B_ROPE,
                              name="mla_attn")
        coT = _full_attn_call(c_scal, cqT, ck, cvT, groups=cg, tq=256, bound_row=HEAD_DIM, name="gqa_attn")
        x2d = _mlp_call(x2d, mod_all[l], aoT, boT, coT, w_o[l].astype(BF16),
                        ln1_g[l][None, :], ln1_b[l][None, :], w1[l].astype(BF16), w2[l].astype(BF16),
                        ln2_g[l][None, :], ln2_b[l][None, :])
    return x2d.reshape(B, S, D)
```

```python
import functools
import math

import numpy as np
import jax
import jax.numpy as jnp
from jax import lax
from jax.experimental import pallas as pl
from jax.experimental.pallas import tpu as pltpu

D_MODEL = 1024
BATCH = 4
SEQ = 4096
DEPTH = 2
HEAD_DIM = 64
A_HEADS = 4
A_PATTERNS = ((128, 1), (512, 4), (2048, 16))
B_HEADS = 4
B_Q_RANK = 384
B_KV_RANK = 256
B_NOPE = 64
B_ROPE = 32
B_V = 64
C_Q_HEADS = 8
C_KV_HEADS = 2
GRID_W = 64
ROPE_THETA = 10000.0
REL_BUCKETS = 32
REL_MAX_DIST = 1024
D_FF = 4 * D_MODEL
EPS = 1e-6
NEG_BIG = -1e30
ALPHA = (2 * DEPTH) ** 0.25
A_W = A_HEADS * HEAD_DIM
IN_COLS = 2208
LOG2E = 1.4426950408889634

F32 = jnp.float32
BF16 = jnp.bfloat16

LANES = 128
VMEM_LIMIT = 56 * 1024 * 1024

OFF_AQ, OFF_AK, OFF_AV = 0, 256, 512
OFF_BCQ, OFF_BCKV, OFF_BKR = 768, 1152, 1408
OFF_CQ, OFF_CK, OFF_CV, OFF_END = 1440, 1952, 2080, 2208

TOK_TILE = 512
MLP_TILE = 1024
MLP_PARTS = 4
A_TILE = 256
A_NCHUNK = 9
A_CENTER = 4
KV_TILE = 512
ROW_BLK = 256
QN_A = 12
KN_A = 8
A_SCAL = 12
A_PV_LAG = 4
PV_LAG = 2
RING = 3
STRIP = 512
Q_MARGIN = 1.02
K_MARGIN = 1.01
BOUND_LIMIT = 50.0


def _exact_zero_of(v):
    bits = lax.bitcast_convert_type(v, jnp.uint32)
    return ((bits >> 16) >> 16).astype(F32).astype(BF16)


def _nt_dot(a, b):
    return lax.dot_general(a, b, (((1,), (1,)), ((), ())), preferred_element_type=F32)


def _ada_kernel(c_ref, w_ref, b_ref, o_ref):
    c = c_ref[...]
    ca = (c / (1.0 + jnp.exp(-c))).astype(BF16)
    w = w_ref[0].astype(BF16)
    o_ref[0] = jnp.dot(ca, w, preferred_element_type=F32) + b_ref[0]


def _ada_call(c_pad, w_ada, b_ada):
    tn = 1536
    n = 6 * D_MODEL
    return pl.pallas_call(
        _ada_kernel,
        out_shape=jax.ShapeDtypeStruct((DEPTH, 8, n), F32),
        grid=(DEPTH, n // tn),
        in_specs=[
            pl.BlockSpec((8, D_MODEL), lambda l, j: (0, 0)),
            pl.BlockSpec((1, D_MODEL, tn), lambda l, j: (l, 0, j)),
            pl.BlockSpec((1, 1, tn), lambda l, j: (l, 0, j)),
        ],
        out_specs=pl.BlockSpec((1, 8, tn), lambda l, j: (l, 0, j)),
        compiler_params=pltpu.CompilerParams(vmem_limit_bytes=VMEM_LIMIT),
        name="ada_mod",
    )(c_pad, w_ada, b_ada.reshape(DEPTH, 1, n))


def _proj_kernel(x_ref, mod_ref, winT_ref, wuqT_ref, gq_ref, wukT_ref, wuvT_ref, gkv_ref, tab_ref,
                 aqT_ref, ak_ref, avT_ref, bqT_ref, bk_ref, bvT_ref, cqT_ref, ck_ref, cvT_ref,
                 qn_ref, kn_ref):
    tm = x_ref.shape[0]
    sh = mod_ref[0, 0:1, :]
    sc = mod_ref[0, 1:2, :]
    h = (x_ref[...] * (1.0 + sc) + sh).astype(BF16)

    proj_t = _nt_dot(winT_ref[...], h)

    def seg(r0, r1):
        return proj_t[r0:r1]

    def first_row(rows, val):
        return jnp.where(lax.broadcasted_iota(jnp.int32, (rows, tm), 0) == 0, val, 0.0)

    def norm(xh):
        return jnp.sqrt(jnp.sum(xh * xh, axis=0, keepdims=True))

    one_row32 = first_row(32, 1.0)
    one_row64 = first_row(64, 1.0)
    qn_ref[0] = jnp.zeros(qn_ref.shape[1:], F32)
    kn_ref[0] = jnp.zeros(kn_ref.shape[1:], F32)

    aq = seg(OFF_AQ, OFF_AK) * (HEAD_DIM ** -0.5 * LOG2E)
    for hd in range(A_HEADS):
        qh = aq[64 * hd:64 * hd + 64]
        qb = Q_MARGIN * norm(qh)
        qn_ref[0, QN_A + hd:QN_A + hd + 1, :] = qb
        aqT_ref[0, 128 * hd:128 * hd + 64, :] = qh.astype(BF16)
        aqT_ref[0, 128 * hd + 64:128 * hd + 128, :] = first_row(64, -qb).astype(BF16)
    ak = seg(OFF_AK, OFF_AV)
    for hd in range(A_HEADS):
        kh = ak[64 * hd:64 * hd + 64]
        kn_ref[0, KN_A + hd:KN_A + hd + 1, :] = norm(kh)
        kt = jnp.concatenate([kh, one_row64], axis=0)
        ak_ref[0, hd] = kt.T.astype(BF16)
    av = seg(OFF_AV, OFF_BCQ).astype(BF16)
    for j in range(tm // A_TILE):
        avT_ref[0, j] = av[:, A_TILE * j:A_TILE * (j + 1)]

    def normrope(xh, c, s):
        r = lax.rsqrt(jnp.mean(xh * xh, axis=0, keepdims=True) + EPS)
        rot = jnp.concatenate([xh[16:32], xh[0:16], xh[48:64], xh[32:48]], axis=0)
        return (xh * c + rot * s) * r

    cqc = tab_ref[32:96, :]
    cqs = tab_ref[96:160, :]
    cq2 = seg(OFF_CQ, OFF_CK)
    for hd in range(C_Q_HEADS):
        qh = normrope(cq2[64 * hd:64 * hd + 64], cqc, cqs)
        qb = Q_MARGIN * norm(qh)
        qn_ref[0, B_HEADS + hd:B_HEADS + hd + 1, :] = qb
        cqT_ref[0, 128 * hd:128 * hd + 64, :] = qh.astype(BF16)
        cqT_ref[0, 128 * hd + 64:128 * hd + 128, :] = first_row(64, -qb).astype(BF16)
    ckc = tab_ref[160:224, :]
    cks = tab_ref[224:288, :]
    ck2 = seg(OFF_CK, OFF_CV)
    for hd in range(C_KV_HEADS):
        kh = normrope(ck2[64 * hd:64 * hd + 64], ckc, cks)
        kn_ref[0, B_HEADS + hd:B_HEADS + hd + 1, :] = norm(kh)
        kt = jnp.concatenate([kh, one_row64], axis=0)
        ck_ref[0, hd] = kt.T.astype(BF16)
    cvT_ref[0, 0] = seg(OFF_CV, OFF_END).astype(BF16)

    cos_t = tab_ref[0:16, :]
    sin_t = tab_ref[16:32, :]
    cq = seg(OFF_BCQ, OFF_BCKV)
    rq = lax.rsqrt(jnp.mean(cq * cq, axis=0, keepdims=True) + EPS)
    wq = (wuqT_ref[...] * gq_ref[...]).astype(BF16)
    uq = jnp.dot(wq, cq.astype(BF16), preferred_element_type=F32)
    uq = uq * (rq * ((B_NOPE + B_ROPE) ** -0.5 * LOG2E))
    for hd in range(B_HEADS):
        b0 = (B_NOPE + B_ROPE) * hd
        x1 = uq[b0 + 64:b0 + 80]
        x2 = uq[b0 + 80:b0 + 96]
        qb = Q_MARGIN * norm(uq[b0:b0 + 96])
        qn_ref[0, hd:hd + 1, :] = qb
        bqT_ref[0, 128 * hd:128 * hd + 64, :] = uq[b0:b0 + 64].astype(BF16)
        bqT_ref[0, 128 * hd + 64:128 * hd + 80, :] = (x1 * cos_t - x2 * sin_t).astype(BF16)
        bqT_ref[0, 128 * hd + 80:128 * hd + 96, :] = (x1 * sin_t + x2 * cos_t).astype(BF16)
        bqT_ref[0, 128 * hd + 96:128 * hd + 128, :] = first_row(32, -qb).astype(BF16)
    ckv = seg(OFF_BCKV, OFF_BKR)
    rkv = lax.rsqrt(jnp.mean(ckv * ckv, axis=0, keepdims=True) + EPS)
    ckv_b = ckv.astype(BF16)
    gkv = gkv_ref[...]
    kn = jnp.dot((wukT_ref[...] * gkv).astype(BF16), ckv_b, preferred_element_type=F32) * rkv
    vv = jnp.dot((wuvT_ref[...] * gkv).astype(BF16), ckv_b, preferred_element_type=F32) * rkv
    bvT_ref[0, 0] = vv.astype(BF16)
    kr = seg(OFF_BKR, OFF_CQ)
    kr1 = kr[0:16]
    kr2 = kr[16:32]
    kro = jnp.concatenate([kr1 * cos_t - kr2 * sin_t, kr1 * sin_t + kr2 * cos_t,
                           one_row32], axis=0)
    kr_sq = jnp.sum(kr * kr, axis=0, keepdims=True)
    for hd in range(B_HEADS):
        knh = kn[64 * hd:64 * hd + 64]
        kn_ref[0, hd:hd + 1, :] = jnp.sqrt(jnp.sum(knh * knh, axis=0, keepdims=True) + kr_sq)
        kt = jnp.concatenate([knh, kro], axis=0)
        bk_ref[0, hd] = kt.T.astype(BF16)


def _proj_call(x2d, mod, winT, wuqT, gq, wukT, wuvT, gkv, tab):
    tm = TOK_TILE
    T = x2d.shape[0]
    nj = SEQ // tm
    const2 = lambda i: (0, 0)
    bj3 = lambda i: (i // nj, 0, i % nj)
    bj4 = lambda i: (i // nj, 0, i % nj, 0)
    cj4 = lambda i: (i // nj, i % nj, 0, 0)
    out_shape = (
        jax.ShapeDtypeStruct((BATCH, 512, SEQ), BF16),
        jax.ShapeDtypeStruct((BATCH, A_HEADS, SEQ, 128), BF16),
        jax.ShapeDtypeStruct((BATCH, SEQ // A_TILE, 256, A_TILE), BF16),
        jax.ShapeDtypeStruct((BATCH, 512, SEQ), BF16),
        jax.ShapeDtypeStruct((BATCH, B_HEADS, SEQ, 128), BF16),
        jax.ShapeDtypeStruct((BATCH, SEQ // KV_TILE, 256, KV_TILE), BF16),
        jax.ShapeDtypeStruct((BATCH, 1024, SEQ), BF16),
        jax.ShapeDtypeStruct((BATCH, C_KV_HEADS, SEQ, 128), BF16),
        jax.ShapeDtypeStruct((BATCH, SEQ // KV_TILE, 128, KV_TILE), BF16),
        jax.ShapeDtypeStruct((BATCH, 16, SEQ), F32),
        jax.ShapeDtypeStruct((BATCH, 16, SEQ), F32),
    )
    out_specs = (
        pl.BlockSpec((1, 512, tm), bj3),
        pl.BlockSpec((1, A_HEADS, tm, 128), bj4),
        pl.BlockSpec((1, tm // A_TILE, 256, A_TILE), cj4),
        pl.BlockSpec((1, 512, tm), bj3),
        pl.BlockSpec((1, B_HEADS, tm, 128), bj4),
        pl.BlockSpec((1, tm // KV_TILE, 256, KV_TILE), cj4),
        pl.BlockSpec((1, 1024, tm), bj3),
        pl.BlockSpec((1, C_KV_HEADS, tm, 128), bj4),
        pl.BlockSpec((1, tm // KV_TILE, 128, KV_TILE), cj4),
        pl.BlockSpec((1, 16, tm), bj3),
        pl.BlockSpec((1, 16, tm), bj3),
    )
    in_specs = [
        pl.BlockSpec((tm, D_MODEL), lambda i: (i, 0)),
        pl.BlockSpec((1, 6, D_MODEL), lambda i: (i // nj, 0, 0)),
        pl.BlockSpec((IN_COLS, D_MODEL), const2),
        pl.BlockSpec((B_Q_RANK, B_Q_RANK), const2),
        pl.BlockSpec((1, B_Q_RANK), const2),
        pl.BlockSpec((256, B_KV_RANK), const2),
        pl.BlockSpec((256, B_KV_RANK), const2),
        pl.BlockSpec((1, B_KV_RANK), const2),
        pl.BlockSpec((288, tm), lambda i: (0, i % nj)),
    ]
    return pl.pallas_call(
        _proj_kernel,
        out_shape=out_shape,
        grid=(T // tm,),
        in_specs=in_specs,
        out_specs=out_specs,
        compiler_params=pltpu.CompilerParams(vmem_limit_bytes=VMEM_LIMIT),
        name="in_proj",
    )(x2d, mod, winT, wuqT, gq, wukT, wuvT, gkv, tab)


def _with_ones_rows(v):
    return jnp.concatenate([v, jnp.ones((16, v.shape[1]), BF16)], axis=0)


def _sublane_partial_sum(p):
    return functools.reduce(lambda a, b: a + b, [p[8 * i:8 * (i + 1)] for i in range(p.shape[0] // 8)])


def _attn_finish(o_ref, num, den, groups, tq):
    o = num / den
    for g in range(groups):
        o_ref[0, 64 * g:64 * g + 64, :] = o[:, tq * g:tq * (g + 1)].astype(o_ref.dtype)


def _attn_online(q_strips, k_ref, vT_ref, o_ref, acc_ref, s_ref, p_ref, *, groups, tq, tk):
    nstrip = len(q_strips)
    w = q_strips[0].shape[1]
    nchunk = SEQ // tk
    rb = ROW_BLK
    nb = tk // rb
    nunit = nchunk * nstrip
    acc_ref[...] = jnp.zeros(acc_ref.shape, F32)
    m = [jnp.full((1, w), NEG_BIG, F32) for _ in range(nstrip)]

    def qk_block(u, b):
        c, j = divmod(u, nstrip)
        k = k_ref[0, 0, c * tk + rb * b:c * tk + rb * (b + 1), :]
        return jnp.dot(k, q_strips[j], preferred_element_type=F32)

    def pv_block(u, b, p_blk):
        c = u // nstrip
        v_aug = _with_ones_rows(vT_ref[0, c, :, rb * b:rb * (b + 1)])
        return jnp.dot(v_aug, p_blk, preferred_element_type=F32)

    alphas = {}
    for t in range(nunit + 2):
        pv = None
        for b in range(nb):
            rows = slice(rb * b, rb * (b + 1))
            if t < nunit:
                s_ref[t % RING, rows, :] = qk_block(t, b)
            if t >= 2:
                part = pv_block(t - 2, b, p_ref[(t - 2) % RING, rows, :])
                pv = part if pv is None else pv + part
        if 1 <= t <= nunit:
            u = t - 1
            j = u % nstrip
            s_u = s_ref[u % RING]
            m_new = jnp.maximum(m[j], jnp.max(s_u, axis=0, keepdims=True))
            alphas[u] = jnp.exp2(m[j] - m_new)
            p_ref[u % RING] = jnp.exp2(s_u - m_new).astype(BF16)
            m[j] = m_new
        if t >= 2:
            u = t - 2
            sl = slice(w * (u % nstrip), w * (u % nstrip + 1))
            acc_ref[:, sl] = alphas.pop(u) * acc_ref[:, sl] + pv
    _attn_finish(o_ref, acc_ref[0:64, :], acc_ref[64:65, :], groups, tq)


def _attn_bounded(q_strips, k_ref, vT_ref, o_ref, *, groups, tq, tk):
    nstrip = len(q_strips)
    rb = ROW_BLK
    nkb = SEQ // rb
    units = [(kb, j) for kb in range(nkb) for j in range(nstrip)]
    acc = [None] * nstrip
    den = [None] * nstrip
    probs = {}

    def pv(u):
        kb, j = units[u]
        c, b = divmod(kb, tk // rb)
        part = jnp.dot(vT_ref[0, c, :, rb * b:rb * (b + 1)], probs.pop(u), preferred_element_type=F32)
        acc[j] = part if acc[j] is None else acc[j] + part

    for t in range(len(units)):
        kb, j = units[t]
        s = jnp.dot(k_ref[0, 0, rb * kb:rb * (kb + 1), :], q_strips[j], preferred_element_type=F32)
        p = jnp.exp2(s)
        part = _sublane_partial_sum(p)
        den[j] = part if den[j] is None else den[j] + part
        probs[t] = p.astype(BF16)
        if t >= PV_LAG:
            pv(t - PV_LAG)
    for u in range(len(units) - PV_LAG, len(units)):
        pv(u)
    den = [jnp.sum(d, axis=0, keepdims=True) for d in den]
    _attn_finish(o_ref, jnp.concatenate(acc, axis=1), jnp.concatenate(den, axis=1), groups, tq)


def _full_attn_kernel(scal_ref, qT_ref, k_ref, vT_ref, o_ref, acc_ref, s_ref, p_ref,
                      *, groups, tq, tk, bound_row):
    n = groups * tq
    w = min(n, STRIP)
    nstrip = n // w
    pair = 2 * (pl.program_id(0) * pl.num_programs(1) + pl.program_id(1))
    k_bound = scal_ref[pair]
    bounded_ok = scal_ref[pair + 1] > 0.5
    cols = [qT_ref[0, 128 * g:128 * (g + 1), :] for g in range(groups)]
    q_cat = cols[0] if groups == 1 else jnp.concatenate(cols, axis=1)
    q_lo = q_cat[0:bound_row]
    q_bound = q_cat[bound_row:bound_row + 16]
    q_hi = q_cat[bound_row + 16:]

    def strips(mid):
        q = jnp.concatenate([q_lo, mid, q_hi], axis=0)
        return [q[:, w * j:w * (j + 1)] for j in range(nstrip)]

    @pl.when(bounded_ok)
    def _():
        mid = (q_bound.astype(F32) * k_bound).astype(BF16)
        _attn_bounded(strips(mid), k_ref, vT_ref, o_ref, groups=groups, tq=tq, tk=tk)

    @pl.when(jnp.logical_not(bounded_ok))
    def _():
        _attn_online(strips(jnp.zeros_like(q_bound)), k_ref, vT_ref, o_ref, acc_ref, s_ref, p_ref,
                     groups=groups, tq=tq, tk=tk)


def _bound_scalars(qn, kn, groups):
    k_bound = K_MARGIN * jnp.max(kn, axis=-1)
    q_max = jnp.max(qn, axis=-1).reshape(BATCH, kn.shape[1], groups).max(-1)
    ok = (q_max * k_bound <= BOUND_LIMIT).astype(F32)
    return jnp.stack([k_bound, ok], axis=-1).reshape(-1)


def _full_attn_call(scal, qT, k, vT, *, groups, tq, bound_row, name):
    tk = KV_TILE
    hkv = k.shape[1]
    n = groups * tq
    kern = functools.partial(_full_attn_kernel, groups=groups, tq=tq, tk=tk, bound_row=bound_row)
    return pl.pallas_call(
        kern,
        out_shape=jax.ShapeDtypeStruct((BATCH, hkv * groups * 64, SEQ), BF16),
        grid=(BATCH, hkv, SEQ // tq),
        in_specs=[
            pl.BlockSpec(memory_space=pltpu.SMEM),
            pl.BlockSpec((1, groups * 128, tq), lambda b, g, i: (b, g, i)),
            pl.BlockSpec((1, 1, SEQ, 128), lambda b, g, i: (b, g, 0, 0)),
            pl.BlockSpec((1, SEQ // tk, 64, tk), lambda b, g, i: (b, 0, g, 0)),
        ],
        out_specs=pl.BlockSpec((1, groups * 64, tq), lambda b, g, i: (b, g, i)),
        scratch_shapes=[pltpu.VMEM((80, n), F32),
                        pltpu.VMEM((RING, tk, min(n, STRIP)), F32),
                        pltpu.VMEM((RING, tk, min(n, STRIP)), BF16)],
        compiler_params=pltpu.CompilerParams(vmem_limit_bytes=VMEM_LIMIT),
        name=name,
    )(scal, qT, k, vT)


def _dil_attn_kernel(scal_ref, qT_ref, k_ref, vT_ref, bias_ref, o_ref):
    t = A_TILE
    nq = SEQ // t
    i = pl.program_id(1)
    first = jnp.clip(i - A_CENTER, 0, nq - A_NCHUNK)
    base = A_SCAL * pl.program_id(0)
    bounded_ok = scal_ref[base] > 0.5
    top_row = lax.broadcasted_iota(jnp.int32, (16, t), 0) == 0

    def q_head(h, mid_fn):
        q = qT_ref[0, 128 * h:128 * (h + 1), :]
        return jnp.concatenate([q[0:HEAD_DIM], mid_fn(h, q[HEAD_DIM:HEAD_DIM + 16]), q[HEAD_DIM + 16:]], axis=0)

    def score_tile(h, r, q):
        kt = first + r
        d = kt - i + A_CENTER
        bi = jnp.where((d >= 0) & (d < A_NCHUNK), d, A_NCHUNK)
        k = k_ref[0, h, pl.ds(pl.multiple_of(kt * t, t), t), :]
        return jnp.dot(k, q, preferred_element_type=F32) + bias_ref[h, bi]

    def pv_tile(h, r, p, ones_rows):
        v = vT_ref[0, first + r, 64 * h:64 * (h + 1), :]
        return jnp.dot(_with_ones_rows(v) if ones_rows else v, p, preferred_element_type=F32)

    def store(h, num, den):
        o_ref[0, 64 * h:64 * (h + 1), :] = (num / den).astype(o_ref.dtype)

    @pl.when(bounded_ok)
    def _():
        def mid(h, q_bound):
            shift = q_bound.astype(F32) * scal_ref[base + 1 + h]
            return (shift - jnp.where(top_row, scal_ref[base + 1 + A_HEADS + h], 0.0)).astype(BF16)

        qs = [q_head(h, mid) for h in range(A_HEADS)]
        units = [(h, r) for r in range(A_NCHUNK) for h in range(A_HEADS)]
        acc = [None] * A_HEADS
        den = [None] * A_HEADS
        probs = {}

        def pv(u):
            h, r = units[u]
            part = pv_tile(h, r, probs.pop(u), False)
            acc[h] = part if acc[h] is None else acc[h] + part

        for u, (h, r) in enumerate(units):
            p = jnp.exp2(score_tile(h, r, qs[h]))
            part = _sublane_partial_sum(p)
            den[h] = part if den[h] is None else den[h] + part
            probs[u] = p.astype(BF16)
            if u >= A_PV_LAG:
                pv(u - A_PV_LAG)
        for u in range(len(units) - A_PV_LAG, len(units)):
            pv(u)
        for h in range(A_HEADS):
            store(h, acc[h], jnp.sum(den[h], axis=0, keepdims=True))

    @pl.when(jnp.logical_not(bounded_ok))
    def _():
        def scores(h):
            q = q_head(h, lambda _, q_bound: jnp.zeros_like(q_bound))
            return [score_tile(h, r, q) for r in range(A_NCHUNK)]

        def finish(h, s_list):
            m = jnp.max(functools.reduce(jnp.maximum, s_list), axis=0, keepdims=True)
            acc = None
            for r in range(A_NCHUNK):
                part = pv_tile(h, r, jnp.exp2(s_list[r] - m).astype(BF16), True)
                acc = part if acc is None else acc + part
            store(h, acc[0:64], acc[64:65])

        s_next = scores(0)
        for h in range(A_HEADS):
            s_cur = s_next
            if h + 1 < A_HEADS:
                s_next = scores(h + 1)
            finish(h, s_cur)


def _dil_scalars(qn, kn, b_max, b_min):
    k_bound = K_MARGIN * jnp.max(kn, axis=-1)
    q_max = jnp.max(qn, axis=-1)
    spread = 2.0 * q_max * k_bound + (b_max - b_min)[None, :]
    ok = jnp.all(spread <= 2.0 * BOUND_LIMIT, axis=-1, keepdims=True).astype(F32)
    pad = jnp.zeros((BATCH, A_SCAL - 1 - 2 * A_HEADS), F32)
    return jnp.concatenate([ok, k_bound, jnp.broadcast_to(b_max[None, :], (BATCH, A_HEADS)), pad], axis=1).reshape(-1)


def _dil_attn_call(scal, qT, k, vT, bias):
    t = A_TILE
    return pl.pallas_call(
        _dil_attn_kernel,
        out_shape=jax.ShapeDtypeStruct((BATCH, A_HEADS * 64, SEQ), BF16),
        grid=(BATCH, SEQ // t),
        in_specs=[
            pl.BlockSpec(memory_space=pltpu.SMEM),
            pl.BlockSpec((1, A_HEADS * 128, t), lambda b, i: (b, 0, i)),
            pl.BlockSpec((1, A_HEADS, SEQ, 128), lambda b, i: (b, 0, 0, 0)),
            pl.BlockSpec((1, SEQ // t, A_HEADS * 64, t), lambda b, i: (b, 0, 0, 0)),
            pl.BlockSpec((A_HEADS, A_NCHUNK + 1, t, t), lambda b, i: (0, 0, 0, 0),
                         pipeline_mode=pl.Buffered(1)),
        ],
        out_specs=pl.BlockSpec((1, A_HEADS * 64, t), lambda b, i: (b, 0, i)),
        compiler_params=pltpu.CompilerParams(vmem_limit_bytes=VMEM_LIMIT),
        name="dilated_attn",
    )(scal, qT, k, vT, bias)


def _layer_norm(z, g, b):
    mu = jnp.mean(z, axis=-1, keepdims=True)
    d = z - mu
    var = jnp.mean(d * d, axis=-1, keepdims=True)
    return d * lax.rsqrt(var + EPS) * g + b


def _mlp_kernel(x_ref, mod_ref, aoT_ref, boT_ref, coT_ref, wo_ref, ln1g_ref, ln1b_ref,
                w1_ref, w2_ref, ln2g_ref, ln2b_ref, o_ref):
    g_a = mod_ref[0, 2:3, :]
    sh_m = mod_ref[0, 3:4, :]
    sc_m = mod_ref[0, 4:5, :]
    g_m = mod_ref[0, 5:6, :]
    tm = x_ref.shape[0]
    hr = tm // MLP_PARTS
    fc = 1024
    nfc = D_FF // fc

    def out_proj(part):
        cols = slice(hr * part, hr * (part + 1))
        catT = jnp.concatenate([aoT_ref[0, :, cols], boT_ref[0, :, cols], coT_ref[0, :, cols]], axis=0)
        cat = catT.astype(F32).T.astype(BF16)
        return jnp.dot(cat, wo_ref[...], preferred_element_type=F32)

    ys = [out_proj(part) for part in range(MLP_PARTS)]
    prev_out = None
    for part in range(MLP_PARTS):
        rows = slice(hr * part, hr * (part + 1))
        x1 = _layer_norm(ALPHA * x_ref[rows, :] + (1.0 + g_a) * ys[part], ln1g_ref[...], ln1b_ref[...])
        h = (x1 * (1.0 + sc_m) + sh_m).astype(BF16)
        y2 = None
        for j in range(nfc):
            u = jnp.dot(h, w1_ref[:, fc * j:fc * (j + 1)], preferred_element_type=F32)
            u = jnp.maximum(u, 0.0)
            u = (u * u).astype(BF16)
            if prev_out is not None and j == nfc - 1:
                folded = _sublane_partial_sum(prev_out)
                folded = functools.reduce(lambda a, b: a + b,
                                          [folded[:, LANES * i:LANES * (i + 1)] for i in range(D_MODEL // LANES)])
                zero = _exact_zero_of(jnp.concatenate([folded, folded], axis=0))
                top = jnp.concatenate([u[0:16, 0:LANES] + zero, u[0:16, LANES:]], axis=1)
                u = jnp.concatenate([top, u[16:]], axis=0)
            term = jnp.dot(u, w2_ref[fc * j:fc * (j + 1), :], preferred_element_type=F32)
            y2 = term if y2 is None else y2 + term
        prev_out = _layer_norm(ALPHA * x1 + (1.0 + g_m) * y2, ln2g_ref[...], ln2b_ref[...])
        o_ref[rows, :] = prev_out


def _mlp_call(x2d, mod, aoT, boT, coT, wo, ln1g, ln1b, w1, w2, ln2g, ln2b):
    tm = MLP_TILE
    T = x2d.shape[0]
    nj = SEQ // tm
    const2 = lambda i: (0, 0)
    bj3 = lambda i: (i // nj, 0, i % nj)
    one = pl.Buffered(1)
    in_specs = [
        pl.BlockSpec((tm, D_MODEL), lambda i: (i, 0)),
        pl.BlockSpec((1, 6, D_MODEL), lambda i: (i // nj, 0, 0)),
        pl.BlockSpec((1, 256, tm), bj3),
        pl.BlockSpec((1, 256, tm), bj3),
        pl.BlockSpec((1, 512, tm), bj3),
        pl.BlockSpec((D_MODEL, D_MODEL), const2, pipeline_mode=one),
        pl.BlockSpec((1, D_MODEL), const2),
        pl.BlockSpec((1, D_MODEL), const2),
        pl.BlockSpec((D_MODEL, D_FF), const2, pipeline_mode=one),
        pl.BlockSpec((D_FF, D_MODEL), const2, pipeline_mode=one),
        pl.BlockSpec((1, D_MODEL), const2),
        pl.BlockSpec((1, D_MODEL), const2),
    ]
    return pl.pallas_call(
        _mlp_kernel,
        out_shape=jax.ShapeDtypeStruct((T, D_MODEL), F32),
        grid=(T // tm,),
        in_specs=in_specs,
        out_specs=pl.BlockSpec((tm, D_MODEL), lambda i: (i, 0)),
        compiler_params=pltpu.CompilerParams(vmem_limit_bytes=VMEM_LIMIT),
        name="out_mlp",
    )(x2d, mod, aoT, boT, coT, wo, ln1g, ln1b, w1, w2, ln2g, ln2b)


def _rope_angles(pos, dim):
    inv = ROPE_THETA ** (-jnp.arange(0, dim, 2, dtype=F32) / dim)
    return pos.astype(F32)[:, None] * inv[None, :]


def _t5_bucket(rel):
    nb = REL_BUCKETS // 2
    max_exact = nb // 2
    sign = jnp.where(rel > 0, nb, 0)
    n = jnp.abs(rel)
    nf = jnp.maximum(n, 1).astype(F32)
    large = max_exact + (jnp.log(nf / max_exact) / math.log(REL_MAX_DIST / max_exact)
                         * (nb - max_exact)).astype(jnp.int32)
    large = jnp.minimum(large, nb - 1)
    return sign + jnp.where(n < max_exact, n, large)


def _dilated_bias_tiles(rel_bias):
    t = A_TILE
    span = (A_CENTER + 1) * t
    deltas = np.arange(-span + 1, span)
    mult = np.zeros(deltas.shape, np.int32)
    for (w, d) in A_PATTERNS:
        half = w // (2 * d)
        mult += ((deltas % d == 0) & (np.abs(deltas) <= half * d)).astype(np.int32)
    bucket = _t5_bucket(jnp.asarray(deltas, jnp.int32))
    logm = jnp.asarray(np.log(np.maximum(mult, 1)), F32)
    tab = jnp.where(jnp.asarray(mult > 0)[:, None],
                    (rel_bias[bucket] + logm[:, None]) * LOG2E, NEG_BIG)
    valid = jnp.asarray(mult > 0)[:, None]
    b_max = jnp.max(jnp.where(valid, tab, -jnp.inf), axis=0)
    b_min = jnp.min(jnp.where(valid, tab, jnp.inf), axis=0)
    x = np.arange(2 * t)
    d = np.where(x <= t - 1, -x, 2 * t - x)
    d[t] = 0
    idx = (np.arange(A_NCHUNK)[:, None] - A_CENTER) * t + d[None, :] + span - 1
    u = jnp.moveaxis(tab[jnp.asarray(idx, jnp.int32)], -1, 0)
    u = jnp.concatenate([u, jnp.full((A_HEADS, 1, 2 * t), NEG_BIG, F32)], axis=1)
    return _toeplitz_call(u.reshape(A_HEADS * (A_NCHUNK + 1), 1, 2 * t)), b_max, b_min


def _toeplitz_kernel(u_ref, o_ref):
    t = A_TILE
    for c in range(u_ref.shape[0]):
        rows = jnp.broadcast_to(u_ref[c], (t, 2 * t))
        o_ref[c] = pltpu.roll(rows, 0, 1, stride=1, stride_axis=0)[:, :t]


def _toeplitz_call(u):
    t = A_TILE
    n = u.shape[0]
    per = A_NCHUNK + 1
    tiles = pl.pallas_call(
        _toeplitz_kernel,
        out_shape=jax.ShapeDtypeStruct((n, t, t), F32),
        grid=(n // per,),
        in_specs=[pl.BlockSpec((per, 1, 2 * t), lambda i: (i, 0, 0))],
        out_specs=pl.BlockSpec((per, t, t), lambda i: (i, 0, 0)),
        name="bias_tiles",
    )(u)
    return tiles.reshape(A_HEADS, A_NCHUNK + 1, t, t)


def _rope_tables(gq, gk):
    t = jnp.arange(SEQ)
    ang_t = _rope_angles(t, B_ROPE)
    ang_row = _rope_angles(t // GRID_W, HEAD_DIM // 2)
    ang_col = _rope_angles(t % GRID_W, HEAD_DIM // 2)
    cos64 = jnp.concatenate([jnp.cos(ang_row)] * 2 + [jnp.cos(ang_col)] * 2, axis=1)
    sin_r, sin_c = jnp.sin(ang_row), jnp.sin(ang_col)
    sin64 = jnp.concatenate([-sin_r, sin_r, -sin_c, sin_c], axis=1)
    perm = np.concatenate([np.arange(16, 32), np.arange(0, 16), np.arange(48, 64), np.arange(32, 48)])

    def with_gain(g, scale):
        return (cos64 * g[None, :] * scale).T, (sin64 * g[perm][None, :] * scale).T

    qc, qs = with_gain(gq, HEAD_DIM ** -0.5 * LOG2E)
    kc, ks = with_gain(gk, 1.0)
    return jnp.concatenate([jnp.cos(ang_t).T, jnp.sin(ang_t).T, qc, qs, kc, ks], axis=0)


def kernel(x, c, w_ada, b_ada, w_in, mla_q_norm, mla_w_uq, mla_kv_norm, mla_w_ukv,
           gqa_q_norm, gqa_k_norm, rel_bias, w_o, ln1_g, ln1_b, w1, w2, ln2_g, ln2_b):
    B, S, D = x.shape
    c_pad = jnp.zeros((8, D), F32).at[:B].set(c)
    mod_all = _ada_call(c_pad, w_ada, b_ada)[:, :B].reshape(DEPTH, B, 6, D)
    bias, b_max, b_min = _dilated_bias_tiles(rel_bias)

    x2d = x.reshape(B * S, D)
    for l in range(DEPTH):
        winT = w_in[l].T.astype(BF16)
        wuqT = mla_w_uq[l].T
        wukv = mla_w_ukv[l].reshape(B_KV_RANK, B_HEADS, B_NOPE + B_V)
        wukT = wukv[:, :, :B_NOPE].reshape(B_KV_RANK, B_HEADS * B_NOPE).T
        wuvT = wukv[:, :, B_NOPE:].reshape(B_KV_RANK, B_HEADS * B_V).T
        tab = _rope_tables(gqa_q_norm[l], gqa_k_norm[l])
        aqT, ak, avT, bqT, bk, bvT, cqT, ck, cvT, qn, kn = _proj_call(
            x2d, mod_all[l], winT, wuqT, mla_q_norm[l][None, :], wukT, wuvT,
            mla_kv_norm[l][None, :], tab)
        a_scal = _dil_scalars(qn[:, QN_A:QN_A + A_HEADS], kn[:, KN_A:KN_A + A_HEADS], b_max, b_min)
        aoT = _dil_attn_call(a_scal, aqT, ak, avT, bias)
        cg = C_Q_HEADS // C_KV_HEADS
        b_scal = _bound_scalars(qn[:, :B_HEADS], kn[:, :B_HEADS], 1)
        c_scal = _bound_scalars(qn[:, B_HEADS:B_HEADS + C_Q_HEADS], kn[:, B_HEADS:B_HEADS + C_KV_HEADS], cg)
        boT = _full_attn_call(b_scal, bqT, bk, bvT, groups=1, tq=1024, bound_row=B_NOPE + B_ROPE,
                              name="mla_attn")
        coT = _full_attn_call(c_scal, cqT, ck, cvT, groups=cg, tq=256, bound_row=HEAD_DIM, name="gqa_attn")
        x2d = _mlp_call(x2d, mod_all[l], aoT, boT, coT, w_o[l].astype(BF16),
                        ln1_g[l][None, :], ln1_b[l][None, :], w1[l].astype(BF16), w2[l].astype(BF16),
                        ln2_g[l][None, :], ln2_b[l][None, :])
    return x2d.reshape(B, S, D)
```

```python
import functools
import math

import numpy as np
import jax
import jax.numpy as jnp
from jax import lax
from jax.experimental import pallas as pl
from jax.experimental.pallas import tpu as pltpu

D_MODEL = 1024
BATCH = 4
SEQ = 4096
DEPTH = 2
HEAD_DIM = 64
A_HEADS = 4
A_PATTERNS = ((128, 1), (512, 4), (2048, 16))
B_HEADS = 4
B_Q_RANK = 384
B_KV_RANK = 256
B_NOPE = 64
B_ROPE = 32
B_V = 64
C_Q_HEADS = 8
C_KV_HEADS = 2
GRID_W = 64
ROPE_THETA = 10000.0
REL_BUCKETS = 32
REL_MAX_DIST = 1024
D_FF = 4 * D_MODEL
EPS = 1e-6
NEG_BIG = -1e30
ALPHA = (2 * DEPTH) ** 0.25
A_W = A_HEADS * HEAD_DIM
IN_COLS = 2208
LOG2E = 1.4426950408889634

F32 = jnp.float32
BF16 = jnp.bfloat16

LANES = 128
VMEM_LIMIT = 56 * 1024 * 1024

OFF_AQ, OFF_AK, OFF_AV = 0, 256, 512
OFF_BCQ, OFF_BCKV, OFF_BKR = 768, 1152, 1408
OFF_CQ, OFF_CK, OFF_CV, OFF_END = 1440, 1952, 2080, 2208

TOK_TILE = 512
MLP_TILE = 1024
MLP_PARTS = 4
A_TILE = 256
A_NCHUNK = 9
A_CENTER = 4
KV_TILE = 512
ROW_BLK = 256
QN_A = 12
KN_A = 8
A_SCAL = 12
A_PV_LAG = 4
PV_LAG = 2
RING = 3
STRIP = 512
TAB_CBOUND = 288
TAB_ROWS = 304
TINY = 1e-30
Q_MARGIN = 1.02
K_MARGIN = 1.01
BOUND_LIMIT = 50.0


def _exact_zero_of(v):
    bits = lax.bitcast_convert_type(v, jnp.uint32)
    return ((bits >> 16) >> 16).astype(F32).astype(BF16)


def _nt_dot(a, b):
    return lax.dot_general(a, b, (((1,), (1,)), ((), ())), preferred_element_type=F32)


def _ada_kernel(c_ref, w_ref, b_ref, o_ref):
    c = c_ref[...]
    ca = (c / (1.0 + jnp.exp(-c))).astype(BF16)
    w = w_ref[0].astype(BF16)
    o_ref[0] = jnp.dot(ca, w, preferred_element_type=F32) + b_ref[0]


def _ada_call(c_pad, w_ada, b_ada):
    tn = 1536
    n = 6 * D_MODEL
    return pl.pallas_call(
        _ada_kernel,
        out_shape=jax.ShapeDtypeStruct((DEPTH, 8, n), F32),
        grid=(DEPTH, n // tn),
        in_specs=[
            pl.BlockSpec((8, D_MODEL), lambda l, j: (0, 0)),
            pl.BlockSpec((1, D_MODEL, tn), lambda l, j: (l, 0, j)),
            pl.BlockSpec((1, 1, tn), lambda l, j: (l, 0, j)),
        ],
        out_specs=pl.BlockSpec((1, 8, tn), lambda l, j: (l, 0, j)),
        compiler_params=pltpu.CompilerParams(vmem_limit_bytes=VMEM_LIMIT),
        name="ada_mod",
    )(c_pad, w_ada, b_ada.reshape(DEPTH, 1, n))


def _proj_kernel(x_ref, mod_ref, winT_ref, wuqT_ref, gq_ref, wukT_ref, wuvT_ref, gkv_ref, tab_ref,
                 aqT_ref, ak_ref, avT_ref, bqT_ref, bk_ref, bvT_ref, cqT_ref, ck_ref, cvT_ref,
                 qn_ref, kn_ref):
    tm = x_ref.shape[0]
    sh = mod_ref[0, 0:1, :]
    sc = mod_ref[0, 1:2, :]
    h = (x_ref[...] * (1.0 + sc) + sh).astype(BF16)

    proj_t = _nt_dot(winT_ref[0], h)

    def seg(r0, r1):
        return proj_t[r0:r1]

    def first_row(rows, val):
        return jnp.where(lax.broadcasted_iota(jnp.int32, (rows, tm), 0) == 0, val, 0.0)

    def norm(xh):
        n2 = jnp.sum(xh * xh, axis=0, keepdims=True)
        return n2 * lax.rsqrt(n2 + TINY)

    one_row32 = first_row(32, 1.0)
    one_row64 = first_row(64, 1.0)
    qn_ref[0] = jnp.zeros(qn_ref.shape[1:], F32)
    kn_ref[0] = jnp.zeros(kn_ref.shape[1:], F32)

    aq = seg(OFF_AQ, OFF_AK) * (HEAD_DIM ** -0.5 * LOG2E)
    for hd in range(A_HEADS):
        qh = aq[64 * hd:64 * hd + 64]
        qb = Q_MARGIN * norm(qh)
        qn_ref[0, QN_A + hd:QN_A + hd + 1, :] = qb
        aqT_ref[0, 128 * hd:128 * hd + 64, :] = qh.astype(BF16)
        aqT_ref[0, 128 * hd + 64:128 * hd + 128, :] = first_row(64, -qb).astype(BF16)
    ak = seg(OFF_AK, OFF_AV)
    for hd in range(A_HEADS):
        kh = ak[64 * hd:64 * hd + 64]
        kn_ref[0, KN_A + hd:KN_A + hd + 1, :] = norm(kh)
        kt = jnp.concatenate([kh, one_row64], axis=0)
        ak_ref[0, hd] = kt.T.astype(BF16)
    av = seg(OFF_AV, OFF_BCQ).astype(BF16)
    for j in range(tm // A_TILE):
        avT_ref[0, j] = av[:, A_TILE * j:A_TILE * (j + 1)]

    def normrope(xh, c, s):
        r = lax.rsqrt(jnp.mean(xh * xh, axis=0, keepdims=True) + EPS)
        rot = jnp.concatenate([xh[16:32], xh[0:16], xh[48:64], xh[32:48]], axis=0)
        return (xh * c + rot * s) * r

    cqc = tab_ref[32:96, :]
    cqs = tab_ref[96:160, :]
    c_bound_rows = tab_ref[TAB_CBOUND:TAB_CBOUND + 16, :].astype(BF16)
    cq2 = seg(OFF_CQ, OFF_CK)
    for hd in range(C_Q_HEADS):
        qh = normrope(cq2[64 * hd:64 * hd + 64], cqc, cqs)
        cqT_ref[0, 128 * hd:128 * hd + 64, :] = qh.astype(BF16)
        cqT_ref[0, 128 * hd + 64:128 * hd + 80, :] = c_bound_rows
        cqT_ref[0, 128 * hd + 80:128 * hd + 128, :] = jnp.zeros((48, tm), BF16)
    ckc = tab_ref[160:224, :]
    cks = tab_ref[224:288, :]
    ck2 = seg(OFF_CK, OFF_CV)
    for hd in range(C_KV_HEADS):
        kh = normrope(ck2[64 * hd:64 * hd + 64], ckc, cks)
        kt = jnp.concatenate([kh, one_row64], axis=0)
        ck_ref[0, hd] = kt.T.astype(BF16)
    cvT_ref[0, 0] = seg(OFF_CV, OFF_END).astype(BF16)

    cos_t = tab_ref[0:16, :]
    sin_t = tab_ref[16:32, :]
    cq = seg(OFF_BCQ, OFF_BCKV)
    rq = lax.rsqrt(jnp.mean(cq * cq, axis=0, keepdims=True) + EPS)
    wq = (wuqT_ref[...] * gq_ref[...]).astype(BF16)
    uq = jnp.dot(wq, cq.astype(BF16), preferred_element_type=F32)
    uq = uq * (rq * ((B_NOPE + B_ROPE) ** -0.5 * LOG2E))
    for hd in range(B_HEADS):
        b0 = (B_NOPE + B_ROPE) * hd
        x1 = uq[b0 + 64:b0 + 80]
        x2 = uq[b0 + 80:b0 + 96]
        qb = Q_MARGIN * norm(uq[b0:b0 + 96])
        qn_ref[0, hd:hd + 1, :] = qb
        bqT_ref[0, 128 * hd:128 * hd + 64, :] = uq[b0:b0 + 64].astype(BF16)
        bqT_ref[0, 128 * hd + 64:128 * hd + 80, :] = (x1 * cos_t - x2 * sin_t).astype(BF16)
        bqT_ref[0, 128 * hd + 80:128 * hd + 96, :] = (x1 * sin_t + x2 * cos_t).astype(BF16)
        bqT_ref[0, 128 * hd + 96:128 * hd + 128, :] = first_row(32, -qb).astype(BF16)
    ckv = seg(OFF_BCKV, OFF_BKR)
    rkv = lax.rsqrt(jnp.mean(ckv * ckv, axis=0, keepdims=True) + EPS)
    ckv_b = ckv.astype(BF16)
    gkv = gkv_ref[...]
    kn = jnp.dot((wukT_ref[...] * gkv).astype(BF16), ckv_b, preferred_element_type=F32) * rkv
    vv = jnp.dot((wuvT_ref[...] * gkv).astype(BF16), ckv_b, preferred_element_type=F32) * rkv
    bvT_ref[0, 0] = vv.astype(BF16)
    kr = seg(OFF_BKR, OFF_CQ)
    kr1 = kr[0:16]
    kr2 = kr[16:32]
    kro = jnp.concatenate([kr1 * cos_t - kr2 * sin_t, kr1 * sin_t + kr2 * cos_t,
                           one_row32], axis=0)
    kr_sq = jnp.sum(kr * kr, axis=0, keepdims=True)
    for hd in range(B_HEADS):
        knh = kn[64 * hd:64 * hd + 64]
        k_sq = jnp.sum(knh * knh, axis=0, keepdims=True) + kr_sq
        kn_ref[0, hd:hd + 1, :] = k_sq * lax.rsqrt(k_sq + TINY)
        kt = jnp.concatenate([knh, kro], axis=0)
        bk_ref[0, hd] = kt.T.astype(BF16)


def _proj_call(x2d, mod, winT_all, layer, wuqT, gq, wukT, wuvT, gkv, tab):
    tm = TOK_TILE
    T = x2d.shape[0]
    nj = SEQ // tm
    const2 = lambda i: (0, 0)
    bj3 = lambda i: (i // nj, 0, i % nj)
    bj4 = lambda i: (i // nj, 0, i % nj, 0)
    cj4 = lambda i: (i // nj, i % nj, 0, 0)
    out_shape = (
        jax.ShapeDtypeStruct((BATCH, 512, SEQ), BF16),
        jax.ShapeDtypeStruct((BATCH, A_HEADS, SEQ, 128), BF16),
        jax.ShapeDtypeStruct((BATCH, SEQ // A_TILE, 256, A_TILE), BF16),
        jax.ShapeDtypeStruct((BATCH, 512, SEQ), BF16),
        jax.ShapeDtypeStruct((BATCH, B_HEADS, SEQ, 128), BF16),
        jax.ShapeDtypeStruct((BATCH, SEQ // KV_TILE, 256, KV_TILE), BF16),
        jax.ShapeDtypeStruct((BATCH, 1024, SEQ), BF16),
        jax.ShapeDtypeStruct((BATCH, C_KV_HEADS, SEQ, 128), BF16),
        jax.ShapeDtypeStruct((BATCH, SEQ // KV_TILE, 128, KV_TILE), BF16),
        jax.ShapeDtypeStruct((BATCH, 16, SEQ), F32),
        jax.ShapeDtypeStruct((BATCH, 16, SEQ), F32),
    )
    out_specs = (
        pl.BlockSpec((1, 512, tm), bj3),
        pl.BlockSpec((1, A_HEADS, tm, 128), bj4),
        pl.BlockSpec((1, tm // A_TILE, 256, A_TILE), cj4),
        pl.BlockSpec((1, 512, tm), bj3),
        pl.BlockSpec((1, B_HEADS, tm, 128), bj4),
        pl.BlockSpec((1, tm // KV_TILE, 256, KV_TILE), cj4),
        pl.BlockSpec((1, 1024, tm), bj3),
        pl.BlockSpec((1, C_KV_HEADS, tm, 128), bj4),
        pl.BlockSpec((1, tm // KV_TILE, 128, KV_TILE), cj4),
        pl.BlockSpec((1, 16, tm), bj3),
        pl.BlockSpec((1, 16, tm), bj3),
    )
    in_specs = [
        pl.BlockSpec((tm, D_MODEL), lambda i: (i, 0)),
        pl.BlockSpec((1, 6, D_MODEL), lambda i: (i // nj, 0, 0)),
        pl.BlockSpec((1, IN_COLS, D_MODEL), lambda i: (layer, 0, 0)),
        pl.BlockSpec((B_Q_RANK, B_Q_RANK), const2),
        pl.BlockSpec((1, B_Q_RANK), const2),
        pl.BlockSpec((256, B_KV_RANK), const2),
        pl.BlockSpec((256, B_KV_RANK), const2),
        pl.BlockSpec((1, B_KV_RANK), const2),
        pl.BlockSpec((TAB_ROWS, tm), lambda i: (0, i % nj)),
    ]
    return pl.pallas_call(
        _proj_kernel,
        out_shape=out_shape,
        grid=(T // tm,),
        in_specs=in_specs,
        out_specs=out_specs,
        compiler_params=pltpu.CompilerParams(vmem_limit_bytes=VMEM_LIMIT),
        name="in_proj",
    )(x2d, mod, winT_all, wuqT, gq, wukT, wuvT, gkv, tab)


def _with_ones_rows(v):
    return jnp.concatenate([v, jnp.ones((16, v.shape[1]), BF16)], axis=0)


def _sublane_partial_sum(p):
    return functools.reduce(lambda a, b: a + b, [p[8 * i:8 * (i + 1)] for i in range(p.shape[0] // 8)])


def _attn_finish(o_ref, num, den, groups, tq):
    o = num / den
    for g in range(groups):
        o_ref[0, 64 * g:64 * g + 64, :] = o[:, tq * g:tq * (g + 1)].astype(o_ref.dtype)


def _attn_online(q_strips, k_ref, vT_ref, o_ref, acc_ref, s_ref, p_ref, *, groups, tq, tk):
    nstrip = len(q_strips)
    w = q_strips[0].shape[1]
    nchunk = SEQ // tk
    rb = ROW_BLK
    nb = tk // rb
    nunit = nchunk * nstrip
    acc_ref[...] = jnp.zeros(acc_ref.shape, F32)
    m = [jnp.full((1, w), NEG_BIG, F32) for _ in range(nstrip)]

    def qk_block(u, b):
        c, j = divmod(u, nstrip)
        k = k_ref[0, 0, c * tk + rb * b:c * tk + rb * (b + 1), :]
        return jnp.dot(k, q_strips[j], preferred_element_type=F32)

    def pv_block(u, b, p_blk):
        c = u // nstrip
        v_aug = _with_ones_rows(vT_ref[0, c, :, rb * b:rb * (b + 1)])
        return jnp.dot(v_aug, p_blk, preferred_element_type=F32)

    alphas = {}
    for t in range(nunit + 2):
        pv = None
        for b in range(nb):
            rows = slice(rb * b, rb * (b + 1))
            if t < nunit:
                s_ref[t % RING, rows, :] = qk_block(t, b)
            if t >= 2:
                part = pv_block(t - 2, b, p_ref[(t - 2) % RING, rows, :])
                pv = part if pv is None else pv + part
        if 1 <= t <= nunit:
            u = t - 1
            j = u % nstrip
            s_u = s_ref[u % RING]
            m_new = jnp.maximum(m[j], jnp.max(s_u, axis=0, keepdims=True))
            alphas[u] = jnp.exp2(m[j] - m_new)
            p_ref[u % RING] = jnp.exp2(s_u - m_new).astype(BF16)
            m[j] = m_new
        if t >= 2:
            u = t - 2
            sl = slice(w * (u % nstrip), w * (u % nstrip + 1))
            acc_ref[:, sl] = alphas.pop(u) * acc_ref[:, sl] + pv
    _attn_finish(o_ref, acc_ref[0:64, :], acc_ref[64:65, :], groups, tq)


def _attn_bounded(q_strips, k_ref, vT_ref, o_ref, *, groups, tq, tk):
    nstrip = len(q_strips)
    rb = ROW_BLK
    nkb = SEQ // rb
    units = [(kb, j) for kb in range(nkb) for j in range(nstrip)]
    acc = [None] * nstrip
    den = [None] * nstrip
    probs = {}

    def pv(u):
        kb, j = units[u]
        c, b = divmod(kb, tk // rb)
        part = jnp.dot(vT_ref[0, c, :, rb * b:rb * (b + 1)], probs.pop(u), preferred_element_type=F32)
        acc[j] = part if acc[j] is None else acc[j] + part

    for t in range(len(units)):
        kb, j = units[t]
        s = jnp.dot(k_ref[0, 0, rb * kb:rb * (kb + 1), :], q_strips[j], preferred_element_type=F32)
        p = jnp.exp2(s)
        part = _sublane_partial_sum(p)
        den[j] = part if den[j] is None else den[j] + part
        probs[t] = p.astype(BF16)
        if t >= PV_LAG:
            pv(t - PV_LAG)
    for u in range(len(units) - PV_LAG, len(units)):
        pv(u)
    den = [jnp.sum(d, axis=0, keepdims=True) for d in den]
    _attn_finish(o_ref, jnp.concatenate(acc, axis=1), jnp.concatenate(den, axis=1), groups, tq)


def _full_attn_kernel(scal_ref, qT_ref, k_ref, vT_ref, o_ref, acc_ref, s_ref, p_ref,
                      *, groups, tq, tk, bound_row):
    n = groups * tq
    w = min(n, STRIP)
    nstrip = n // w
    pair = 2 * (pl.program_id(0) * pl.num_programs(1) + pl.program_id(1))
    k_bound = scal_ref[pair]
    bounded_ok = scal_ref[pair + 1] > 0.5
    cols = [qT_ref[0, 128 * g:128 * (g + 1), :] for g in range(groups)]
    q_cat = cols[0] if groups == 1 else jnp.concatenate(cols, axis=1)
    q_lo = q_cat[0:bound_row]
    q_bound = q_cat[bound_row:bound_row + 16]
    q_hi = q_cat[bound_row + 16:]

    def strips(mid):
        q = jnp.concatenate([q_lo, mid, q_hi], axis=0)
        return [q[:, w * j:w * (j + 1)] for j in range(nstrip)]

    @pl.when(bounded_ok)
    def _():
        mid = (q_bound.astype(F32) * k_bound).astype(BF16)
        _attn_bounded(strips(mid), k_ref, vT_ref, o_ref, groups=groups, tq=tq, tk=tk)

    @pl.when(jnp.logical_not(bounded_ok))
    def _():
        _attn_online(strips(jnp.zeros_like(q_bound)), k_ref, vT_ref, o_ref, acc_ref, s_ref, p_ref,
                     groups=groups, tq=tq, tk=tk)


def _bound_scalars(qn, kn, groups):
    k_bound = K_MARGIN * jnp.max(kn, axis=-1)
    q_max = jnp.max(qn, axis=-1).reshape(BATCH, kn.shape[1], groups).max(-1)
    ok = (q_max * k_bound <= BOUND_LIMIT).astype(F32)
    return jnp.stack([k_bound, ok], axis=-1).reshape(-1)


def _full_attn_call(scal, qT, k, vT, *, groups, tq, bound_row, name):
    tk = KV_TILE
    hkv = k.shape[1]
    n = groups * tq
    kern = functools.partial(_full_attn_kernel, groups=groups, tq=tq, tk=tk, bound_row=bound_row)
    return pl.pallas_call(
        kern,
        out_shape=jax.ShapeDtypeStruct((BATCH, hkv * groups * 64, SEQ), BF16),
        grid=(BATCH, hkv, SEQ // tq),
        in_specs=[
            pl.BlockSpec(memory_space=pltpu.SMEM),
            pl.BlockSpec((1, groups * 128, tq), lambda b, g, i: (b, g, i)),
            pl.BlockSpec((1, 1, SEQ, 128), lambda b, g, i: (b, g, 0, 0)),
            pl.BlockSpec((1, SEQ // tk, 64, tk), lambda b, g, i: (b, 0, g, 0)),
        ],
        out_specs=pl.BlockSpec((1, groups * 64, tq), lambda b, g, i: (b, g, i)),
        scratch_shapes=[pltpu.VMEM((80, n), F32),
                        pltpu.VMEM((RING, tk, min(n, STRIP)), F32),
                        pltpu.VMEM((RING, tk, min(n, STRIP)), BF16)],
        compiler_params=pltpu.CompilerParams(vmem_limit_bytes=VMEM_LIMIT),
        name=name,
    )(scal, qT, k, vT)


def _dil_attn_kernel(scal_ref, qT_ref, k_ref, vT_ref, bias_ref, o_ref):
    t = A_TILE
    nq = SEQ // t
    i = pl.program_id(1)
    first = jnp.clip(i - A_CENTER, 0, nq - A_NCHUNK)
    base = A_SCAL * pl.program_id(0)
    bounded_ok = scal_ref[base] > 0.5
    top_row = lax.broadcasted_iota(jnp.int32, (16, t), 0) == 0

    def q_head(h, mid_fn):
        q = qT_ref[0, 128 * h:128 * (h + 1), :]
        return jnp.concatenate([q[0:HEAD_DIM], mid_fn(h, q[HEAD_DIM:HEAD_DIM + 16]), q[HEAD_DIM + 16:]], axis=0)

    def score_tile(h, r, q):
        kt = first + r
        d = kt - i + A_CENTER
        bi = jnp.where((d >= 0) & (d < A_NCHUNK), d, A_NCHUNK)
        k = k_ref[0, h, pl.ds(pl.multiple_of(kt * t, t), t), :]
        return jnp.dot(k, q, preferred_element_type=F32) + bias_ref[h, bi]

    def pv_tile(h, r, p, ones_rows):
        v = vT_ref[0, first + r, 64 * h:64 * (h + 1), :]
        return jnp.dot(_with_ones_rows(v) if ones_rows else v, p, preferred_element_type=F32)

    def store(h, num, den):
        o_ref[0, 64 * h:64 * (h + 1), :] = (num / den).astype(o_ref.dtype)

    @pl.when(bounded_ok)
    def _():
        def mid(h, q_bound):
            shift = q_bound.astype(F32) * scal_ref[base + 1 + h]
            return (shift - jnp.where(top_row, scal_ref[base + 1 + A_HEADS + h], 0.0)).astype(BF16)

        qs = [q_head(h, mid) for h in range(A_HEADS)]
        units = [(h, r) for r in range(A_NCHUNK) for h in range(A_HEADS)]
        acc = [None] * A_HEADS
        den = [None] * A_HEADS
        probs = {}

        def pv(u):
            h, r = units[u]
            part = pv_tile(h, r, probs.pop(u), False)
            acc[h] = part if acc[h] is None else acc[h] + part

        for u, (h, r) in enumerate(units):
            p = jnp.exp2(score_tile(h, r, qs[h]))
            part = _sublane_partial_sum(p)
            den[h] = part if den[h] is None else den[h] + part
            probs[u] = p.astype(BF16)
            if u >= A_PV_LAG:
                pv(u - A_PV_LAG)
        for u in range(len(units) - A_PV_LAG, len(units)):
            pv(u)
        for h in range(A_HEADS):
            store(h, acc[h], jnp.sum(den[h], axis=0, keepdims=True))

    @pl.when(jnp.logical_not(bounded_ok))
    def _():
        def scores(h):
            q = q_head(h, lambda _, q_bound: jnp.zeros_like(q_bound))
            return [score_tile(h, r, q) for r in range(A_NCHUNK)]

        def finish(h, s_list):
            m = jnp.max(functools.reduce(jnp.maximum, s_list), axis=0, keepdims=True)
            acc = None
            for r in range(A_NCHUNK):
                part = pv_tile(h, r, jnp.exp2(s_list[r] - m).astype(BF16), True)
                acc = part if acc is None else acc + part
            store(h, acc[0:64], acc[64:65])

        s_next = scores(0)
        for h in range(A_HEADS):
            s_cur = s_next
            if h + 1 < A_HEADS:
                s_next = scores(h + 1)
            finish(h, s_cur)


def _dil_scalars(qn, kn, b_max, b_min):
    k_bound = K_MARGIN * jnp.max(kn, axis=-1)
    q_max = jnp.max(qn, axis=-1)
    spread = 2.0 * q_max * k_bound + (b_max - b_min)[None, :]
    ok = jnp.all(spread <= 2.0 * BOUND_LIMIT, axis=-1, keepdims=True).astype(F32)
    pad = jnp.zeros((BATCH, A_SCAL - 1 - 2 * A_HEADS), F32)
    return jnp.concatenate([ok, k_bound, jnp.broadcast_to(b_max[None, :], (BATCH, A_HEADS)), pad], axis=1).reshape(-1)


def _dil_attn_call(scal, qT, k, vT, bias):
    t = A_TILE
    return pl.pallas_call(
        _dil_attn_kernel,
        out_shape=jax.ShapeDtypeStruct((BATCH, A_HEADS * 64, SEQ), BF16),
        grid=(BATCH, SEQ // t),
        in_specs=[
            pl.BlockSpec(memory_space=pltpu.SMEM),
            pl.BlockSpec((1, A_HEADS * 128, t), lambda b, i: (b, 0, i)),
            pl.BlockSpec((1, A_HEADS, SEQ, 128), lambda b, i: (b, 0, 0, 0)),
            pl.BlockSpec((1, SEQ // t, A_HEADS * 64, t), lambda b, i: (b, 0, 0, 0)),
            pl.BlockSpec((A_HEADS, A_NCHUNK + 1, t, t), lambda b, i: (0, 0, 0, 0),
                         pipeline_mode=pl.Buffered(1)),
        ],
        out_specs=pl.BlockSpec((1, A_HEADS * 64, t), lambda b, i: (b, 0, i)),
        compiler_params=pltpu.CompilerParams(vmem_limit_bytes=VMEM_LIMIT),
        name="dilated_attn",
    )(scal, qT, k, vT, bias)


def _layer_norm(z, g, b):
    mu = jnp.mean(z, axis=-1, keepdims=True)
    d = z - mu
    var = jnp.mean(d * d, axis=-1, keepdims=True)
    return d * lax.rsqrt(var + EPS) * g + b


def _mlp_kernel(x_ref, mod_ref, aoT_ref, boT_ref, coT_ref, wo_ref, ln1g_ref, ln1b_ref,
                w1_ref, w2_ref, ln2g_ref, ln2b_ref, o_ref):
    g_a = mod_ref[0, 2:3, :]
    sh_m = mod_ref[0, 3:4, :]
    sc_m = mod_ref[0, 4:5, :]
    g_m = mod_ref[0, 5:6, :]
    tm = x_ref.shape[0]
    hr = tm // MLP_PARTS
    fc = 1024
    nfc = D_FF // fc

    def out_proj(part):
        cols = slice(hr * part, hr * (part + 1))
        catT = jnp.concatenate([aoT_ref[0, :, cols], boT_ref[0, :, cols], coT_ref[0, :, cols]], axis=0)
        cat = catT.astype(F32).T.astype(BF16)
        return jnp.dot(cat, wo_ref[0], preferred_element_type=F32)

    ys = [out_proj(part) for part in range(MLP_PARTS)]
    prev_out = None
    for part in range(MLP_PARTS):
        rows = slice(hr * part, hr * (part + 1))
        x1 = _layer_norm(ALPHA * x_ref[rows, :] + (1.0 + g_a) * ys[part], ln1g_ref[...], ln1b_ref[...])
        h = (x1 * (1.0 + sc_m) + sh_m).astype(BF16)
        y2 = None
        for j in range(nfc):
            u = jnp.dot(h, w1_ref[0, :, fc * j:fc * (j + 1)], preferred_element_type=F32)
            u = jnp.maximum(u, 0.0)
            u = (u * u).astype(BF16)
            if prev_out is not None and j == nfc - 1:
                folded = _sublane_partial_sum(prev_out)
                folded = functools.reduce(lambda a, b: a + b,
                                          [folded[:, LANES * i:LANES * (i + 1)] for i in range(D_MODEL // LANES)])
                zero = _exact_zero_of(jnp.concatenate([folded, folded], axis=0))
                top = jnp.concatenate([u[0:16, 0:LANES] + zero, u[0:16, LANES:]], axis=1)
                u = jnp.concatenate([top, u[16:]], axis=0)
            term = jnp.dot(u, w2_ref[0, fc * j:fc * (j + 1), :], preferred_element_type=F32)
            y2 = term if y2 is None else y2 + term
        prev_out = _layer_norm(ALPHA * x1 + (1.0 + g_m) * y2, ln2g_ref[...], ln2b_ref[...])
        o_ref[rows, :] = prev_out


def _mlp_call(x2d, mod, aoT, boT, coT, layer, wo_all, ln1g, ln1b, w1_all, w2_all, ln2g, ln2b):
    tm = MLP_TILE
    T = x2d.shape[0]
    nj = SEQ // tm
    const2 = lambda i: (0, 0)
    bj3 = lambda i: (i // nj, 0, i % nj)
    one = pl.Buffered(1)
    in_specs = [
        pl.BlockSpec((tm, D_MODEL), lambda i: (i, 0)),
        pl.BlockSpec((1, 6, D_MODEL), lambda i: (i // nj, 0, 0)),
        pl.BlockSpec((1, 256, tm), bj3),
        pl.BlockSpec((1, 256, tm), bj3),
        pl.BlockSpec((1, 512, tm), bj3),
        pl.BlockSpec((1, D_MODEL, D_MODEL), lambda i: (layer, 0, 0), pipeline_mode=one),
        pl.BlockSpec((1, D_MODEL), const2),
        pl.BlockSpec((1, D_MODEL), const2),
        pl.BlockSpec((1, D_MODEL, D_FF), lambda i: (layer, 0, 0), pipeline_mode=one),
        pl.BlockSpec((1, D_FF, D_MODEL), lambda i: (layer, 0, 0), pipeline_mode=one),
        pl.BlockSpec((1, D_MODEL), const2),
        pl.BlockSpec((1, D_MODEL), const2),
    ]
    return pl.pallas_call(
        _mlp_kernel,
        out_shape=jax.ShapeDtypeStruct((T, D_MODEL), F32),
        grid=(T // tm,),
        in_specs=in_specs,
        out_specs=pl.BlockSpec((tm, D_MODEL), lambda i: (i, 0)),
        compiler_params=pltpu.CompilerParams(vmem_limit_bytes=VMEM_LIMIT),
        name="out_mlp",
    )(x2d, mod, aoT, boT, coT, wo_all, ln1g, ln1b, w1_all, w2_all, ln2g, ln2b)


def _rope_angles(pos, dim):
    inv = ROPE_THETA ** (-jnp.arange(0, dim, 2, dtype=F32) / dim)
    return pos.astype(F32)[:, None] * inv[None, :]


def _t5_bucket(rel):
    nb = REL_BUCKETS // 2
    max_exact = nb // 2
    sign = jnp.where(rel > 0, nb, 0)
    n = jnp.abs(rel)
    nf = jnp.maximum(n, 1).astype(F32)
    large = max_exact + (jnp.log(nf / max_exact) / math.log(REL_MAX_DIST / max_exact)
                         * (nb - max_exact)).astype(jnp.int32)
    large = jnp.minimum(large, nb - 1)
    return sign + jnp.where(n < max_exact, n, large)


def _dilated_bias_tiles(rel_bias):
    t = A_TILE
    span = (A_CENTER + 1) * t
    deltas = np.arange(-span + 1, span)
    mult = np.zeros(deltas.shape, np.int32)
    for (w, d) in A_PATTERNS:
        half = w // (2 * d)
        mult += ((deltas % d == 0) & (np.abs(deltas) <= half * d)).astype(np.int32)
    bucket = _t5_bucket(jnp.asarray(deltas, jnp.int32))
    logm = jnp.asarray(np.log(np.maximum(mult, 1)), F32)
    tab = jnp.where(jnp.asarray(mult > 0)[:, None],
                    (rel_bias[bucket] + logm[:, None]) * LOG2E, NEG_BIG)
    valid = jnp.asarray(mult > 0)[:, None]
    b_max = jnp.max(jnp.where(valid, tab, -jnp.inf), axis=0)
    b_min = jnp.min(jnp.where(valid, tab, jnp.inf), axis=0)
    x = np.arange(2 * t)
    d = np.where(x <= t - 1, -x, 2 * t - x)
    d[t] = 0
    idx = (np.arange(A_NCHUNK)[:, None] - A_CENTER) * t + d[None, :] + span - 1
    u = jnp.moveaxis(tab[jnp.asarray(idx, jnp.int32)], -1, 0)
    u = jnp.concatenate([u, jnp.full((A_HEADS, 1, 2 * t), NEG_BIG, F32)], axis=1)
    return _toeplitz_call(u.reshape(A_HEADS * (A_NCHUNK + 1), 1, 2 * t)), b_max, b_min


def _toeplitz_kernel(u_ref, o_ref):
    t = A_TILE
    for c in range(u_ref.shape[0]):
        rows = jnp.broadcast_to(u_ref[c], (t, 2 * t))
        o_ref[c] = pltpu.roll(rows, 0, 1, stride=1, stride_axis=0)[:, :t]


def _toeplitz_call(u):
    t = A_TILE
    n = u.shape[0]
    per = A_NCHUNK + 1
    tiles = pl.pallas_call(
        _toeplitz_kernel,
        out_shape=jax.ShapeDtypeStruct((n, t, t), F32),
        grid=(n // per,),
        in_specs=[pl.BlockSpec((per, 1, 2 * t), lambda i: (i, 0, 0))],
        out_specs=pl.BlockSpec((per, t, t), lambda i: (i, 0, 0)),
        name="bias_tiles",
    )(u)
    return tiles.reshape(A_HEADS, A_NCHUNK + 1, t, t)


def _rope_tables(gq, gk):
    t = jnp.arange(SEQ)
    ang_t = _rope_angles(t, B_ROPE)
    ang_row = _rope_angles(t // GRID_W, HEAD_DIM // 2)
    ang_col = _rope_angles(t % GRID_W, HEAD_DIM // 2)
    cos64 = jnp.concatenate([jnp.cos(ang_row)] * 2 + [jnp.cos(ang_col)] * 2, axis=1)
    sin_r, sin_c = jnp.sin(ang_row), jnp.sin(ang_col)
    sin64 = jnp.concatenate([-sin_r, sin_r, -sin_c, sin_c], axis=1)
    perm = np.concatenate([np.arange(16, 32), np.arange(0, 16), np.arange(48, 64), np.arange(32, 48)])

    def with_gain(g, scale):
        return (cos64 * g[None, :] * scale).T, (sin64 * g[perm][None, :] * scale).T

    qc, qs = with_gain(gq, HEAD_DIM ** -0.5 * LOG2E)
    kc, ks = with_gain(gk, 1.0)
    q_bound, _ = _gqa_bounds(gq, gk)
    bound_rows = jnp.zeros((16, SEQ), F32).at[0].set(-q_bound)
    return jnp.concatenate([jnp.cos(ang_t).T, jnp.sin(ang_t).T, qc, qs, kc, ks, bound_rows], axis=0)


def _gqa_bounds(gq, gk):
    root = math.sqrt(HEAD_DIM)
    q_bound = Q_MARGIN * root * (HEAD_DIM ** -0.5 * LOG2E) * jnp.max(jnp.abs(gq))
    k_bound = root * jnp.max(jnp.abs(gk))
    return q_bound, k_bound


def kernel(x, c, w_ada, b_ada, w_in, mla_q_norm, mla_w_uq, mla_kv_norm, mla_w_ukv,
           gqa_q_norm, gqa_k_norm, rel_bias, w_o, ln1_g, ln1_b, w1, w2, ln2_g, ln2_b):
    B, S, D = x.shape
    c_pad = jnp.zeros((8, D), F32).at[:B].set(c)
    mod_all = _ada_call(c_pad, w_ada, b_ada)[:, :B].reshape(DEPTH, B, 6, D)
    bias, b_max, b_min = _dilated_bias_tiles(rel_bias)

    winT_all = jnp.swapaxes(w_in, 1, 2).astype(BF16)
    wo_all, w1_all, w2_all = w_o.astype(BF16), w1.astype(BF16), w2.astype(BF16)
    x2d = x.reshape(B * S, D)
    for l in range(DEPTH):
        wuqT = mla_w_uq[l].T
        wukv = mla_w_ukv[l].reshape(B_KV_RANK, B_HEADS, B_NOPE + B_V)
        wukT = wukv[:, :, :B_NOPE].reshape(B_KV_RANK, B_HEADS * B_NOPE).T
        wuvT = wukv[:, :, B_NOPE:].reshape(B_KV_RANK, B_HEADS * B_V).T
        tab = _rope_tables(gqa_q_norm[l], gqa_k_norm[l])
        aqT, ak, avT, bqT, bk, bvT, cqT, ck, cvT, qn, kn = _proj_call(
            x2d, mod_all[l], winT_all, l, wuqT, mla_q_norm[l][None, :], wukT, wuvT,
            mla_kv_norm[l][None, :], tab)
        a_scal = _dil_scalars(qn[:, QN_A:QN_A + A_HEADS], kn[:, KN_A:KN_A + A_HEADS], b_max, b_min)
        aoT = _dil_attn_call(a_scal, aqT, ak, avT, bias)
        cg = C_Q_HEADS // C_KV_HEADS
        b_scal = _bound_scalars(qn[:, :B_HEADS], kn[:, :B_HEADS], 1)
        cq_bound, ck_bound = _gqa_bounds(gqa_q_norm[l], gqa_k_norm[l])
        c_scal = _bound_scalars(jnp.full((B, C_Q_HEADS, 1), cq_bound), jnp.full((B, C_KV_HEADS, 1), ck_bound), cg)
        boT = _full_attn_call(b_scal, bqT, bk, bvT, groups=1, tq=1024, bound_row=B_NOPE + B_ROPE,
                              name="mla_attn")
        coT = _full_attn_call(c_scal, cqT, ck, cvT, groups=cg, tq=256, bound_row=HEAD_DIM, name="gqa_attn")
        x2d = _mlp_call(x2d, mod_all[l], aoT, boT, coT, l, wo_all,
                        ln1_g[l][None, :], ln1_b[l][None, :], w1_all, w2_all,
                        ln2_g[l][None, :], ln2_b[l][None, :])
    return x2d.reshape(B, S, D)
```

```python
import functools
import math

import numpy as np
import jax
import jax.numpy as jnp
from jax import lax
from jax.experimental import pallas as pl
from jax.experimental.pallas import tpu as pltpu

D_MODEL = 1024
BATCH = 4
SEQ = 4096
DEPTH = 2
HEAD_DIM = 64
A_HEADS = 4
A_PATTERNS = ((128, 1), (512, 4), (2048, 16))
B_HEADS = 4
B_Q_RANK = 384
B_KV_RANK = 256
B_NOPE = 64
B_ROPE = 32
B_V = 64
C_Q_HEADS = 8
C_KV_HEADS = 2
GRID_W = 64
ROPE_THETA = 10000.0
REL_BUCKETS = 32
REL_MAX_DIST = 1024
D_FF = 4 * D_MODEL
EPS = 1e-6
NEG_BIG = -1e30
ALPHA = (2 * DEPTH) ** 0.25
A_W = A_HEADS * HEAD_DIM
IN_COLS = 2208
LOG2E = 1.4426950408889634

F32 = jnp.float32
BF16 = jnp.bfloat16

LANES = 128
VMEM_LIMIT = 56 * 1024 * 1024

OFF_AQ, OFF_AK, OFF_AV = 0, 256, 512
OFF_BCQ, OFF_BCKV, OFF_BKR = 768, 1152, 1408
OFF_CQ, OFF_CK, OFF_CV, OFF_END = 1440, 1952, 2080, 2208

TOK_TILE = 512
MLP_TILE = 1024
MLP_PARTS = 4
A_TILE = 256
A_NCHUNK = 9
A_CENTER = 4
KV_TILE = 512
ROW_BLK = 256
QN_A = 12
KN_A = 8
A_SCAL = 12
A_PV_LAG = 4
PV_LAG = 2
RING = 3
STRIP = 512
TAB_CBOUND = 288
TAB_ROWS = 304
TINY = 1e-30
Q_MARGIN = 1.02
K_MARGIN = 1.01
BOUND_LIMIT = 50.0


def _exact_zero_of(v):
    bits = lax.bitcast_convert_type(v, jnp.uint32)
    return ((bits >> 16) >> 16).astype(F32).astype(BF16)


def _nt_dot(a, b):
    return lax.dot_general(a, b, (((1,), (1,)), ((), ())), preferred_element_type=F32)


def _ada_kernel(c_ref, w_ref, b_ref, o_ref):
    c = c_ref[...]
    ca = (c / (1.0 + jnp.exp(-c))).astype(BF16)
    w = w_ref[0].astype(BF16)
    o_ref[0] = jnp.dot(ca, w, preferred_element_type=F32) + b_ref[0]


def _ada_call(c_pad, w_ada, b_ada):
    tn = 1536
    n = 6 * D_MODEL
    return pl.pallas_call(
        _ada_kernel,
        out_shape=jax.ShapeDtypeStruct((DEPTH, 8, n), F32),
        grid=(DEPTH, n // tn),
        in_specs=[
            pl.BlockSpec((8, D_MODEL), lambda l, j: (0, 0)),
            pl.BlockSpec((1, D_MODEL, tn), lambda l, j: (l, 0, j)),
            pl.BlockSpec((1, 1, tn), lambda l, j: (l, 0, j)),
        ],
        out_specs=pl.BlockSpec((1, 8, tn), lambda l, j: (l, 0, j)),
        compiler_params=pltpu.CompilerParams(vmem_limit_bytes=VMEM_LIMIT),
        name="ada_mod",
    )(c_pad, w_ada, b_ada.reshape(DEPTH, 1, n))


def _proj_kernel(x_ref, mod_ref, winT_ref, wuqT_ref, gq_ref, wukT_ref, wuvT_ref, gkv_ref, tab_ref,
                 aqT_ref, ak_ref, avT_ref, bqT_ref, bk_ref, bvT_ref, cqT_ref, ck_ref, cvT_ref,
                 qn_ref, kn_ref):
    tm = x_ref.shape[0]
    sh = mod_ref[0, 0:1, :]
    sc = mod_ref[0, 1:2, :]
    h = (x_ref[...] * (1.0 + sc) + sh).astype(BF16)

    proj_t = _nt_dot(winT_ref[0], h)

    def seg(r0, r1):
        return proj_t[r0:r1]

    def first_row(rows, val):
        return jnp.where(lax.broadcasted_iota(jnp.int32, (rows, tm), 0) == 0, val, 0.0)

    def norm(xh):
        n2 = _sum_sq_rows(xh)
        return n2 * lax.rsqrt(n2 + TINY)

    one_row32 = first_row(32, 1.0)
    one_row64 = first_row(64, 1.0)
    qn_ref[0] = jnp.zeros(qn_ref.shape[1:], F32)
    kn_ref[0] = jnp.zeros(kn_ref.shape[1:], F32)

    aq = seg(OFF_AQ, OFF_AK) * (HEAD_DIM ** -0.5 * LOG2E)
    for hd in range(A_HEADS):
        qh = aq[64 * hd:64 * hd + 64]
        qb = Q_MARGIN * norm(qh)
        qn_ref[0, QN_A + hd:QN_A + hd + 1, :] = qb
        aqT_ref[0, 128 * hd:128 * hd + 64, :] = qh.astype(BF16)
        aqT_ref[0, 128 * hd + 64:128 * hd + 128, :] = first_row(64, -qb).astype(BF16)
    ak = seg(OFF_AK, OFF_AV)
    for hd in range(A_HEADS):
        kh = ak[64 * hd:64 * hd + 64]
        kn_ref[0, KN_A + hd:KN_A + hd + 1, :] = norm(kh)
        kt = jnp.concatenate([kh, one_row64], axis=0)
        ak_ref[0, hd] = kt.T.astype(BF16)
    av = seg(OFF_AV, OFF_BCQ).astype(BF16)
    for j in range(tm // A_TILE):
        avT_ref[0, j] = av[:, A_TILE * j:A_TILE * (j + 1)]

    def normrope(xh, c, s):
        r = lax.rsqrt(_sum_sq_rows(xh) * (1.0 / xh.shape[0]) + EPS)
        rot = jnp.concatenate([xh[16:32], xh[0:16], xh[48:64], xh[32:48]], axis=0)
        return (xh * c + rot * s) * r

    cqc = tab_ref[32:96, :]
    cqs = tab_ref[96:160, :]
    c_bound_rows = tab_ref[TAB_CBOUND:TAB_CBOUND + 16, :].astype(BF16)
    cq2 = seg(OFF_CQ, OFF_CK)
    for hd in range(C_Q_HEADS):
        qh = normrope(cq2[64 * hd:64 * hd + 64], cqc, cqs)
        cqT_ref[0, 128 * hd:128 * hd + 64, :] = qh.astype(BF16)
        cqT_ref[0, 128 * hd + 64:128 * hd + 80, :] = c_bound_rows
        cqT_ref[0, 128 * hd + 80:128 * hd + 128, :] = jnp.zeros((48, tm), BF16)
    ckc = tab_ref[160:224, :]
    cks = tab_ref[224:288, :]
    ck2 = seg(OFF_CK, OFF_CV)
    for hd in range(C_KV_HEADS):
        kh = normrope(ck2[64 * hd:64 * hd + 64], ckc, cks)
        kt = jnp.concatenate([kh, one_row64], axis=0)
        ck_ref[0, hd] = kt.T.astype(BF16)
    cvT_ref[0, 0] = seg(OFF_CV, OFF_END).astype(BF16)

    cos_t = tab_ref[0:16, :]
    sin_t = tab_ref[16:32, :]
    cq = seg(OFF_BCQ, OFF_BCKV)
    rq = lax.rsqrt(_sum_sq_rows(cq) * (1.0 / B_Q_RANK) + EPS)
    wq = (wuqT_ref[...] * gq_ref[...]).astype(BF16)
    uq = jnp.dot(wq, cq.astype(BF16), preferred_element_type=F32)
    uq = uq * (rq * ((B_NOPE + B_ROPE) ** -0.5 * LOG2E))
    for hd in range(B_HEADS):
        b0 = (B_NOPE + B_ROPE) * hd
        x1 = uq[b0 + 64:b0 + 80]
        x2 = uq[b0 + 80:b0 + 96]
        qb = Q_MARGIN * norm(uq[b0:b0 + 96])
        qn_ref[0, hd:hd + 1, :] = qb
        bqT_ref[0, 128 * hd:128 * hd + 64, :] = uq[b0:b0 + 64].astype(BF16)
        bqT_ref[0, 128 * hd + 64:128 * hd + 80, :] = (x1 * cos_t - x2 * sin_t).astype(BF16)
        bqT_ref[0, 128 * hd + 80:128 * hd + 96, :] = (x1 * sin_t + x2 * cos_t).astype(BF16)
        bqT_ref[0, 128 * hd + 96:128 * hd + 128, :] = first_row(32, -qb).astype(BF16)
    ckv = seg(OFF_BCKV, OFF_BKR)
    rkv = lax.rsqrt(_sum_sq_rows(ckv) * (1.0 / B_KV_RANK) + EPS)
    ckv_b = ckv.astype(BF16)
    gkv = gkv_ref[...]
    kn = jnp.dot((wukT_ref[...] * gkv).astype(BF16), ckv_b, preferred_element_type=F32) * rkv
    vv = jnp.dot((wuvT_ref[...] * gkv).astype(BF16), ckv_b, preferred_element_type=F32) * rkv
    bvT_ref[0, 0] = vv.astype(BF16)
    kr = seg(OFF_BKR, OFF_CQ)
    kr1 = kr[0:16]
    kr2 = kr[16:32]
    kro = jnp.concatenate([kr1 * cos_t - kr2 * sin_t, kr1 * sin_t + kr2 * cos_t,
                           one_row32], axis=0)
    kr_sq = _sum_sq_rows(kr)
    for hd in range(B_HEADS):
        knh = kn[64 * hd:64 * hd + 64]
        k_sq = _sum_sq_rows(knh) + kr_sq
        kn_ref[0, hd:hd + 1, :] = k_sq * lax.rsqrt(k_sq + TINY)
        kt = jnp.concatenate([knh, kro], axis=0)
        bk_ref[0, hd] = kt.T.astype(BF16)


def _proj_call(x2d, mod, winT_all, layer, wuqT, gq, wukT, wuvT, gkv, tab):
    tm = TOK_TILE
    T = x2d.shape[0]
    nj = SEQ // tm
    const2 = lambda i: (0, 0)
    bj3 = lambda i: (i // nj, 0, i % nj)
    bj4 = lambda i: (i // nj, 0, i % nj, 0)
    cj4 = lambda i: (i // nj, i % nj, 0, 0)
    out_shape = (
        jax.ShapeDtypeStruct((BATCH, 512, SEQ), BF16),
        jax.ShapeDtypeStruct((BATCH, A_HEADS, SEQ, 128), BF16),
        jax.ShapeDtypeStruct((BATCH, SEQ // A_TILE, 256, A_TILE), BF16),
        jax.ShapeDtypeStruct((BATCH, 512, SEQ), BF16),
        jax.ShapeDtypeStruct((BATCH, B_HEADS, SEQ, 128), BF16),
        jax.ShapeDtypeStruct((BATCH, SEQ // KV_TILE, 256, KV_TILE), BF16),
        jax.ShapeDtypeStruct((BATCH, 1024, SEQ), BF16),
        jax.ShapeDtypeStruct((BATCH, C_KV_HEADS, SEQ, 128), BF16),
        jax.ShapeDtypeStruct((BATCH, SEQ // KV_TILE, 128, KV_TILE), BF16),
        jax.ShapeDtypeStruct((BATCH, 16, SEQ), F32),
        jax.ShapeDtypeStruct((BATCH, 16, SEQ), F32),
    )
    out_specs = (
        pl.BlockSpec((1, 512, tm), bj3),
        pl.BlockSpec((1, A_HEADS, tm, 128), bj4),
        pl.BlockSpec((1, tm // A_TILE, 256, A_TILE), cj4),
        pl.BlockSpec((1, 512, tm), bj3),
        pl.BlockSpec((1, B_HEADS, tm, 128), bj4),
        pl.BlockSpec((1, tm // KV_TILE, 256, KV_TILE), cj4),
        pl.BlockSpec((1, 1024, tm), bj3),
        pl.BlockSpec((1, C_KV_HEADS, tm, 128), bj4),
        pl.BlockSpec((1, tm // KV_TILE, 128, KV_TILE), cj4),
        pl.BlockSpec((1, 16, tm), bj3),
        pl.BlockSpec((1, 16, tm), bj3),
    )
    in_specs = [
        pl.BlockSpec((tm, D_MODEL), lambda i: (i, 0)),
        pl.BlockSpec((1, 6, D_MODEL), lambda i: (i // nj, 0, 0)),
        pl.BlockSpec((1, IN_COLS, D_MODEL), lambda i: (layer, 0, 0)),
        pl.BlockSpec((B_Q_RANK, B_Q_RANK), const2),
        pl.BlockSpec((1, B_Q_RANK), const2),
        pl.BlockSpec((256, B_KV_RANK), const2),
        pl.BlockSpec((256, B_KV_RANK), const2),
        pl.BlockSpec((1, B_KV_RANK), const2),
        pl.BlockSpec((TAB_ROWS, tm), lambda i: (0, i % nj)),
    ]
    return pl.pallas_call(
        _proj_kernel,
        out_shape=out_shape,
        grid=(T // tm,),
        in_specs=in_specs,
        out_specs=out_specs,
        compiler_params=pltpu.CompilerParams(vmem_limit_bytes=VMEM_LIMIT),
        name="in_proj",
    )(x2d, mod, winT_all, wuqT, gq, wukT, wuvT, gkv, tab)


def _with_ones_rows(v):
    return jnp.concatenate([v, jnp.ones((16, v.shape[1]), BF16)], axis=0)


def _sublane_partial_sum(p):
    return functools.reduce(lambda a, b: a + b, [p[8 * i:8 * (i + 1)] for i in range(p.shape[0] // 8)])


def _sum_sq_rows(x):
    return jnp.sum(_sublane_partial_sum(x * x), axis=0, keepdims=True)


def _attn_finish(o_ref, num, den, groups, tq):
    o = num / den
    for g in range(groups):
        o_ref[0, 64 * g:64 * g + 64, :] = o[:, tq * g:tq * (g + 1)].astype(o_ref.dtype)


def _attn_online(q_strips, k_ref, vT_ref, o_ref, acc_ref, s_ref, p_ref, *, groups, tq, tk):
    nstrip = len(q_strips)
    w = q_strips[0].shape[1]
    nchunk = SEQ // tk
    rb = ROW_BLK
    nb = tk // rb
    nunit = nchunk * nstrip
    acc_ref[...] = jnp.zeros(acc_ref.shape, F32)
    m = [jnp.full((1, w), NEG_BIG, F32) for _ in range(nstrip)]

    def qk_block(u, b):
        c, j = divmod(u, nstrip)
        k = k_ref[0, 0, c * tk + rb * b:c * tk + rb * (b + 1), :]
        return jnp.dot(k, q_strips[j], preferred_element_type=F32)

    def pv_block(u, b, p_blk):
        c = u // nstrip
        v_aug = _with_ones_rows(vT_ref[0, c, :, rb * b:rb * (b + 1)])
        return jnp.dot(v_aug, p_blk, preferred_element_type=F32)

    alphas = {}
    for t in range(nunit + 2):
        pv = None
        for b in range(nb):
            rows = slice(rb * b, rb * (b + 1))
            if t < nunit:
                s_ref[t % RING, rows, :] = qk_block(t, b)
            if t >= 2:
                part = pv_block(t - 2, b, p_ref[(t - 2) % RING, rows, :])
                pv = part if pv is None else pv + part
        if 1 <= t <= nunit:
            u = t - 1
            j = u % nstrip
            s_u = s_ref[u % RING]
            m_new = jnp.maximum(m[j], jnp.max(s_u, axis=0, keepdims=True))
            alphas[u] = jnp.exp2(m[j] - m_new)
            p_ref[u % RING] = jnp.exp2(s_u - m_new).astype(BF16)
            m[j] = m_new
        if t >= 2:
            u = t - 2
            sl = slice(w * (u % nstrip), w * (u % nstrip + 1))
            acc_ref[:, sl] = alphas.pop(u) * acc_ref[:, sl] + pv
    _attn_finish(o_ref, acc_ref[0:64, :], acc_ref[64:65, :], groups, tq)


def _attn_bounded(q_strips, k_ref, vT_ref, o_ref, *, groups, tq, tk):
    nstrip = len(q_strips)
    rb = ROW_BLK
    nkb = SEQ // rb
    units = [(kb, j) for kb in range(nkb) for j in range(nstrip)]
    acc = [None] * nstrip
    den = [None] * nstrip
    probs = {}

    def pv(u):
        kb, j = units[u]
        c, b = divmod(kb, tk // rb)
        part = jnp.dot(vT_ref[0, c, :, rb * b:rb * (b + 1)], probs.pop(u), preferred_element_type=F32)
        acc[j] = part if acc[j] is None else acc[j] + part

    for t in range(len(units)):
        kb, j = units[t]
        s = jnp.dot(k_ref[0, 0, rb * kb:rb * (kb + 1), :], q_strips[j], preferred_element_type=F32)
        p = jnp.exp2(s)
        part = _sublane_partial_sum(p)
        den[j] = part if den[j] is None else den[j] + part
        probs[t] = p.astype(BF16)
        if t >= PV_LAG:
            pv(t - PV_LAG)
    for u in range(len(units) - PV_LAG, len(units)):
        pv(u)
    den = [jnp.sum(d, axis=0, keepdims=True) for d in den]
    _attn_finish(o_ref, jnp.concatenate(acc, axis=1), jnp.concatenate(den, axis=1), groups, tq)


def _full_attn_kernel(scal_ref, qT_ref, k_ref, vT_ref, o_ref, acc_ref, s_ref, p_ref,
                      *, groups, tq, tk, bound_row):
    n = groups * tq
    w = min(n, STRIP)
    nstrip = n // w
    pair = 2 * (pl.program_id(0) * pl.num_programs(1) + pl.program_id(1))
    k_bound = scal_ref[pair]
    bounded_ok = scal_ref[pair + 1] > 0.5
    cols = [qT_ref[0, 128 * g:128 * (g + 1), :] for g in range(groups)]
    q_cat = cols[0] if groups == 1 else jnp.concatenate(cols, axis=1)
    q_lo = q_cat[0:bound_row]
    q_bound = q_cat[bound_row:bound_row + 16]
    q_hi = q_cat[bound_row + 16:]

    def strips(mid):
        q = jnp.concatenate([q_lo, mid, q_hi], axis=0)
        return [q[:, w * j:w * (j + 1)] for j in range(nstrip)]

    @pl.when(bounded_ok)
    def _():
        mid = (q_bound.astype(F32) * k_bound).astype(BF16)
        _attn_bounded(strips(mid), k_ref, vT_ref, o_ref, groups=groups, tq=tq, tk=tk)

    @pl.when(jnp.logical_not(bounded_ok))
    def _():
        _attn_online(strips(jnp.zeros_like(q_bound)), k_ref, vT_ref, o_ref, acc_ref, s_ref, p_ref,
                     groups=groups, tq=tq, tk=tk)


def _bound_scalars(qn, kn, groups):
    k_bound = K_MARGIN * jnp.max(kn, axis=-1)
    q_max = jnp.max(qn, axis=-1).reshape(BATCH, kn.shape[1], groups).max(-1)
    ok = (q_max * k_bound <= BOUND_LIMIT).astype(F32)
    return jnp.stack([k_bound, ok], axis=-1).reshape(-1)


def _full_attn_call(scal, qT, k, vT, *, groups, tq, bound_row, name):
    tk = KV_TILE
    hkv = k.shape[1]
    n = groups * tq
    kern = functools.partial(_full_attn_kernel, groups=groups, tq=tq, tk=tk, bound_row=bound_row)
    return pl.pallas_call(
        kern,
        out_shape=jax.ShapeDtypeStruct((BATCH, hkv * groups * 64, SEQ), BF16),
        grid=(BATCH, hkv, SEQ // tq),
        in_specs=[
            pl.BlockSpec(memory_space=pltpu.SMEM),
            pl.BlockSpec((1, groups * 128, tq), lambda b, g, i: (b, g, i)),
            pl.BlockSpec((1, 1, SEQ, 128), lambda b, g, i: (b, g, 0, 0)),
            pl.BlockSpec((1, SEQ // tk, 64, tk), lambda b, g, i: (b, 0, g, 0)),
        ],
        out_specs=pl.BlockSpec((1, groups * 64, tq), lambda b, g, i: (b, g, i)),
        scratch_shapes=[pltpu.VMEM((80, n), F32),
                        pltpu.VMEM((RING, tk, min(n, STRIP)), F32),
                        pltpu.VMEM((RING, tk, min(n, STRIP)), BF16)],
        compiler_params=pltpu.CompilerParams(vmem_limit_bytes=VMEM_LIMIT),
        name=name,
    )(scal, qT, k, vT)


def _dil_attn_kernel(scal_ref, qT_ref, k_ref, vT_ref, bias_ref, o_ref):
    t = A_TILE
    nq = SEQ // t
    i = pl.program_id(1)
    first = jnp.clip(i - A_CENTER, 0, nq - A_NCHUNK)
    base = A_SCAL * pl.program_id(0)
    bounded_ok = scal_ref[base] > 0.5
    top_row = lax.broadcasted_iota(jnp.int32, (16, t), 0) == 0

    def q_head(h, mid_fn):
        q = qT_ref[0, 128 * h:128 * (h + 1), :]
        return jnp.concatenate([q[0:HEAD_DIM], mid_fn(h, q[HEAD_DIM:HEAD_DIM + 16]), q[HEAD_DIM + 16:]], axis=0)

    def score_tile(h, r, q):
        kt = first + r
        d = kt - i + A_CENTER
        bi = jnp.where((d >= 0) & (d < A_NCHUNK), d, A_NCHUNK)
        k = k_ref[0, h, pl.ds(pl.multiple_of(kt * t, t), t), :]
        return jnp.dot(k, q, preferred_element_type=F32) + bias_ref[h, bi]

    def pv_tile(h, r, p, ones_rows):
        v = vT_ref[0, first + r, 64 * h:64 * (h + 1), :]
        return jnp.dot(_with_ones_rows(v) if ones_rows else v, p, preferred_element_type=F32)

    def store(h, num, den):
        o_ref[0, 64 * h:64 * (h + 1), :] = (num / den).astype(o_ref.dtype)

    @pl.when(bounded_ok)
    def _():
        def mid(h, q_bound):
            shift = q_bound.astype(F32) * scal_ref[base + 1 + h]
            return (shift - jnp.where(top_row, scal_ref[base + 1 + A_HEADS + h], 0.0)).astype(BF16)

        qs = [q_head(h, mid) for h in range(A_HEADS)]
        units = [(h, r) for r in range(A_NCHUNK) for h in range(A_HEADS)]
        acc = [None] * A_HEADS
        den = [None] * A_HEADS
        probs = {}

        def pv(u):
            h, r = units[u]
            part = pv_tile(h, r, probs.pop(u), False)
            acc[h] = part if acc[h] is None else acc[h] + part

        for u, (h, r) in enumerate(units):
            p = jnp.exp2(score_tile(h, r, qs[h]))
            part = _sublane_partial_sum(p)
            den[h] = part if den[h] is None else den[h] + part
            probs[u] = p.astype(BF16)
            if u >= A_PV_LAG:
                pv(u - A_PV_LAG)
        for u in range(len(units) - A_PV_LAG, len(units)):
            pv(u)
        for h in range(A_HEADS):
            store(h, acc[h], jnp.sum(den[h], axis=0, keepdims=True))

    @pl.when(jnp.logical_not(bounded_ok))
    def _():
        def scores(h):
            q = q_head(h, lambda _, q_bound: jnp.zeros_like(q_bound))
            return [score_tile(h, r, q) for r in range(A_NCHUNK)]

        def finish(h, s_list):
            m = jnp.max(functools.reduce(jnp.maximum, s_list), axis=0, keepdims=True)
            acc = None
            for r in range(A_NCHUNK):
                part = pv_tile(h, r, jnp.exp2(s_list[r] - m).astype(BF16), True)
                acc = part if acc is None else acc + part
            store(h, acc[0:64], acc[64:65])

        s_next = scores(0)
        for h in range(A_HEADS):
            s_cur = s_next
            if h + 1 < A_HEADS:
                s_next = scores(h + 1)
            finish(h, s_cur)


def _dil_scalars(qn, kn, b_max, b_min):
    k_bound = K_MARGIN * jnp.max(kn, axis=-1)
    q_max = jnp.max(qn, axis=-1)
    spread = 2.0 * q_max * k_bound + (b_max - b_min)[None, :]
    ok = jnp.all(spread <= 2.0 * BOUND_LIMIT, axis=-1, keepdims=True).astype(F32)
    pad = jnp.zeros((BATCH, A_SCAL - 1 - 2 * A_HEADS), F32)
    return jnp.concatenate([ok, k_bound, jnp.broadcast_to(b_max[None, :], (BATCH, A_HEADS)), pad], axis=1).reshape(-1)


def _dil_attn_call(scal, qT, k, vT, bias):
    t = A_TILE
    return pl.pallas_call(
        _dil_attn_kernel,
        out_shape=jax.ShapeDtypeStruct((BATCH, A_HEADS * 64, SEQ), BF16),
        grid=(BATCH, SEQ // t),
        in_specs=[
            pl.BlockSpec(memory_space=pltpu.SMEM),
            pl.BlockSpec((1, A_HEADS * 128, t), lambda b, i: (b, 0, i)),
            pl.BlockSpec((1, A_HEADS, SEQ, 128), lambda b, i: (b, 0, 0, 0)),
            pl.BlockSpec((1, SEQ // t, A_HEADS * 64, t), lambda b, i: (b, 0, 0, 0)),
            pl.BlockSpec((A_HEADS, A_NCHUNK + 1, t, t), lambda b, i: (0, 0, 0, 0),
                         pipeline_mode=pl.Buffered(1)),
        ],
        out_specs=pl.BlockSpec((1, A_HEADS * 64, t), lambda b, i: (b, 0, i)),
        compiler_params=pltpu.CompilerParams(vmem_limit_bytes=VMEM_LIMIT),
        name="dilated_attn",
    )(scal, qT, k, vT, bias)


def _layer_norm(z, g, b):
    mu = jnp.mean(z, axis=-1, keepdims=True)
    d = z - mu
    var = jnp.mean(d * d, axis=-1, keepdims=True)
    return d * lax.rsqrt(var + EPS) * g + b


def _mlp_kernel(x_ref, mod_ref, aoT_ref, boT_ref, coT_ref, wo_ref, ln1g_ref, ln1b_ref,
                w1_ref, w2_ref, ln2g_ref, ln2b_ref, o_ref):
    g_a = mod_ref[0, 2:3, :]
    sh_m = mod_ref[0, 3:4, :]
    sc_m = mod_ref[0, 4:5, :]
    g_m = mod_ref[0, 5:6, :]
    tm = x_ref.shape[0]
    hr = tm // MLP_PARTS
    fc = 1024
    nfc = D_FF // fc

    def out_proj(part):
        cols = slice(hr * part, hr * (part + 1))
        catT = jnp.concatenate([aoT_ref[0, :, cols], boT_ref[0, :, cols], coT_ref[0, :, cols]], axis=0)
        cat = catT.astype(F32).T.astype(BF16)
        return jnp.dot(cat, wo_ref[0], preferred_element_type=F32)

    ys = [out_proj(part) for part in range(MLP_PARTS)]
    prev_out = None
    for part in range(MLP_PARTS):
        rows = slice(hr * part, hr * (part + 1))
        x1 = _layer_norm(ALPHA * x_ref[rows, :] + (1.0 + g_a) * ys[part], ln1g_ref[...], ln1b_ref[...])
        h = (x1 * (1.0 + sc_m) + sh_m).astype(BF16)
        y2 = None
        for j in range(nfc):
            u = jnp.dot(h, w1_ref[0, :, fc * j:fc * (j + 1)], preferred_element_type=F32)
            u = jnp.maximum(u, 0.0)
            u = (u * u).astype(BF16)
            if prev_out is not None and j == nfc - 1:
                folded = _sublane_partial_sum(prev_out)
                folded = functools.reduce(lambda a, b: a + b,
                                          [folded[:, LANES * i:LANES * (i + 1)] for i in range(D_MODEL // LANES)])
                zero = _exact_zero_of(jnp.concatenate([folded, folded], axis=0))
                top = jnp.concatenate([u[0:16, 0:LANES] + zero, u[0:16, LANES:]], axis=1)
                u = jnp.concatenate([top, u[16:]], axis=0)
            term = jnp.dot(u, w2_ref[0, fc * j:fc * (j + 1), :], preferred_element_type=F32)
            y2 = term if y2 is None else y2 + term
        prev_out = _layer_norm(ALPHA * x1 + (1.0 + g_m) * y2, ln2g_ref[...], ln2b_ref[...])
        o_ref[rows, :] = prev_out


def _mlp_call(x2d, mod, aoT, boT, coT, layer, wo_all, ln1g, ln1b, w1_all, w2_all, ln2g, ln2b):
    tm = MLP_TILE
    T = x2d.shape[0]
    nj = SEQ // tm
    const2 = lambda i: (0, 0)
    bj3 = lambda i: (i // nj, 0, i % nj)
    one = pl.Buffered(1)
    in_specs = [
        pl.BlockSpec((tm, D_MODEL), lambda i: (i, 0)),
        pl.BlockSpec((1, 6, D_MODEL), lambda i: (i // nj, 0, 0)),
        pl.BlockSpec((1, 256, tm), bj3),
        pl.BlockSpec((1, 256, tm), bj3),
        pl.BlockSpec((1, 512, tm), bj3),
        pl.BlockSpec((1, D_MODEL, D_MODEL), lambda i: (layer, 0, 0), pipeline_mode=one),
        pl.BlockSpec((1, D_MODEL), const2),
        pl.BlockSpec((1, D_MODEL), const2),
        pl.BlockSpec((1, D_MODEL, D_FF), lambda i: (layer, 0, 0), pipeline_mode=one),
        pl.BlockSpec((1, D_FF, D_MODEL), lambda i: (layer, 0, 0), pipeline_mode=one),
        pl.BlockSpec((1, D_MODEL), const2),
        pl.BlockSpec((1, D_MODEL), const2),
    ]
    return pl.pallas_call(
        _mlp_kernel,
        out_shape=jax.ShapeDtypeStruct((T, D_MODEL), F32),
        grid=(T // tm,),
        in_specs=in_specs,
        out_specs=pl.BlockSpec((tm, D_MODEL), lambda i: (i, 0)),
        compiler_params=pltpu.CompilerParams(vmem_limit_bytes=VMEM_LIMIT),
        name="out_mlp",
    )(x2d, mod, aoT, boT, coT, wo_all, ln1g, ln1b, w1_all, w2_all, ln2g, ln2b)


def _rope_angles(pos, dim):
    inv = ROPE_THETA ** (-jnp.arange(0, dim, 2, dtype=F32) / dim)
    return pos.astype(F32)[:, None] * inv[None, :]


def _t5_bucket(rel):
    nb = REL_BUCKETS // 2
    max_exact = nb // 2
    sign = jnp.where(rel > 0, nb, 0)
    n = jnp.abs(rel)
    nf = jnp.maximum(n, 1).astype(F32)
    large = max_exact + (jnp.log(nf / max_exact) / math.log(REL_MAX_DIST / max_exact)
                         * (nb - max_exact)).astype(jnp.int32)
    large = jnp.minimum(large, nb - 1)
    return sign + jnp.where(n < max_exact, n, large)


def _dilated_bias_tiles(rel_bias):
    t = A_TILE
    span = (A_CENTER + 1) * t
    deltas = np.arange(-span + 1, span)
    mult = np.zeros(deltas.shape, np.int32)
    for (w, d) in A_PATTERNS:
        half = w // (2 * d)
        mult += ((deltas % d == 0) & (np.abs(deltas) <= half * d)).astype(np.int32)
    bucket = _t5_bucket(jnp.asarray(deltas, jnp.int32))
    logm = jnp.asarray(np.log(np.maximum(mult, 1)), F32)
    tab = jnp.where(jnp.asarray(mult > 0)[:, None],
                    (rel_bias[bucket] + logm[:, None]) * LOG2E, NEG_BIG)
    valid = jnp.asarray(mult > 0)[:, None]
    b_max = jnp.max(jnp.where(valid, tab, -jnp.inf), axis=0)
    b_min = jnp.min(jnp.where(valid, tab, jnp.inf), axis=0)
    x = np.arange(2 * t)
    d = np.where(x <= t - 1, -x, 2 * t - x)
    d[t] = 0
    idx = (np.arange(A_NCHUNK)[:, None] - A_CENTER) * t + d[None, :] + span - 1
    u = jnp.moveaxis(tab[jnp.asarray(idx, jnp.int32)], -1, 0)
    u = jnp.concatenate([u, jnp.full((A_HEADS, 1, 2 * t), NEG_BIG, F32)], axis=1)
    return _toeplitz_call(u.reshape(A_HEADS * (A_NCHUNK + 1), 1, 2 * t)), b_max, b_min


def _toeplitz_kernel(u_ref, o_ref):
    t = A_TILE
    for c in range(u_ref.shape[0]):
        rows = jnp.broadcast_to(u_ref[c], (t, 2 * t))
        o_ref[c] = pltpu.roll(rows, 0, 1, stride=1, stride_axis=0)[:, :t]


def _toeplitz_call(u):
    t = A_TILE
    n = u.shape[0]
    per = A_NCHUNK + 1
    tiles = pl.pallas_call(
        _toeplitz_kernel,
        out_shape=jax.ShapeDtypeStruct((n, t, t), F32),
        grid=(n // per,),
        in_specs=[pl.BlockSpec((per, 1, 2 * t), lambda i: (i, 0, 0))],
        out_specs=pl.BlockSpec((per, t, t), lambda i: (i, 0, 0)),
        name="bias_tiles",
    )(u)
    return tiles.reshape(A_HEADS, A_NCHUNK + 1, t, t)


def _rope_tables(gq, gk):
    t = jnp.arange(SEQ)
    ang_t = _rope_angles(t, B_ROPE)
    ang_row = _rope_angles(t // GRID_W, HEAD_DIM // 2)
    ang_col = _rope_angles(t % GRID_W, HEAD_DIM // 2)
    cos64 = jnp.concatenate([jnp.cos(ang_row)] * 2 + [jnp.cos(ang_col)] * 2, axis=1)
    sin_r, sin_c = jnp.sin(ang_row), jnp.sin(ang_col)
    sin64 = jnp.concatenate([-sin_r, sin_r, -sin_c, sin_c], axis=1)
    perm = np.concatenate([np.arange(16, 32), np.arange(0, 16), np.arange(48, 64), np.arange(32, 48)])

    def with_gain(g, scale):
        return (cos64 * g[None, :] * scale).T, (sin64 * g[perm][None, :] * scale).T

    qc, qs = with_gain(gq, HEAD_DIM ** -0.5 * LOG2E)
    kc, ks = with_gain(gk, 1.0)
    q_bound, _ = _gqa_bounds(gq, gk)
    bound_rows = jnp.zeros((16, SEQ), F32).at[0].set(-q_bound)
    return jnp.concatenate([jnp.cos(ang_t).T, jnp.sin(ang_t).T, qc, qs, kc, ks, bound_rows], axis=0)


def _gqa_bounds(gq, gk):
    root = math.sqrt(HEAD_DIM)
    q_bound = Q_MARGIN * root * (HEAD_DIM ** -0.5 * LOG2E) * jnp.max(jnp.abs(gq))
    k_bound = root * jnp.max(jnp.abs(gk))
    return q_bound, k_bound


def kernel(x, c, w_ada, b_ada, w_in, mla_q_norm, mla_w_uq, mla_kv_norm, mla_w_ukv,
           gqa_q_norm, gqa_k_norm, rel_bias, w_o, ln1_g, ln1_b, w1, w2, ln2_g, ln2_b):
    B, S, D = x.shape
    c_pad = jnp.zeros((8, D), F32).at[:B].set(c)
    mod_all = _ada_call(c_pad, w_ada, b_ada)[:, :B].reshape(DEPTH, B, 6, D)
    bias, b_max, b_min = _dilated_bias_tiles(rel_bias)

    winT_all = jnp.swapaxes(w_in, 1, 2).astype(BF16)
    wo_all, w1_all, w2_all = w_o.astype(BF16), w1.astype(BF16), w2.astype(BF16)
    x2d = x.reshape(B * S, D)
    for l in range(DEPTH):
        wuqT = mla_w_uq[l].T
        wukv = mla_w_ukv[l].reshape(B_KV_RANK, B_HEADS, B_NOPE + B_V)
        wukT = wukv[:, :, :B_NOPE].reshape(B_KV_RANK, B_HEADS * B_NOPE).T
        wuvT = wukv[:, :, B_NOPE:].reshape(B_KV_RANK, B_HEADS * B_V).T
        tab = _rope_tables(gqa_q_norm[l], gqa_k_norm[l])
        aqT, ak, avT, bqT, bk, bvT, cqT, ck, cvT, qn, kn = _proj_call(
            x2d, mod_all[l], winT_all, l, wuqT, mla_q_norm[l][None, :], wukT, wuvT,
            mla_kv_norm[l][None, :], tab)
        a_scal = _dil_scalars(qn[:, QN_A:QN_A + A_HEADS], kn[:, KN_A:KN_A + A_HEADS], b_max, b_min)
        aoT = _dil_attn_call(a_scal, aqT, ak, avT, bias)
        cg = C_Q_HEADS // C_KV_HEADS
        b_scal = _bound_scalars(qn[:, :B_HEADS], kn[:, :B_HEADS], 1)
        cq_bound, ck_bound = _gqa_bounds(gqa_q_norm[l], gqa_k_norm[l])
        c_scal = _bound_scalars(jnp.full((B, C_Q_HEADS, 1), cq_bound), jnp.full((B, C_KV_HEADS, 1), ck_bound), cg)
        boT = _full_attn_call(b_scal, bqT, bk, bvT, groups=1, tq=2048, bound_row=B_NOPE + B_ROPE,
                              name="mla_attn")
        coT = _full_attn_call(c_scal, cqT, ck, cvT, groups=cg, tq=512, bound_row=HEAD_DIM, name="gqa_attn")
        x2d = _mlp_call(x2d, mod_all[l], aoT, boT, coT, l, wo_all,
                        ln1_g[l][None, :], ln1_b[l][None, :], w1_all, w2_all,
                        ln2_g[l][None, :], ln2_b[l][None, :])
    return x2d.reshape(B, S, D)
```

```python
import functools
import math

import numpy as np
import jax
import jax.numpy as jnp
from jax import lax
from jax.experimental import pallas as pl
from jax.experimental.pallas import tpu as pltpu

D_MODEL = 1024
BATCH = 4
SEQ = 4096
DEPTH = 2
HEAD_DIM = 64
A_HEADS = 4
A_PATTERNS = ((128, 1), (512, 4), (2048, 16))
B_HEADS = 4
B_Q_RANK = 384
B_KV_RANK = 256
B_NOPE = 64
B_ROPE = 32
B_V = 64
C_Q_HEADS = 8
C_KV_HEADS = 2
GRID_W = 64
ROPE_THETA = 10000.0
REL_BUCKETS = 32
REL_MAX_DIST = 1024
D_FF = 4 * D_MODEL
EPS = 1e-6
NEG_BIG = -1e30
ALPHA = (2 * DEPTH) ** 0.25
A_W = A_HEADS * HEAD_DIM
IN_COLS = 2208
LOG2E = 1.4426950408889634

F32 = jnp.float32
BF16 = jnp.bfloat16

LANES = 128
VMEM_LIMIT = 56 * 1024 * 1024

OFF_AQ, OFF_AK, OFF_AV = 0, 256, 512
OFF_BCQ, OFF_BCKV, OFF_BKR = 768, 1152, 1408
OFF_CQ, OFF_CK, OFF_CV, OFF_END = 1440, 1952, 2080, 2208

TOK_TILE = 512
MLP_TILE = 1024
MLP_PARTS = 4
A_TILE = 256
A_NCHUNK = 9
A_CENTER = 4
KV_TILE = 512
ROW_BLK = 256
QN_A = 12
KN_A = 8
A_SCAL = 12
A_PV_LAG = 4
PV_LAG = 2
RING = 3
STRIP = 512
TAB_CBOUND = 288
TAB_ROWS = 304
TINY = 1e-30
Q_MARGIN = 1.02
K_MARGIN = 1.01
BOUND_LIMIT = 50.0


def _exact_zero_of(v):
    bits = lax.bitcast_convert_type(v, jnp.uint32)
    return ((bits >> 16) >> 16).astype(F32).astype(BF16)


def _nt_dot(a, b):
    return lax.dot_general(a, b, (((1,), (1,)), ((), ())), preferred_element_type=F32)


def _ada_kernel(c_ref, w_ref, b_ref, o_ref):
    c = c_ref[...]
    ca = (c / (1.0 + jnp.exp(-c))).astype(BF16)
    w = w_ref[0].astype(BF16)
    o_ref[0] = jnp.dot(ca, w, preferred_element_type=F32) + b_ref[0]


def _ada_call(c_pad, w_ada, b_ada):
    tn = 1536
    n = 6 * D_MODEL
    return pl.pallas_call(
        _ada_kernel,
        out_shape=jax.ShapeDtypeStruct((DEPTH, 8, n), F32),
        grid=(DEPTH, n // tn),
        in_specs=[
            pl.BlockSpec((8, D_MODEL), lambda l, j: (0, 0)),
            pl.BlockSpec((1, D_MODEL, tn), lambda l, j: (l, 0, j)),
            pl.BlockSpec((1, 1, tn), lambda l, j: (l, 0, j)),
        ],
        out_specs=pl.BlockSpec((1, 8, tn), lambda l, j: (l, 0, j)),
        compiler_params=pltpu.CompilerParams(vmem_limit_bytes=VMEM_LIMIT),
        name="ada_mod",
    )(c_pad, w_ada, b_ada.reshape(DEPTH, 1, n))


def _proj_kernel(x_ref, mod_ref, winT_ref, wuqT_ref, gq_ref, wukT_ref, wuvT_ref, gkv_ref, tab_ref,
                 aqT_ref, ak_ref, avT_ref, bqT_ref, bk_ref, bvT_ref, cqT_ref, ck_ref, cvT_ref,
                 qn_ref, kn_ref):
    tm = x_ref.shape[0]
    sh = mod_ref[0, 0:1, :]
    sc = mod_ref[0, 1:2, :]
    h = (x_ref[...] * (1.0 + sc) + sh).astype(BF16)

    proj_t = _nt_dot(winT_ref[0], h)

    def seg(r0, r1):
        return proj_t[r0:r1]

    def first_row(rows, val):
        return jnp.where(lax.broadcasted_iota(jnp.int32, (rows, tm), 0) == 0, val, 0.0)

    def norm(xh):
        n2 = _sum_sq_rows(xh)
        return n2 * lax.rsqrt(n2 + TINY)

    one_row32 = first_row(32, 1.0)
    one_row64 = first_row(64, 1.0)
    qn_ref[0] = jnp.zeros(qn_ref.shape[1:], F32)
    kn_ref[0] = jnp.zeros(kn_ref.shape[1:], F32)

    aq = seg(OFF_AQ, OFF_AK) * (HEAD_DIM ** -0.5 * LOG2E)
    for hd in range(A_HEADS):
        qh = aq[64 * hd:64 * hd + 64]
        qb = Q_MARGIN * norm(qh)
        qn_ref[0, QN_A + hd:QN_A + hd + 1, :] = qb
        aqT_ref[0, 128 * hd:128 * hd + 64, :] = qh.astype(BF16)
        aqT_ref[0, 128 * hd + 64:128 * hd + 128, :] = first_row(64, -qb).astype(BF16)
    ak = seg(OFF_AK, OFF_AV)
    for hd in range(A_HEADS):
        kh = ak[64 * hd:64 * hd + 64]
        kn_ref[0, KN_A + hd:KN_A + hd + 1, :] = norm(kh)
        kt = jnp.concatenate([kh, one_row64], axis=0)
        ak_ref[0, hd] = kt.T.astype(BF16)
    av = seg(OFF_AV, OFF_BCQ).astype(BF16)
    for j in range(tm // A_TILE):
        avT_ref[0, j] = av[:, A_TILE * j:A_TILE * (j + 1)]

    def normrope(xh, c, s):
        r = lax.rsqrt(_sum_sq_rows(xh) * (1.0 / xh.shape[0]) + EPS)
        rot = jnp.concatenate([xh[16:32], xh[0:16], xh[48:64], xh[32:48]], axis=0)
        return (xh * c + rot * s) * r

    cqc = tab_ref[32:96, :]
    cqs = tab_ref[96:160, :]
    c_bound_rows = tab_ref[TAB_CBOUND:TAB_CBOUND + 16, :].astype(BF16)
    cq2 = seg(OFF_CQ, OFF_CK)
    for hd in range(C_Q_HEADS):
        qh = normrope(cq2[64 * hd:64 * hd + 64], cqc, cqs)
        cqT_ref[0, 128 * hd:128 * hd + 64, :] = qh.astype(BF16)
        cqT_ref[0, 128 * hd + 64:128 * hd + 80, :] = c_bound_rows
        cqT_ref[0, 128 * hd + 80:128 * hd + 128, :] = jnp.zeros((48, tm), BF16)
    ckc = tab_ref[160:224, :]
    cks = tab_ref[224:288, :]
    ck2 = seg(OFF_CK, OFF_CV)
    for hd in range(C_KV_HEADS):
        kh = normrope(ck2[64 * hd:64 * hd + 64], ckc, cks)
        kt = jnp.concatenate([kh, one_row64], axis=0)
        ck_ref[0, hd] = kt.T.astype(BF16)
    cvT_ref[0, 0] = seg(OFF_CV, OFF_END).astype(BF16)

    cos_t = tab_ref[0:16, :]
    sin_t = tab_ref[16:32, :]
    cq = seg(OFF_BCQ, OFF_BCKV)
    rq = lax.rsqrt(_sum_sq_rows(cq) * (1.0 / B_Q_RANK) + EPS)
    wq = (wuqT_ref[...] * gq_ref[...]).astype(BF16)
    uq = jnp.dot(wq, cq.astype(BF16), preferred_element_type=F32)
    uq = uq * (rq * ((B_NOPE + B_ROPE) ** -0.5 * LOG2E))
    for hd in range(B_HEADS):
        b0 = (B_NOPE + B_ROPE) * hd
        x1 = uq[b0 + 64:b0 + 80]
        x2 = uq[b0 + 80:b0 + 96]
        qb = Q_MARGIN * norm(uq[b0:b0 + 96])
        qn_ref[0, hd:hd + 1, :] = qb
        bqT_ref[0, 128 * hd:128 * hd + 64, :] = uq[b0:b0 + 64].astype(BF16)
        bqT_ref[0, 128 * hd + 64:128 * hd + 80, :] = (x1 * cos_t - x2 * sin_t).astype(BF16)
        bqT_ref[0, 128 * hd + 80:128 * hd + 96, :] = (x1 * sin_t + x2 * cos_t).astype(BF16)
        bqT_ref[0, 128 * hd + 96:128 * hd + 128, :] = first_row(32, -qb).astype(BF16)
    ckv = seg(OFF_BCKV, OFF_BKR)
    rkv = lax.rsqrt(_sum_sq_rows(ckv) * (1.0 / B_KV_RANK) + EPS)
    ckv_b = ckv.astype(BF16)
    gkv = gkv_ref[...]
    kn = jnp.dot((wukT_ref[...] * gkv).astype(BF16), ckv_b, preferred_element_type=F32) * rkv
    vv = jnp.dot((wuvT_ref[...] * gkv).astype(BF16), ckv_b, preferred_element_type=F32) * rkv
    bvT_ref[0, 0] = vv.astype(BF16)
    kr = seg(OFF_BKR, OFF_CQ)
    kr1 = kr[0:16]
    kr2 = kr[16:32]
    kro = jnp.concatenate([kr1 * cos_t - kr2 * sin_t, kr1 * sin_t + kr2 * cos_t,
                           one_row32], axis=0)
    kr_sq = _sum_sq_rows(kr)
    for hd in range(B_HEADS):
        knh = kn[64 * hd:64 * hd + 64]
        k_sq = _sum_sq_rows(knh) + kr_sq
        kn_ref[0, hd:hd + 1, :] = k_sq * lax.rsqrt(k_sq + TINY)
        kt = jnp.concatenate([knh, kro], axis=0)
        bk_ref[0, hd] = kt.T.astype(BF16)


def _proj_call(x2d, mod, winT_all, layer, wuqT, gq, wukT, wuvT, gkv, tab):
    tm = TOK_TILE
    T = x2d.shape[0]
    nj = SEQ // tm
    const2 = lambda i: (0, 0)
    bj3 = lambda i: (i // nj, 0, i % nj)
    bj4 = lambda i: (i // nj, 0, i % nj, 0)
    cj4 = lambda i: (i // nj, i % nj, 0, 0)
    out_shape = (
        jax.ShapeDtypeStruct((BATCH, 512, SEQ), BF16),
        jax.ShapeDtypeStruct((BATCH, A_HEADS, SEQ, 128), BF16),
        jax.ShapeDtypeStruct((BATCH, SEQ // A_TILE, 256, A_TILE), BF16),
        jax.ShapeDtypeStruct((BATCH, 512, SEQ), BF16),
        jax.ShapeDtypeStruct((BATCH, B_HEADS, SEQ, 128), BF16),
        jax.ShapeDtypeStruct((BATCH, SEQ // KV_TILE, 256, KV_TILE), BF16),
        jax.ShapeDtypeStruct((BATCH, 1024, SEQ), BF16),
        jax.ShapeDtypeStruct((BATCH, C_KV_HEADS, SEQ, 128), BF16),
        jax.ShapeDtypeStruct((BATCH, SEQ // KV_TILE, 128, KV_TILE), BF16),
        jax.ShapeDtypeStruct((BATCH, 16, SEQ), F32),
        jax.ShapeDtypeStruct((BATCH, 16, SEQ), F32),
    )
    out_specs = (
        pl.BlockSpec((1, 512, tm), bj3),
        pl.BlockSpec((1, A_HEADS, tm, 128), bj4),
        pl.BlockSpec((1, tm // A_TILE, 256, A_TILE), cj4),
        pl.BlockSpec((1, 512, tm), bj3),
        pl.BlockSpec((1, B_HEADS, tm, 128), bj4),
        pl.BlockSpec((1, tm // KV_TILE, 256, KV_TILE), cj4),
        pl.BlockSpec((1, 1024, tm), bj3),
        pl.BlockSpec((1, C_KV_HEADS, tm, 128), bj4),
        pl.BlockSpec((1, tm // KV_TILE, 128, KV_TILE), cj4),
        pl.BlockSpec((1, 16, tm), bj3),
        pl.BlockSpec((1, 16, tm), bj3),
    )
    in_specs = [
        pl.BlockSpec((tm, D_MODEL), lambda i: (i, 0)),
        pl.BlockSpec((1, 6, D_MODEL), lambda i: (i // nj, 0, 0)),
        pl.BlockSpec((1, IN_COLS, D_MODEL), lambda i: (layer, 0, 0)),
        pl.BlockSpec((B_Q_RANK, B_Q_RANK), const2),
        pl.BlockSpec((1, B_Q_RANK), const2),
        pl.BlockSpec((256, B_KV_RANK), const2),
        pl.BlockSpec((256, B_KV_RANK), const2),
        pl.BlockSpec((1, B_KV_RANK), const2),
        pl.BlockSpec((TAB_ROWS, tm), lambda i: (0, i % nj)),
    ]
    return pl.pallas_call(
        _proj_kernel,
        out_shape=out_shape,
        grid=(T // tm,),
        in_specs=in_specs,
        out_specs=out_specs,
        compiler_params=pltpu.CompilerParams(vmem_limit_bytes=VMEM_LIMIT),
        name="in_proj",
    )(x2d, mod, winT_all, wuqT, gq, wukT, wuvT, gkv, tab)


def _with_ones_rows(v):
    return jnp.concatenate([v, jnp.ones((16, v.shape[1]), BF16)], axis=0)


def _sublane_partial_sum(p):
    return functools.reduce(lambda a, b: a + b, [p[8 * i:8 * (i + 1)] for i in range(p.shape[0] // 8)])


def _sum_sq_rows(x):
    return jnp.sum(_sublane_partial_sum(x * x), axis=0, keepdims=True)


def _attn_finish(o_ref, num, den, groups, tq):
    o = num / den
    for g in range(groups):
        o_ref[0, 64 * g:64 * g + 64, :] = o[:, tq * g:tq * (g + 1)].astype(o_ref.dtype)


def _attn_online(q_strips, k_ref, vT_ref, o_ref, acc_ref, s_ref, p_ref, *, groups, tq, tk):
    nstrip = len(q_strips)
    w = q_strips[0].shape[1]
    nchunk = SEQ // tk
    rb = ROW_BLK
    nb = tk // rb
    nunit = nchunk * nstrip
    acc_ref[...] = jnp.zeros(acc_ref.shape, F32)
    m = [jnp.full((1, w), NEG_BIG, F32) for _ in range(nstrip)]

    def qk_block(u, b):
        c, j = divmod(u, nstrip)
        k = k_ref[0, 0, c * tk + rb * b:c * tk + rb * (b + 1), :]
        return jnp.dot(k, q_strips[j], preferred_element_type=F32)

    def pv_block(u, b, p_blk):
        c = u // nstrip
        v_aug = _with_ones_rows(vT_ref[0, c, :, rb * b:rb * (b + 1)])
        return jnp.dot(v_aug, p_blk, preferred_element_type=F32)

    alphas = {}
    for t in range(nunit + 2):
        pv = None
        for b in range(nb):
            rows = slice(rb * b, rb * (b + 1))
            if t < nunit:
                s_ref[t % RING, rows, :] = qk_block(t, b)
            if t >= 2:
                part = pv_block(t - 2, b, p_ref[(t - 2) % RING, rows, :])
                pv = part if pv is None else pv + part
        if 1 <= t <= nunit:
            u = t - 1
            j = u % nstrip
            s_u = s_ref[u % RING]
            m_new = jnp.maximum(m[j], jnp.max(s_u, axis=0, keepdims=True))
            alphas[u] = jnp.exp2(m[j] - m_new)
            p_ref[u % RING] = jnp.exp2(s_u - m_new).astype(BF16)
            m[j] = m_new
        if t >= 2:
            u = t - 2
            sl = slice(w * (u % nstrip), w * (u % nstrip + 1))
            acc_ref[:, sl] = alphas.pop(u) * acc_ref[:, sl] + pv
    _attn_finish(o_ref, acc_ref[0:64, :], acc_ref[64:65, :], groups, tq)


def _attn_bounded(q_strips, k_ref, vT_ref, o_ref, *, groups, tq, tk):
    nstrip = len(q_strips)
    rb = ROW_BLK
    nkb = SEQ // rb
    units = [(kb, j) for kb in range(nkb) for j in range(nstrip)]
    acc = [None] * nstrip
    den = [None] * nstrip
    probs = {}

    def pv(u):
        kb, j = units[u]
        c, b = divmod(kb, tk // rb)
        part = jnp.dot(vT_ref[0, c, :, rb * b:rb * (b + 1)], probs.pop(u), preferred_element_type=F32)
        acc[j] = part if acc[j] is None else acc[j] + part

    for t in range(len(units)):
        kb, j = units[t]
        s = jnp.dot(k_ref[0, 0, rb * kb:rb * (kb + 1), :], q_strips[j], preferred_element_type=F32)
        p = jnp.exp2(s)
        part = _sublane_partial_sum(p)
        den[j] = part if den[j] is None else den[j] + part
        probs[t] = p.astype(BF16)
        if t >= PV_LAG:
            pv(t - PV_LAG)
    for u in range(len(units) - PV_LAG, len(units)):
        pv(u)
    den = [jnp.sum(d, axis=0, keepdims=True) for d in den]
    _attn_finish(o_ref, jnp.concatenate(acc, axis=1), jnp.concatenate(den, axis=1), groups, tq)


def _full_attn_kernel(scal_ref, qT_ref, k_ref, vT_ref, o_ref, acc_ref, s_ref, p_ref,
                      *, groups, tq, tk, bound_row):
    n = groups * tq
    w = min(n, STRIP)
    nstrip = n // w
    pair = 2 * (pl.program_id(0) * pl.num_programs(1) + pl.program_id(1))
    k_bound = scal_ref[pair]
    bounded_ok = scal_ref[pair + 1] > 0.5
    cols = [qT_ref[0, 128 * g:128 * (g + 1), :] for g in range(groups)]
    q_cat = cols[0] if groups == 1 else jnp.concatenate(cols, axis=1)
    q_lo = q_cat[0:bound_row]
    q_bound = q_cat[bound_row:bound_row + 16]
    q_hi = q_cat[bound_row + 16:]

    def strips(mid):
        q = jnp.concatenate([q_lo, mid, q_hi], axis=0)
        return [q[:, w * j:w * (j + 1)] for j in range(nstrip)]

    @pl.when(bounded_ok)
    def _():
        mid = (q_bound.astype(F32) * k_bound).astype(BF16)
        _attn_bounded(strips(mid), k_ref, vT_ref, o_ref, groups=groups, tq=tq, tk=tk)

    @pl.when(jnp.logical_not(bounded_ok))
    def _():
        _attn_online(strips(jnp.zeros_like(q_bound)), k_ref, vT_ref, o_ref, acc_ref, s_ref, p_ref,
                     groups=groups, tq=tq, tk=tk)


def _bound_scalars(qn, kn, groups):
    k_bound = K_MARGIN * jnp.max(kn, axis=-1)
    q_max = jnp.max(qn, axis=-1).reshape(BATCH, kn.shape[1], groups).max(-1)
    ok = (q_max * k_bound <= BOUND_LIMIT).astype(F32)
    return jnp.stack([k_bound, ok], axis=-1).reshape(-1)


def _full_attn_call(scal, qT, k, vT, *, groups, tq, bound_row, name):
    tk = KV_TILE
    hkv = k.shape[1]
    n = groups * tq
    kern = functools.partial(_full_attn_kernel, groups=groups, tq=tq, tk=tk, bound_row=bound_row)
    return pl.pallas_call(
        kern,
        out_shape=jax.ShapeDtypeStruct((BATCH, hkv * groups * 64, SEQ), BF16),
        grid=(BATCH, hkv, SEQ // tq),
        in_specs=[
            pl.BlockSpec(memory_space=pltpu.SMEM),
            pl.BlockSpec((1, groups * 128, tq), lambda b, g, i: (b, g, i)),
            pl.BlockSpec((1, 1, SEQ, 128), lambda b, g, i: (b, g, 0, 0)),
            pl.BlockSpec((1, SEQ // tk, 64, tk), lambda b, g, i: (b, 0, g, 0)),
        ],
        out_specs=pl.BlockSpec((1, groups * 64, tq), lambda b, g, i: (b, g, i)),
        scratch_shapes=[pltpu.VMEM((80, n), F32),
                        pltpu.VMEM((RING, tk, min(n, STRIP)), F32),
                        pltpu.VMEM((RING, tk, min(n, STRIP)), BF16)],
        compiler_params=pltpu.CompilerParams(vmem_limit_bytes=VMEM_LIMIT),
        name=name,
    )(scal, qT, k, vT)


def _dil_attn_kernel(scal_ref, qT_ref, k_ref, vT_ref, bias_ref, o_ref):
    t = A_TILE
    nq = SEQ // t
    i = pl.program_id(1)
    first = jnp.clip(i - A_CENTER, 0, nq - A_NCHUNK)
    base = A_SCAL * pl.program_id(0)
    bounded_ok = scal_ref[base] > 0.5
    top_row = lax.broadcasted_iota(jnp.int32, (16, t), 0) == 0

    def q_head(h, mid_fn):
        q = qT_ref[0, 128 * h:128 * (h + 1), :]
        return jnp.concatenate([q[0:HEAD_DIM], mid_fn(h, q[HEAD_DIM:HEAD_DIM + 16]), q[HEAD_DIM + 16:]], axis=0)

    def score_tile(h, r, q):
        kt = first + r
        d = kt - i + A_CENTER
        bi = jnp.where((d >= 0) & (d < A_NCHUNK), d, A_NCHUNK)
        k = k_ref[0, h, pl.ds(pl.multiple_of(kt * t, t), t), :]
        return jnp.dot(k, q, preferred_element_type=F32) + bias_ref[h, bi]

    def pv_tile(h, r, p, ones_rows):
        v = vT_ref[0, first + r, 64 * h:64 * (h + 1), :]
        return jnp.dot(_with_ones_rows(v) if ones_rows else v, p, preferred_element_type=F32)

    def store(h, num, den):
        o_ref[0, 64 * h:64 * (h + 1), :] = (num / den).astype(o_ref.dtype)

    @pl.when(bounded_ok)
    def _():
        def mid(h, q_bound):
            shift = q_bound.astype(F32) * scal_ref[base + 1 + h]
            return (shift - jnp.where(top_row, scal_ref[base + 1 + A_HEADS + h], 0.0)).astype(BF16)

        qs = [q_head(h, mid) for h in range(A_HEADS)]
        units = [(h, r) for r in range(A_NCHUNK) for h in range(A_HEADS)]
        acc = [None] * A_HEADS
        den = [None] * A_HEADS
        probs = {}

        def pv(u):
            h, r = units[u]
            part = pv_tile(h, r, probs.pop(u), False)
            acc[h] = part if acc[h] is None else acc[h] + part

        for u, (h, r) in enumerate(units):
            p = jnp.exp2(score_tile(h, r, qs[h]))
            part = _sublane_partial_sum(p)
            den[h] = part if den[h] is None else den[h] + part
            probs[u] = p.astype(BF16)
            if u >= A_PV_LAG:
                pv(u - A_PV_LAG)
        for u in range(len(units) - A_PV_LAG, len(units)):
            pv(u)
        for h in range(A_HEADS):
            store(h, acc[h], jnp.sum(den[h], axis=0, keepdims=True))

    @pl.when(jnp.logical_not(bounded_ok))
    def _():
        def scores(h):
            q = q_head(h, lambda _, q_bound: jnp.zeros_like(q_bound))
            return [score_tile(h, r, q) for r in range(A_NCHUNK)]

        def finish(h, s_list):
            m = jnp.max(functools.reduce(jnp.maximum, s_list), axis=0, keepdims=True)
            acc = None
            for r in range(A_NCHUNK):
                part = pv_tile(h, r, jnp.exp2(s_list[r] - m).astype(BF16), True)
                acc = part if acc is None else acc + part
            store(h, acc[0:64], acc[64:65])

        s_next = scores(0)
        for h in range(A_HEADS):
            s_cur = s_next
            if h + 1 < A_HEADS:
                s_next = scores(h + 1)
            finish(h, s_cur)


def _dil_scalars(qn, kn, b_max, b_min):
    k_bound = K_MARGIN * jnp.max(kn, axis=-1)
    q_max = jnp.max(qn, axis=-1)
    spread = 2.0 * q_max * k_bound + (b_max - b_min)[None, :]
    ok = jnp.all(spread <= 2.0 * BOUND_LIMIT, axis=-1, keepdims=True).astype(F32)
    pad = jnp.zeros((BATCH, A_SCAL - 1 - 2 * A_HEADS), F32)
    return jnp.concatenate([ok, k_bound, jnp.broadcast_to(b_max[None, :], (BATCH, A_HEADS)), pad], axis=1).reshape(-1)


def _dil_attn_call(scal, qT, k, vT, bias):
    t = A_TILE
    return pl.pallas_call(
        _dil_attn_kernel,
        out_shape=jax.ShapeDtypeStruct((BATCH, A_HEADS * 64, SEQ), BF16),
        grid=(BATCH, SEQ // t),
        in_specs=[
            pl.BlockSpec(memory_space=pltpu.SMEM),
            pl.BlockSpec((1, A_HEADS * 128, t), lambda b, i: (b, 0, i)),
            pl.BlockSpec((1, A_HEADS, SEQ, 128), lambda b, i: (b, 0, 0, 0)),
            pl.BlockSpec((1, SEQ // t, A_HEADS * 64, t), lambda b, i: (b, 0, 0, 0)),
            pl.BlockSpec((A_HEADS, A_NCHUNK + 1, t, t), lambda b, i: (0, 0, 0, 0),
                         pipeline_mode=pl.Buffered(1)),
        ],
        out_specs=pl.BlockSpec((1, A_HEADS * 64, t), lambda b, i: (b, 0, i)),
        compiler_params=pltpu.CompilerParams(vmem_limit_bytes=VMEM_LIMIT),
        name="dilated_attn",
    )(scal, qT, k, vT, bias)


def _layer_norm(z, g, b):
    mu = jnp.mean(z, axis=-1, keepdims=True)
    d = z - mu
    var = jnp.mean(d * d, axis=-1, keepdims=True)
    return d * lax.rsqrt(var + EPS) * g + b


def _mlp_kernel(x_ref, mod_ref, aoT_ref, boT_ref, coT_ref, wo_ref, ln1g_ref, ln1b_ref,
                w1_ref, w2_ref, ln2g_ref, ln2b_ref, o_ref):
    g_a = mod_ref[0, 2:3, :]
    sh_m = mod_ref[0, 3:4, :]
    sc_m = mod_ref[0, 4:5, :]
    g_m = mod_ref[0, 5:6, :]
    tm = x_ref.shape[0]
    hr = tm // MLP_PARTS
    fc = 1024
    nfc = D_FF // fc

    def out_proj(part):
        cols = slice(hr * part, hr * (part + 1))
        catT = jnp.concatenate([aoT_ref[0, :, cols], boT_ref[0, :, cols], coT_ref[0, :, cols]], axis=0)
        cat = catT.astype(F32).T.astype(BF16)
        return jnp.dot(cat, wo_ref[0], preferred_element_type=F32)

    ys = [out_proj(part) for part in range(MLP_PARTS)]
    prev_out = None
    for part in range(MLP_PARTS):
        rows = slice(hr * part, hr * (part + 1))
        x1 = _layer_norm(ALPHA * x_ref[rows, :] + (1.0 + g_a) * ys[part], ln1g_ref[...], ln1b_ref[...])
        h = (x1 * (1.0 + sc_m) + sh_m).astype(BF16)
        y2 = None
        for j in range(nfc):
            u = jnp.dot(h, w1_ref[0, :, fc * j:fc * (j + 1)], preferred_element_type=F32)
            u = jnp.maximum(u, 0.0)
            u = (u * u).astype(BF16)
            if prev_out is not None and j == nfc - 1:
                folded = _sublane_partial_sum(prev_out)
                folded = functools.reduce(lambda a, b: a + b,
                                          [folded[:, LANES * i:LANES * (i + 1)] for i in range(D_MODEL // LANES)])
                zero = _exact_zero_of(jnp.concatenate([folded, folded], axis=0))
                top = jnp.concatenate([u[0:16, 0:LANES] + zero, u[0:16, LANES:]], axis=1)
                u = jnp.concatenate([top, u[16:]], axis=0)
            term = jnp.dot(u, w2_ref[0, fc * j:fc * (j + 1), :], preferred_element_type=F32)
            y2 = term if y2 is None else y2 + term
        prev_out = _layer_norm(ALPHA * x1 + (1.0 + g_m) * y2, ln2g_ref[...], ln2b_ref[...])
        o_ref[rows, :] = prev_out


def _mlp_call(x2d, mod, aoT, boT, coT, layer, wo_all, ln1g, ln1b, w1_all, w2_all, ln2g, ln2b):
    tm = MLP_TILE
    T = x2d.shape[0]
    nj = SEQ // tm
    const2 = lambda i: (0, 0)
    bj3 = lambda i: (i // nj, 0, i % nj)
    one = pl.Buffered(1)
    in_specs = [
        pl.BlockSpec((tm, D_MODEL), lambda i: (i, 0)),
        pl.BlockSpec((1, 6, D_MODEL), lambda i: (i // nj, 0, 0)),
        pl.BlockSpec((1, 256, tm), bj3),
        pl.BlockSpec((1, 256, tm), bj3),
        pl.BlockSpec((1, 512, tm), bj3),
        pl.BlockSpec((1, D_MODEL, D_MODEL), lambda i: (layer, 0, 0), pipeline_mode=one),
        pl.BlockSpec((1, D_MODEL), const2),
        pl.BlockSpec((1, D_MODEL), const2),
        pl.BlockSpec((1, D_MODEL, D_FF), lambda i: (layer, 0, 0), pipeline_mode=one),
        pl.BlockSpec((1, D_FF, D_MODEL), lambda i: (layer, 0, 0), pipeline_mode=one),
        pl.BlockSpec((1, D_MODEL), const2),
        pl.BlockSpec((1, D_MODEL), const2),
    ]
    return pl.pallas_call(
        _mlp_kernel,
        out_shape=jax.ShapeDtypeStruct((T, D_MODEL), F32),
        grid=(T // tm,),
        in_specs=in_specs,
        out_specs=pl.BlockSpec((tm, D_MODEL), lambda i: (i, 0)),
        compiler_params=pltpu.CompilerParams(vmem_limit_bytes=VMEM_LIMIT),
        name="out_mlp",
    )(x2d, mod, aoT, boT, coT, wo_all, ln1g, ln1b, w1_all, w2_all, ln2g, ln2b)


def _rope_angles(pos, dim):
    inv = ROPE_THETA ** (-jnp.arange(0, dim, 2, dtype=F32) / dim)
    return pos.astype(F32)[:, None] * inv[None, :]


def _t5_bucket(rel):
    nb = REL_BUCKETS // 2
    max_exact = nb // 2
    sign = jnp.where(rel > 0, nb, 0)
    n = jnp.abs(rel)
    nf = jnp.maximum(n, 1).astype(F32)
    large = max_exact + (jnp.log(nf / max_exact) / math.log(REL_MAX_DIST / max_exact)
                         * (nb - max_exact)).astype(jnp.int32)
    large = jnp.minimum(large, nb - 1)
    return sign + jnp.where(n < max_exact, n, large)


def _dilated_bias_tiles(rel_bias):
    t = A_TILE
    span = (A_CENTER + 1) * t
    deltas = np.arange(-span + 1, span)
    mult = np.zeros(deltas.shape, np.int32)
    for (w, d) in A_PATTERNS:
        half = w // (2 * d)
        mult += ((deltas % d == 0) & (np.abs(deltas) <= half * d)).astype(np.int32)
    bucket = _t5_bucket(jnp.asarray(deltas, jnp.int32))
    logm = jnp.asarray(np.log(np.maximum(mult, 1)), F32)
    tab = jnp.where(jnp.asarray(mult > 0)[:, None],
                    (rel_bias[bucket] + logm[:, None]) * LOG2E, NEG_BIG)
    valid = jnp.asarray(mult > 0)[:, None]
    b_max = jnp.max(jnp.where(valid, tab, -jnp.inf), axis=0)
    b_min = jnp.min(jnp.where(valid, tab, jnp.inf), axis=0)
    x = np.arange(2 * t)
    d = np.where(x <= t - 1, -x, 2 * t - x)
    d[t] = 0
    idx = (np.arange(A_NCHUNK)[:, None] - A_CENTER) * t + d[None, :] + span - 1
    u = jnp.moveaxis(tab[jnp.asarray(idx, jnp.int32)], -1, 0)
    u = jnp.concatenate([u, jnp.full((A_HEADS, 1, 2 * t), NEG_BIG, F32)], axis=1)
    return _toeplitz_call(u.reshape(A_HEADS * (A_NCHUNK + 1), 1, 2 * t)), b_max, b_min


def _toeplitz_kernel(u_ref, o_ref):
    t = A_TILE
    for c in range(u_ref.shape[0]):
        rows = jnp.broadcast_to(u_ref[c], (t, 2 * t))
        o_ref[c] = pltpu.roll(rows, 0, 1, stride=1, stride_axis=0)[:, :t]


def _toeplitz_call(u):
    t = A_TILE
    n = u.shape[0]
    per = A_NCHUNK + 1
    tiles = pl.pallas_call(
        _toeplitz_kernel,
        out_shape=jax.ShapeDtypeStruct((n, t, t), F32),
        grid=(n // per,),
        in_specs=[pl.BlockSpec((per, 1, 2 * t), lambda i: (i, 0, 0))],
        out_specs=pl.BlockSpec((per, t, t), lambda i: (i, 0, 0)),
        name="bias_tiles",
    )(u)
    return tiles.reshape(A_HEADS, A_NCHUNK + 1, t, t)


def _rope_tables(gq, gk):
    t = jnp.arange(SEQ)
    ang_t = _rope_angles(t, B_ROPE)
    ang_row = _rope_angles(t // GRID_W, HEAD_DIM // 2)
    ang_col = _rope_angles(t % GRID_W, HEAD_DIM // 2)
    cos64 = jnp.concatenate([jnp.cos(ang_row)] * 2 + [jnp.cos(ang_col)] * 2, axis=1)
    sin_r, sin_c = jnp.sin(ang_row), jnp.sin(ang_col)
    sin64 = jnp.concatenate([-sin_r, sin_r, -sin_c, sin_c], axis=1)
    perm = np.concatenate([np.arange(16, 32), np.arange(0, 16), np.arange(48, 64), np.arange(32, 48)])

    def with_gain(g, scale):
        return (cos64 * g[None, :] * scale).T, (sin64 * g[perm][None, :] * scale).T

    qc, qs = with_gain(gq, HEAD_DIM ** -0.5 * LOG2E)
    kc, ks = with_gain(gk, 1.0)
    q_bound, _ = _gqa_bounds(gq, gk)
    bound_rows = jnp.zeros((16, SEQ), F32).at[0].set(-q_bound)
    return jnp.concatenate([jnp.cos(ang_t).T, jnp.sin(ang_t).T, qc, qs, kc, ks, bound_rows], axis=0)


def _gqa_bounds(gq, gk):
    root = math.sqrt(HEAD_DIM)
    q_bound = Q_MARGIN * root * (HEAD_DIM ** -0.5 * LOG2E) * jnp.max(jnp.abs(gq))
    k_bound = root * jnp.max(jnp.abs(gk))
    return q_bound, k_bound


def kernel(x, c, w_ada, b_ada, w_in, mla_q_norm, mla_w_uq, mla_kv_norm, mla_w_ukv,
           gqa_q_norm, gqa_k_norm, rel_bias, w_o, ln1_g, ln1_b, w1, w2, ln2_g, ln2_b):
    B, S, D = x.shape
    c_pad = jnp.zeros((8, D), F32).at[:B].set(c)
    mod_all = _ada_call(c_pad, w_ada, b_ada)[:, :B].reshape(DEPTH, B, 6, D)
    bias, b_max, b_min = _dilated_bias_tiles(rel_bias)

    winT_all = jnp.swapaxes(w_in, 1, 2).astype(BF16)
    wo_all, w1_all, w2_all = w_o.astype(BF16), w1.astype(BF16), w2.astype(BF16)
    x2d = x.reshape(B * S, D)
    for l in range(DEPTH):
        wuqT = mla_w_uq[l].T
        wukv = mla_w_ukv[l].reshape(B_KV_RANK, B_HEADS, B_NOPE + B_V)
        wukT = wukv[:, :, :B_NOPE].reshape(B_KV_RANK, B_HEADS * B_NOPE).T
        wuvT = wukv[:, :, B_NOPE:].reshape(B_KV_RANK, B_HEADS * B_V).T
        tab = _rope_tables(gqa_q_norm[l], gqa_k_norm[l])
        aqT, ak, avT, bqT, bk, bvT, cqT, ck, cvT, qn, kn = _proj_call(
            x2d, mod_all[l], winT_all, l, wuqT, mla_q_norm[l][None, :], wukT, wuvT,
            mla_kv_norm[l][None, :], tab)
        a_scal = _dil_scalars(qn[:, QN_A:QN_A + A_HEADS], kn[:, KN_A:KN_A + A_HEADS], b_max, b_min)
        aoT = _dil_attn_call(a_scal, aqT, ak, avT, bias)
        cg = C_Q_HEADS // C_KV_HEADS
        b_scal = _bound_scalars(qn[:, :B_HEADS], kn[:, :B_HEADS], 1)
        cq_bound, ck_bound = _gqa_bounds(gqa_q_norm[l], gqa_k_norm[l])
        c_scal = _bound_scalars(jnp.full((B, C_Q_HEADS, 1), cq_bound), jnp.full((B, C_KV_HEADS, 1), ck_bound), cg)
        boT = _full_attn_call(b_scal, bqT, bk, bvT, groups=1, tq=4096, bound_row=B_NOPE + B_ROPE,
                              name="mla_attn")
        coT = _full_attn_call(c_scal, cqT, ck, cvT, groups=cg, tq=1024, bound_row=HEAD_DIM, name="gqa_attn")
        x2d = _mlp_call(x2d, mod_all[l], aoT, boT, coT, l, wo_all,
                        ln1_g[l][None, :], ln1_b[l][None, :], w1_all, w2_all,
                        ln2_g[l][None, :], ln2_b[l][None, :])
    return x2d.reshape(B, S, D)
```

```python
import functools
import math

import numpy as np
import jax
import jax.numpy as jnp
from jax import lax
from jax.experimental import pallas as pl
from jax.experimental.pallas import tpu as pltpu

D_MODEL = 1024
BATCH = 4
SEQ = 4096
DEPTH = 2
HEAD_DIM = 64
A_HEADS = 4
A_PATTERNS = ((128, 1), (512, 4), (2048, 16))
B_HEADS = 4
B_Q_RANK = 384
B_KV_RANK = 256
B_NOPE = 64
B_ROPE = 32
B_V = 64
C_Q_HEADS = 8
C_KV_HEADS = 2
GRID_W = 64
ROPE_THETA = 10000.0
REL_BUCKETS = 32
REL_MAX_DIST = 1024
D_FF = 4 * D_MODEL
EPS = 1e-6
NEG_BIG = -1e30
ALPHA = (2 * DEPTH) ** 0.25
A_W = A_HEADS * HEAD_DIM
IN_COLS = 2208
LOG2E = 1.4426950408889634

F32 = jnp.float32
BF16 = jnp.bfloat16

LANES = 128
VMEM_LIMIT = 56 * 1024 * 1024

OFF_AQ, OFF_AK, OFF_AV = 0, 256, 512
OFF_BCQ, OFF_BCKV, OFF_BKR = 768, 1152, 1408
OFF_CQ, OFF_CK, OFF_CV, OFF_END = 1440, 1952, 2080, 2208

TOK_TILE = 1024
MLP_TILE = 1024
MLP_PARTS = 4
A_TILE = 256
A_NCHUNK = 9
A_CENTER = 4
KV_TILE = 512
ROW_BLK = 256
QN_A = 12
KN_A = 8
A_SCAL = 12
A_PV_LAG = 4
PV_LAG = 2
RING = 3
STRIP = 512
TAB_CBOUND = 288
TAB_ROWS = 304
TINY = 1e-30
Q_MARGIN = 1.02
K_MARGIN = 1.01
BOUND_LIMIT = 50.0


def _exact_zero_of(v):
    bits = lax.bitcast_convert_type(v, jnp.uint32)
    return ((bits >> 16) >> 16).astype(F32).astype(BF16)


def _nt_dot(a, b):
    return lax.dot_general(a, b, (((1,), (1,)), ((), ())), preferred_element_type=F32)


def _ada_kernel(c_ref, w_ref, b_ref, o_ref):
    c = c_ref[...]
    ca = (c / (1.0 + jnp.exp(-c))).astype(BF16)
    w = w_ref[0].astype(BF16)
    o_ref[0] = jnp.dot(ca, w, preferred_element_type=F32) + b_ref[0]


def _ada_call(c_pad, w_ada, b_ada):
    tn = 1536
    n = 6 * D_MODEL
    return pl.pallas_call(
        _ada_kernel,
        out_shape=jax.ShapeDtypeStruct((DEPTH, 8, n), F32),
        grid=(DEPTH, n // tn),
        in_specs=[
            pl.BlockSpec((8, D_MODEL), lambda l, j: (0, 0)),
            pl.BlockSpec((1, D_MODEL, tn), lambda l, j: (l, 0, j)),
            pl.BlockSpec((1, 1, tn), lambda l, j: (l, 0, j)),
        ],
        out_specs=pl.BlockSpec((1, 8, tn), lambda l, j: (l, 0, j)),
        compiler_params=pltpu.CompilerParams(vmem_limit_bytes=VMEM_LIMIT),
        name="ada_mod",
    )(c_pad, w_ada, b_ada.reshape(DEPTH, 1, n))


def _proj_kernel(x_ref, mod_ref, winT_ref, wuqT_ref, gq_ref, wukT_ref, wuvT_ref, gkv_ref, tab_ref,
                 aqT_ref, ak_ref, avT_ref, bqT_ref, bk_ref, bvT_ref, cqT_ref, ck_ref, cvT_ref,
                 qn_ref, kn_ref):
    tm = x_ref.shape[0]
    sh = mod_ref[0, 0:1, :]
    sc = mod_ref[0, 1:2, :]
    h = (x_ref[...] * (1.0 + sc) + sh).astype(BF16)

    proj_t = _nt_dot(winT_ref[0], h)

    def seg(r0, r1):
        return proj_t[r0:r1]

    def first_row(rows, val):
        return jnp.where(lax.broadcasted_iota(jnp.int32, (rows, tm), 0) == 0, val, 0.0)

    def norm(xh):
        n2 = _sum_sq_rows(xh)
        return n2 * lax.rsqrt(n2 + TINY)

    one_row32 = first_row(32, 1.0)
    one_row64 = first_row(64, 1.0)
    qn_ref[0] = jnp.zeros(qn_ref.shape[1:], F32)
    kn_ref[0] = jnp.zeros(kn_ref.shape[1:], F32)

    aq = seg(OFF_AQ, OFF_AK) * (HEAD_DIM ** -0.5 * LOG2E)
    for hd in range(A_HEADS):
        qh = aq[64 * hd:64 * hd + 64]
        qb = Q_MARGIN * norm(qh)
        qn_ref[0, QN_A + hd:QN_A + hd + 1, :] = qb
        aqT_ref[0, 128 * hd:128 * hd + 64, :] = qh.astype(BF16)
        aqT_ref[0, 128 * hd + 64:128 * hd + 128, :] = first_row(64, -qb).astype(BF16)
    ak = seg(OFF_AK, OFF_AV)
    for hd in range(A_HEADS):
        kh = ak[64 * hd:64 * hd + 64]
        kn_ref[0, KN_A + hd:KN_A + hd + 1, :] = norm(kh)
        kt = jnp.concatenate([kh, one_row64], axis=0)
        ak_ref[0, hd] = kt.T.astype(BF16)
    av = seg(OFF_AV, OFF_BCQ).astype(BF16)
    for j in range(tm // A_TILE):
        avT_ref[0, j] = av[:, A_TILE * j:A_TILE * (j + 1)]

    def normrope(xh, c, s):
        r = lax.rsqrt(_sum_sq_rows(xh) * (1.0 / xh.shape[0]) + EPS)
        rot = jnp.concatenate([xh[16:32], xh[0:16], xh[48:64], xh[32:48]], axis=0)
        return (xh * c + rot * s) * r

    cqc = tab_ref[32:96, :]
    cqs = tab_ref[96:160, :]
    c_bound_rows = tab_ref[TAB_CBOUND:TAB_CBOUND + 16, :].astype(BF16)
    cq2 = seg(OFF_CQ, OFF_CK)
    for hd in range(C_Q_HEADS):
        qh = normrope(cq2[64 * hd:64 * hd + 64], cqc, cqs)
        cqT_ref[0, 128 * hd:128 * hd + 64, :] = qh.astype(BF16)
        cqT_ref[0, 128 * hd + 64:128 * hd + 80, :] = c_bound_rows
        cqT_ref[0, 128 * hd + 80:128 * hd + 128, :] = jnp.zeros((48, tm), BF16)
    ckc = tab_ref[160:224, :]
    cks = tab_ref[224:288, :]
    ck2 = seg(OFF_CK, OFF_CV)
    for hd in range(C_KV_HEADS):
        kh = normrope(ck2[64 * hd:64 * hd + 64], ckc, cks)
        kt = jnp.concatenate([kh, one_row64], axis=0)
        ck_ref[0, hd] = kt.T.astype(BF16)
    cv = seg(OFF_CV, OFF_END).astype(BF16)
    for j in range(tm // KV_TILE):
        cvT_ref[0, j] = cv[:, KV_TILE * j:KV_TILE * (j + 1)]

    cos_t = tab_ref[0:16, :]
    sin_t = tab_ref[16:32, :]
    cq = seg(OFF_BCQ, OFF_BCKV)
    rq = lax.rsqrt(_sum_sq_rows(cq) * (1.0 / B_Q_RANK) + EPS)
    wq = (wuqT_ref[...] * gq_ref[...]).astype(BF16)
    uq = jnp.dot(wq, cq.astype(BF16), preferred_element_type=F32)
    uq = uq * (rq * ((B_NOPE + B_ROPE) ** -0.5 * LOG2E))
    for hd in range(B_HEADS):
        b0 = (B_NOPE + B_ROPE) * hd
        x1 = uq[b0 + 64:b0 + 80]
        x2 = uq[b0 + 80:b0 + 96]
        qb = Q_MARGIN * norm(uq[b0:b0 + 96])
        qn_ref[0, hd:hd + 1, :] = qb
        bqT_ref[0, 128 * hd:128 * hd + 64, :] = uq[b0:b0 + 64].astype(BF16)
        bqT_ref[0, 128 * hd + 64:128 * hd + 80, :] = (x1 * cos_t - x2 * sin_t).astype(BF16)
        bqT_ref[0, 128 * hd + 80:128 * hd + 96, :] = (x1 * sin_t + x2 * cos_t).astype(BF16)
        bqT_ref[0, 128 * hd + 96:128 * hd + 128, :] = first_row(32, -qb).astype(BF16)
    ckv = seg(OFF_BCKV, OFF_BKR)
    rkv = lax.rsqrt(_sum_sq_rows(ckv) * (1.0 / B_KV_RANK) + EPS)
    ckv_b = ckv.astype(BF16)
    gkv = gkv_ref[...]
    kn = jnp.dot((wukT_ref[...] * gkv).astype(BF16), ckv_b, preferred_element_type=F32) * rkv
    vv = jnp.dot((wuvT_ref[...] * gkv).astype(BF16), ckv_b, preferred_element_type=F32) * rkv
    vv = vv.astype(BF16)
    for j in range(tm // KV_TILE):
        bvT_ref[0, j] = vv[:, KV_TILE * j:KV_TILE * (j + 1)]
    kr = seg(OFF_BKR, OFF_CQ)
    kr1 = kr[0:16]
    kr2 = kr[16:32]
    kro = jnp.concatenate([kr1 * cos_t - kr2 * sin_t, kr1 * sin_t + kr2 * cos_t,
                           one_row32], axis=0)
    kr_sq = _sum_sq_rows(kr)
    for hd in range(B_HEADS):
        knh = kn[64 * hd:64 * hd + 64]
        k_sq = _sum_sq_rows(knh) + kr_sq
        kn_ref[0, hd:hd + 1, :] = k_sq * lax.rsqrt(k_sq + TINY)
        kt = jnp.concatenate([knh, kro], axis=0)
        bk_ref[0, hd] = kt.T.astype(BF16)


def _proj_call(x2d, mod, winT_all, layer, wuqT, gq, wukT, wuvT, gkv, tab):
    tm = TOK_TILE
    T = x2d.shape[0]
    nj = SEQ // tm
    const2 = lambda i: (0, 0)
    bj3 = lambda i: (i // nj, 0, i % nj)
    bj4 = lambda i: (i // nj, 0, i % nj, 0)
    cj4 = lambda i: (i // nj, i % nj, 0, 0)
    out_shape = (
        jax.ShapeDtypeStruct((BATCH, 512, SEQ), BF16),
        jax.ShapeDtypeStruct((BATCH, A_HEADS, SEQ, 128), BF16),
        jax.ShapeDtypeStruct((BATCH, SEQ // A_TILE, 256, A_TILE), BF16),
        jax.ShapeDtypeStruct((BATCH, 512, SEQ), BF16),
        jax.ShapeDtypeStruct((BATCH, B_HEADS, SEQ, 128), BF16),
        jax.ShapeDtypeStruct((BATCH, SEQ // KV_TILE, 256, KV_TILE), BF16),
        jax.ShapeDtypeStruct((BATCH, 1024, SEQ), BF16),
        jax.ShapeDtypeStruct((BATCH, C_KV_HEADS, SEQ, 128), BF16),
        jax.ShapeDtypeStruct((BATCH, SEQ // KV_TILE, 128, KV_TILE), BF16),
        jax.ShapeDtypeStruct((BATCH, 16, SEQ), F32),
        jax.ShapeDtypeStruct((BATCH, 16, SEQ), F32),
    )
    out_specs = (
        pl.BlockSpec((1, 512, tm), bj3),
        pl.BlockSpec((1, A_HEADS, tm, 128), bj4),
        pl.BlockSpec((1, tm // A_TILE, 256, A_TILE), cj4),
        pl.BlockSpec((1, 512, tm), bj3),
        pl.BlockSpec((1, B_HEADS, tm, 128), bj4),
        pl.BlockSpec((1, tm // KV_TILE, 256, KV_TILE), cj4),
        pl.BlockSpec((1, 1024, tm), bj3),
        pl.BlockSpec((1, C_KV_HEADS, tm, 128), bj4),
        pl.BlockSpec((1, tm // KV_TILE, 128, KV_TILE), cj4),
        pl.BlockSpec((1, 16, tm), bj3),
        pl.BlockSpec((1, 16, tm), bj3),
    )
    in_specs = [
        pl.BlockSpec((tm, D_MODEL), lambda i: (i, 0)),
        pl.BlockSpec((1, 6, D_MODEL), lambda i: (i // nj, 0, 0)),
        pl.BlockSpec((1, IN_COLS, D_MODEL), lambda i: (layer, 0, 0)),
        pl.BlockSpec((B_Q_RANK, B_Q_RANK), const2),
        pl.BlockSpec((1, B_Q_RANK), const2),
        pl.BlockSpec((256, B_KV_RANK), const2),
        pl.BlockSpec((256, B_KV_RANK), const2),
        pl.BlockSpec((1, B_KV_RANK), const2),
        pl.BlockSpec((TAB_ROWS, tm), lambda i: (0, i % nj)),
    ]
    return pl.pallas_call(
        _proj_kernel,
        out_shape=out_shape,
        grid=(T // tm,),
        in_specs=in_specs,
        out_specs=out_specs,
        compiler_params=pltpu.CompilerParams(vmem_limit_bytes=VMEM_LIMIT),
        name="in_proj",
    )(x2d, mod, winT_all, wuqT, gq, wukT, wuvT, gkv, tab)


def _with_ones_rows(v):
    return jnp.concatenate([v, jnp.ones((16, v.shape[1]), BF16)], axis=0)


def _sublane_partial_sum(p):
    return functools.reduce(lambda a, b: a + b, [p[8 * i:8 * (i + 1)] for i in range(p.shape[0] // 8)])


def _sum_sq_rows(x):
    return jnp.sum(_sublane_partial_sum(x * x), axis=0, keepdims=True)


def _attn_finish(o_ref, num, den, groups, tq):
    o = num / den
    for g in range(groups):
        o_ref[0, 64 * g:64 * g + 64, :] = o[:, tq * g:tq * (g + 1)].astype(o_ref.dtype)


def _attn_online(q_strips, k_ref, vT_ref, o_ref, acc_ref, s_ref, p_ref, *, groups, tq, tk):
    nstrip = len(q_strips)
    w = q_strips[0].shape[1]
    nchunk = SEQ // tk
    rb = ROW_BLK
    nb = tk // rb
    nunit = nchunk * nstrip
    acc_ref[...] = jnp.zeros(acc_ref.shape, F32)
    m = [jnp.full((1, w), NEG_BIG, F32) for _ in range(nstrip)]

    def qk_block(u, b):
        c, j = divmod(u, nstrip)
        k = k_ref[0, 0, c * tk + rb * b:c * tk + rb * (b + 1), :]
        return jnp.dot(k, q_strips[j], preferred_element_type=F32)

    def pv_block(u, b, p_blk):
        c = u // nstrip
        v_aug = _with_ones_rows(vT_ref[0, c, :, rb * b:rb * (b + 1)])
        return jnp.dot(v_aug, p_blk, preferred_element_type=F32)

    alphas = {}
    for t in range(nunit + 2):
        pv = None
        for b in range(nb):
            rows = slice(rb * b, rb * (b + 1))
            if t < nunit:
                s_ref[t % RING, rows, :] = qk_block(t, b)
            if t >= 2:
                part = pv_block(t - 2, b, p_ref[(t - 2) % RING, rows, :])
                pv = part if pv is None else pv + part
        if 1 <= t <= nunit:
            u = t - 1
            j = u % nstrip
            s_u = s_ref[u % RING]
            m_new = jnp.maximum(m[j], jnp.max(s_u, axis=0, keepdims=True))
            alphas[u] = jnp.exp2(m[j] - m_new)
            p_ref[u % RING] = jnp.exp2(s_u - m_new).astype(BF16)
            m[j] = m_new
        if t >= 2:
            u = t - 2
            sl = slice(w * (u % nstrip), w * (u % nstrip + 1))
            acc_ref[:, sl] = alphas.pop(u) * acc_ref[:, sl] + pv
    _attn_finish(o_ref, acc_ref[0:64, :], acc_ref[64:65, :], groups, tq)


def _attn_bounded(q_strips, k_ref, vT_ref, o_ref, *, groups, tq, tk):
    nstrip = len(q_strips)
    rb = ROW_BLK
    nkb = SEQ // rb
    units = [(kb, j) for kb in range(nkb) for j in range(nstrip)]
    acc = [None] * nstrip
    den = [None] * nstrip
    probs = {}

    def pv(u):
        kb, j = units[u]
        c, b = divmod(kb, tk // rb)
        part = jnp.dot(vT_ref[0, c, :, rb * b:rb * (b + 1)], probs.pop(u), preferred_element_type=F32)
        acc[j] = part if acc[j] is None else acc[j] + part

    for t in range(len(units)):
        kb, j = units[t]
        s = jnp.dot(k_ref[0, 0, rb * kb:rb * (kb + 1), :], q_strips[j], preferred_element_type=F32)
        p = jnp.exp2(s)
        part = _sublane_partial_sum(p)
        den[j] = part if den[j] is None else den[j] + part
        probs[t] = p.astype(BF16)
        if t >= PV_LAG:
            pv(t - PV_LAG)
    for u in range(len(units) - PV_LAG, len(units)):
        pv(u)
    den = [jnp.sum(d, axis=0, keepdims=True) for d in den]
    _attn_finish(o_ref, jnp.concatenate(acc, axis=1), jnp.concatenate(den, axis=1), groups, tq)


def _full_attn_kernel(scal_ref, qT_ref, k_ref, vT_ref, o_ref, acc_ref, s_ref, p_ref,
                      *, groups, tq, tk, bound_row):
    n = groups * tq
    w = min(n, STRIP)
    nstrip = n // w
    pair = 2 * (pl.program_id(0) * pl.num_programs(1) + pl.program_id(1))
    k_bound = scal_ref[pair]
    bounded_ok = scal_ref[pair + 1] > 0.5
    cols = [qT_ref[0, 128 * g:128 * (g + 1), :] for g in range(groups)]
    q_cat = cols[0] if groups == 1 else jnp.concatenate(cols, axis=1)
    q_lo = q_cat[0:bound_row]
    q_bound = q_cat[bound_row:bound_row + 16]
    q_hi = q_cat[bound_row + 16:]

    def strips(mid):
        q = jnp.concatenate([q_lo, mid, q_hi], axis=0)
        return [q[:, w * j:w * (j + 1)] for j in range(nstrip)]

    @pl.when(bounded_ok)
    def _():
        mid = (q_bound.astype(F32) * k_bound).astype(BF16)
        _attn_bounded(strips(mid), k_ref, vT_ref, o_ref, groups=groups, tq=tq, tk=tk)

    @pl.when(jnp.logical_not(bounded_ok))
    def _():
        _attn_online(strips(jnp.zeros_like(q_bound)), k_ref, vT_ref, o_ref, acc_ref, s_ref, p_ref,
                     groups=groups, tq=tq, tk=tk)


def _bound_scalars(qn, kn, groups):
    k_bound = K_MARGIN * jnp.max(kn, axis=-1)
    q_max = jnp.max(qn, axis=-1).reshape(BATCH, kn.shape[1], groups).max(-1)
    ok = (q_max * k_bound <= BOUND_LIMIT).astype(F32)
    return jnp.stack([k_bound, ok], axis=-1).reshape(-1)


def _full_attn_call(scal, qT, k, vT, *, groups, tq, bound_row, name):
    tk = KV_TILE
    hkv = k.shape[1]
    n = groups * tq
    kern = functools.partial(_full_attn_kernel, groups=groups, tq=tq, tk=tk, bound_row=bound_row)
    return pl.pallas_call(
        kern,
        out_shape=jax.ShapeDtypeStruct((BATCH, hkv * groups * 64, SEQ), BF16),
        grid=(BATCH, hkv, SEQ // tq),
        in_specs=[
            pl.BlockSpec(memory_space=pltpu.SMEM),
            pl.BlockSpec((1, groups * 128, tq), lambda b, g, i: (b, g, i)),
            pl.BlockSpec((1, 1, SEQ, 128), lambda b, g, i: (b, g, 0, 0)),
            pl.BlockSpec((1, SEQ // tk, 64, tk), lambda b, g, i: (b, 0, g, 0)),
        ],
        out_specs=pl.BlockSpec((1, groups * 64, tq), lambda b, g, i: (b, g, i)),
        scratch_shapes=[pltpu.VMEM((80, n), F32),
                        pltpu.VMEM((RING, tk, min(n, STRIP)), F32),
                        pltpu.VMEM((RING, tk, min(n, STRIP)), BF16)],
        compiler_params=pltpu.CompilerParams(vmem_limit_bytes=VMEM_LIMIT),
        name=name,
    )(scal, qT, k, vT)


def _dil_attn_kernel(scal_ref, qT_ref, k_ref, vT_ref, bias_ref, o_ref):
    t = A_TILE
    nq = SEQ // t
    i = pl.program_id(1)
    first = jnp.clip(i - A_CENTER, 0, nq - A_NCHUNK)
    base = A_SCAL * pl.program_id(0)
    bounded_ok = scal_ref[base] > 0.5
    top_row = lax.broadcasted_iota(jnp.int32, (16, t), 0) == 0

    def q_head(h, mid_fn):
        q = qT_ref[0, 128 * h:128 * (h + 1), :]
        return jnp.concatenate([q[0:HEAD_DIM], mid_fn(h, q[HEAD_DIM:HEAD_DIM + 16]), q[HEAD_DIM + 16:]], axis=0)

    def score_tile(h, r, q):
        kt = first + r
        d = kt - i + A_CENTER
        bi = jnp.where((d >= 0) & (d < A_NCHUNK), d, A_NCHUNK)
        k = k_ref[0, h, pl.ds(pl.multiple_of(kt * t, t), t), :]
        return jnp.dot(k, q, preferred_element_type=F32) + bias_ref[h, bi]

    def pv_tile(h, r, p, ones_rows):
        v = vT_ref[0, first + r, 64 * h:64 * (h + 1), :]
        return jnp.dot(_with_ones_rows(v) if ones_rows else v, p, preferred_element_type=F32)

    def store(h, num, den):
        o_ref[0, 64 * h:64 * (h + 1), :] = (num / den).astype(o_ref.dtype)

    @pl.when(bounded_ok)
    def _():
        def mid(h, q_bound):
            shift = q_bound.astype(F32) * scal_ref[base + 1 + h]
            return (shift - jnp.where(top_row, scal_ref[base + 1 + A_HEADS + h], 0.0)).astype(BF16)

        qs = [q_head(h, mid) for h in range(A_HEADS)]
        units = [(h, r) for r in range(A_NCHUNK) for h in range(A_HEADS)]
        acc = [None] * A_HEADS
        den = [None] * A_HEADS
        probs = {}

        def pv(u):
            h, r = units[u]
            part = pv_tile(h, r, probs.pop(u), False)
            acc[h] = part if acc[h] is None else acc[h] + part

        for u, (h, r) in enumerate(units):
            p = jnp.exp2(score_tile(h, r, qs[h]))
            part = _sublane_partial_sum(p)
            den[h] = part if den[h] is None else den[h] + part
            probs[u] = p.astype(BF16)
            if u >= A_PV_LAG:
                pv(u - A_PV_LAG)
        for u in range(len(units) - A_PV_LAG, len(units)):
            pv(u)
        for h in range(A_HEADS):
            store(h, acc[h], jnp.sum(den[h], axis=0, keepdims=True))

    @pl.when(jnp.logical_not(bounded_ok))
    def _():
        def scores(h):
            q = q_head(h, lambda _, q_bound: jnp.zeros_like(q_bound))
            return [score_tile(h, r, q) for r in range(A_NCHUNK)]

        def finish(h, s_list):
            m = jnp.max(functools.reduce(jnp.maximum, s_list), axis=0, keepdims=True)
            acc = None
            for r in range(A_NCHUNK):
                part = pv_tile(h, r, jnp.exp2(s_list[r] - m).astype(BF16), True)
                acc = part if acc is None else acc + part
            store(h, acc[0:64], acc[64:65])

        s_next = scores(0)
        for h in range(A_HEADS):
            s_cur = s_next
            if h + 1 < A_HEADS:
                s_next = scores(h + 1)
            finish(h, s_cur)


def _dil_scalars(qn, kn, b_max, b_min):
    k_bound = K_MARGIN * jnp.max(kn, axis=-1)
    q_max = jnp.max(qn, axis=-1)
    spread = 2.0 * q_max * k_bound + (b_max - b_min)[None, :]
    ok = jnp.all(spread <= 2.0 * BOUND_LIMIT, axis=-1, keepdims=True).astype(F32)
    pad = jnp.zeros((BATCH, A_SCAL - 1 - 2 * A_HEADS), F32)
    return jnp.concatenate([ok, k_bound, jnp.broadcast_to(b_max[None, :], (BATCH, A_HEADS)), pad], axis=1).reshape(-1)


def _dil_attn_call(scal, qT, k, vT, bias):
    t = A_TILE
    return pl.pallas_call(
        _dil_attn_kernel,
        out_shape=jax.ShapeDtypeStruct((BATCH, A_HEADS * 64, SEQ), BF16),
        grid=(BATCH, SEQ // t),
        in_specs=[
            pl.BlockSpec(memory_space=pltpu.SMEM),
            pl.BlockSpec((1, A_HEADS * 128, t), lambda b, i: (b, 0, i)),
            pl.BlockSpec((1, A_HEADS, SEQ, 128), lambda b, i: (b, 0, 0, 0)),
            pl.BlockSpec((1, SEQ // t, A_HEADS * 64, t), lambda b, i: (b, 0, 0, 0)),
            pl.BlockSpec((A_HEADS, A_NCHUNK + 1, t, t), lambda b, i: (0, 0, 0, 0),
                         pipeline_mode=pl.Buffered(1)),
        ],
        out_specs=pl.BlockSpec((1, A_HEADS * 64, t), lambda b, i: (b, 0, i)),
        compiler_params=pltpu.CompilerParams(vmem_limit_bytes=VMEM_LIMIT),
        name="dilated_attn",
    )(scal, qT, k, vT, bias)


def _layer_norm(z, g, b):
    mu = jnp.mean(z, axis=-1, keepdims=True)
    d = z - mu
    var = jnp.mean(d * d, axis=-1, keepdims=True)
    return d * lax.rsqrt(var + EPS) * g + b


def _mlp_kernel(x_ref, mod_ref, aoT_ref, boT_ref, coT_ref, wo_ref, ln1g_ref, ln1b_ref,
                w1_ref, w2_ref, ln2g_ref, ln2b_ref, o_ref):
    g_a = mod_ref[0, 2:3, :]
    sh_m = mod_ref[0, 3:4, :]
    sc_m = mod_ref[0, 4:5, :]
    g_m = mod_ref[0, 5:6, :]
    tm = x_ref.shape[0]
    hr = tm // MLP_PARTS
    fc = 1024
    nfc = D_FF // fc

    def out_proj(part):
        cols = slice(hr * part, hr * (part + 1))
        catT = jnp.concatenate([aoT_ref[0, :, cols], boT_ref[0, :, cols], coT_ref[0, :, cols]], axis=0)
        cat = catT.astype(F32).T.astype(BF16)
        return jnp.dot(cat, wo_ref[0], preferred_element_type=F32)

    ys = [out_proj(part) for part in range(MLP_PARTS)]
    prev_out = None
    for part in range(MLP_PARTS):
        rows = slice(hr * part, hr * (part + 1))
        x1 = _layer_norm(ALPHA * x_ref[rows, :] + (1.0 + g_a) * ys[part], ln1g_ref[...], ln1b_ref[...])
        h = (x1 * (1.0 + sc_m) + sh_m).astype(BF16)
        y2 = None
        for j in range(nfc):
            u = jnp.dot(h, w1_ref[0, :, fc * j:fc * (j + 1)], preferred_element_type=F32)
            u = jnp.maximum(u, 0.0)
            u = (u * u).astype(BF16)
            if prev_out is not None and j == nfc - 1:
                folded = _sublane_partial_sum(prev_out)
                folded = functools.reduce(lambda a, b: a + b,
                                          [folded[:, LANES * i:LANES * (i + 1)] for i in range(D_MODEL // LANES)])
                zero = _exact_zero_of(jnp.concatenate([folded, folded], axis=0))
                top = jnp.concatenate([u[0:16, 0:LANES] + zero, u[0:16, LANES:]], axis=1)
                u = jnp.concatenate([top, u[16:]], axis=0)
            term = jnp.dot(u, w2_ref[0, fc * j:fc * (j + 1), :], preferred_element_type=F32)
            y2 = term if y2 is None else y2 + term
        prev_out = _layer_norm(ALPHA * x1 + (1.0 + g_m) * y2, ln2g_ref[...], ln2b_ref[...])
        o_ref[rows, :] = prev_out


def _mlp_call(x2d, mod, aoT, boT, coT, layer, wo_all, ln1g, ln1b, w1_all, w2_all, ln2g, ln2b):
    tm = MLP_TILE
    T = x2d.shape[0]
    nj = SEQ // tm
    const2 = lambda i: (0, 0)
    bj3 = lambda i: (i // nj, 0, i % nj)
    one = pl.Buffered(1)
    in_specs = [
        pl.BlockSpec((tm, D_MODEL), lambda i: (i, 0)),
        pl.BlockSpec((1, 6, D_MODEL), lambda i: (i // nj, 0, 0)),
        pl.BlockSpec((1, 256, tm), bj3),
        pl.BlockSpec((1, 256, tm), bj3),
        pl.BlockSpec((1, 512, tm), bj3),
        pl.BlockSpec((1, D_MODEL, D_MODEL), lambda i: (layer, 0, 0), pipeline_mode=one),
        pl.BlockSpec((1, D_MODEL), const2),
        pl.BlockSpec((1, D_MODEL), const2),
        pl.BlockSpec((1, D_MODEL, D_FF), lambda i: (layer, 0, 0), pipeline_mode=one),
        pl.BlockSpec((1, D_FF, D_MODEL), lambda i: (layer, 0, 0), pipeline_mode=one),
        pl.BlockSpec((1, D_MODEL), const2),
        pl.BlockSpec((1, D_MODEL), const2),
    ]
    return pl.pallas_call(
        _mlp_kernel,
        out_shape=jax.ShapeDtypeStruct((T, D_MODEL), F32),
        grid=(T // tm,),
        in_specs=in_specs,
        out_specs=pl.BlockSpec((tm, D_MODEL), lambda i: (i, 0)),
        compiler_params=pltpu.CompilerParams(vmem_limit_bytes=VMEM_LIMIT),
        name="out_mlp",
    )(x2d, mod, aoT, boT, coT, wo_all, ln1g, ln1b, w1_all, w2_all, ln2g, ln2b)


def _rope_angles(pos, dim):
    inv = ROPE_THETA ** (-jnp.arange(0, dim, 2, dtype=F32) / dim)
    return pos.astype(F32)[:, None] * inv[None, :]


def _t5_bucket(rel):
    nb = REL_BUCKETS // 2
    max_exact = nb // 2
    sign = jnp.where(rel > 0, nb, 0)
    n = jnp.abs(rel)
    nf = jnp.maximum(n, 1).astype(F32)
    large = max_exact + (jnp.log(nf / max_exact) / math.log(REL_MAX_DIST / max_exact)
                         * (nb - max_exact)).astype(jnp.int32)
    large = jnp.minimum(large, nb - 1)
    return sign + jnp.where(n < max_exact, n, large)


def _dilated_bias_tiles(rel_bias):
    t = A_TILE
    span = (A_CENTER + 1) * t
    deltas = np.arange(-span + 1, span)
    mult = np.zeros(deltas.shape, np.int32)
    for (w, d) in A_PATTERNS:
        half = w // (2 * d)
        mult += ((deltas % d == 0) & (np.abs(deltas) <= half * d)).astype(np.int32)
    bucket = _t5_bucket(jnp.asarray(deltas, jnp.int32))
    logm = jnp.asarray(np.log(np.maximum(mult, 1)), F32)
    tab = jnp.where(jnp.asarray(mult > 0)[:, None],
                    (rel_bias[bucket] + logm[:, None]) * LOG2E, NEG_BIG)
    valid = jnp.asarray(mult > 0)[:, None]
    b_max = jnp.max(jnp.where(valid, tab, -jnp.inf), axis=0)
    b_min = jnp.min(jnp.where(valid, tab, jnp.inf), axis=0)
    x = np.arange(2 * t)
    d = np.where(x <= t - 1, -x, 2 * t - x)
    d[t] = 0
    idx = (np.arange(A_NCHUNK)[:, None] - A_CENTER) * t + d[None, :] + span - 1
    u = jnp.moveaxis(tab[jnp.asarray(idx, jnp.int32)], -1, 0)
    u = jnp.concatenate([u, jnp.full((A_HEADS, 1, 2 * t), NEG_BIG, F32)], axis=1)
    return _toeplitz_call(u.reshape(A_HEADS * (A_NCHUNK + 1), 1, 2 * t)), b_max, b_min


def _toeplitz_kernel(u_ref, o_ref):
    t = A_TILE
    for c in range(u_ref.shape[0]):
        rows = jnp.broadcast_to(u_ref[c], (t, 2 * t))
        o_ref[c] = pltpu.roll(rows, 0, 1, stride=1, stride_axis=0)[:, :t]


def _toeplitz_call(u):
    t = A_TILE
    n = u.shape[0]
    per = A_NCHUNK + 1
    tiles = pl.pallas_call(
        _toeplitz_kernel,
        out_shape=jax.ShapeDtypeStruct((n, t, t), F32),
        grid=(n // per,),
        in_specs=[pl.BlockSpec((per, 1, 2 * t), lambda i: (i, 0, 0))],
        out_specs=pl.BlockSpec((per, t, t), lambda i: (i, 0, 0)),
        name="bias_tiles",
    )(u)
    return tiles.reshape(A_HEADS, A_NCHUNK + 1, t, t)


def _rope_tables(gq, gk):
    t = jnp.arange(SEQ)
    ang_t = _rope_angles(t, B_ROPE)
    ang_row = _rope_angles(t // GRID_W, HEAD_DIM // 2)
    ang_col = _rope_angles(t % GRID_W, HEAD_DIM // 2)
    cos64 = jnp.concatenate([jnp.cos(ang_row)] * 2 + [jnp.cos(ang_col)] * 2, axis=1)
    sin_r, sin_c = jnp.sin(ang_row), jnp.sin(ang_col)
    sin64 = jnp.concatenate([-sin_r, sin_r, -sin_c, sin_c], axis=1)
    perm = np.concatenate([np.arange(16, 32), np.arange(0, 16), np.arange(48, 64), np.arange(32, 48)])

    def with_gain(g, scale):
        return (cos64 * g[None, :] * scale).T, (sin64 * g[perm][None, :] * scale).T

    qc, qs = with_gain(gq, HEAD_DIM ** -0.5 * LOG2E)
    kc, ks = with_gain(gk, 1.0)
    q_bound, _ = _gqa_bounds(gq, gk)
    bound_rows = jnp.zeros((16, SEQ), F32).at[0].set(-q_bound)
    return jnp.concatenate([jnp.cos(ang_t).T, jnp.sin(ang_t).T, qc, qs, kc, ks, bound_rows], axis=0)


def _gqa_bounds(gq, gk):
    root = math.sqrt(HEAD_DIM)
    q_bound = Q_MARGIN * root * (HEAD_DIM ** -0.5 * LOG2E) * jnp.max(jnp.abs(gq))
    k_bound = root * jnp.max(jnp.abs(gk))
    return q_bound, k_bound


def kernel(x, c, w_ada, b_ada, w_in, mla_q_norm, mla_w_uq, mla_kv_norm, mla_w_ukv,
           gqa_q_norm, gqa_k_norm, rel_bias, w_o, ln1_g, ln1_b, w1, w2, ln2_g, ln2_b):
    B, S, D = x.shape
    c_pad = jnp.zeros((8, D), F32).at[:B].set(c)
    mod_all = _ada_call(c_pad, w_ada, b_ada)[:, :B].reshape(DEPTH, B, 6, D)
    bias, b_max, b_min = _dilated_bias_tiles(rel_bias)

    winT_all = jnp.swapaxes(w_in, 1, 2).astype(BF16)
    wo_all, w1_all, w2_all = w_o.astype(BF16), w1.astype(BF16), w2.astype(BF16)
    x2d = x.reshape(B * S, D)
    for l in range(DEPTH):
        wuqT = mla_w_uq[l].T
        wukv = mla_w_ukv[l].reshape(B_KV_RANK, B_HEADS, B_NOPE + B_V)
        wukT = wukv[:, :, :B_NOPE].reshape(B_KV_RANK, B_HEADS * B_NOPE).T
        wuvT = wukv[:, :, B_NOPE:].reshape(B_KV_RANK, B_HEADS * B_V).T
        tab = _rope_tables(gqa_q_norm[l], gqa_k_norm[l])
        aqT, ak, avT, bqT, bk, bvT, cqT, ck, cvT, qn, kn = _proj_call(
            x2d, mod_all[l], winT_all, l, wuqT, mla_q_norm[l][None, :], wukT, wuvT,
            mla_kv_norm[l][None, :], tab)
        a_scal = _dil_scalars(qn[:, QN_A:QN_A + A_HEADS], kn[:, KN_A:KN_A + A_HEADS], b_max, b_min)
        aoT = _dil_attn_call(a_scal, aqT, ak, avT, bias)
        cg = C_Q_HEADS // C_KV_HEADS
        b_scal = _bound_scalars(qn[:, :B_HEADS], kn[:, :B_HEADS], 1)
        cq_bound, ck_bound = _gqa_bounds(gqa_q_norm[l], gqa_k_norm[l])
        c_scal = _bound_scalars(jnp.full((B, C_Q_HEADS, 1), cq_bound), jnp.full((B, C_KV_HEADS, 1), ck_bound), cg)
        boT = _full_attn_call(b_scal, bqT, bk, bvT, groups=1, tq=2048, bound_row=B_NOPE + B_ROPE,
                              name="mla_attn")
        coT = _full_attn_call(c_scal, cqT, ck, cvT, groups=cg, tq=512, bound_row=HEAD_DIM, name="gqa_attn")
        x2d = _mlp_call(x2d, mod_all[l], aoT, boT, coT, l, wo_all,
                        ln1_g[l][None, :], ln1_b[l][None, :], w1_all, w2_all,
                        ln2_g[l][None, :], ln2_b[l][None, :])
    return x2d.reshape(B, S, D)
```

```python
import functools
import math

import numpy as np
import jax
import jax.numpy as jnp
from jax import lax
from jax.experimental import pallas as pl
from jax.experimental.pallas import tpu as pltpu

D_MODEL = 1024
BATCH = 4
SEQ = 4096
DEPTH = 2
HEAD_DIM = 64
A_HEADS = 4
A_PATTERNS = ((128, 1), (512, 4), (2048, 16))
B_HEADS = 4
B_Q_RANK = 384
B_KV_RANK = 256
B_NOPE = 64
B_ROPE = 32
B_V = 64
C_Q_HEADS = 8
C_KV_HEADS = 2
GRID_W = 64
ROPE_THETA = 10000.0
REL_BUCKETS = 32
REL_MAX_DIST = 1024
D_FF = 4 * D_MODEL
EPS = 1e-6
NEG_BIG = -1e30
ALPHA = (2 * DEPTH) ** 0.25
A_W = A_HEADS * HEAD_DIM
IN_COLS = 2208
LOG2E = 1.4426950408889634

F32 = jnp.float32
BF16 = jnp.bfloat16

LANES = 128
VMEM_LIMIT = 56 * 1024 * 1024

OFF_AQ, OFF_AK, OFF_AV = 0, 256, 512
OFF_BCQ, OFF_BCKV, OFF_BKR = 768, 1152, 1408
OFF_CQ, OFF_CK, OFF_CV, OFF_END = 1440, 1952, 2080, 2208

TOK_TILE = 1024
MLP_TILE = 1024
MLP_PARTS = 4
A_TILE = 256
A_QTILES = 2
A_NCHUNK = 9
A_CENTER = 4
KV_TILE = 512
ROW_BLK = 256
QN_A = 12
KN_A = 8
A_SCAL = 12
A_PV_LAG = 4
PV_LAG = 2
RING = 3
STRIP = 512
TAB_CBOUND = 288
TAB_ROWS = 304
TINY = 1e-30
Q_MARGIN = 1.02
K_MARGIN = 1.01
BOUND_LIMIT = 50.0


def _exact_zero_of(v):
    bits = lax.bitcast_convert_type(v, jnp.uint32)
    return ((bits >> 16) >> 16).astype(F32).astype(BF16)


def _nt_dot(a, b):
    return lax.dot_general(a, b, (((1,), (1,)), ((), ())), preferred_element_type=F32)


def _ada_kernel(c_ref, w_ref, b_ref, o_ref):
    c = c_ref[...]
    ca = (c / (1.0 + jnp.exp(-c))).astype(BF16)
    w = w_ref[0].astype(BF16)
    o_ref[0] = jnp.dot(ca, w, preferred_element_type=F32) + b_ref[0]


def _ada_call(c_pad, w_ada, b_ada):
    tn = 1536
    n = 6 * D_MODEL
    return pl.pallas_call(
        _ada_kernel,
        out_shape=jax.ShapeDtypeStruct((DEPTH, 8, n), F32),
        grid=(DEPTH, n // tn),
        in_specs=[
            pl.BlockSpec((8, D_MODEL), lambda l, j: (0, 0)),
            pl.BlockSpec((1, D_MODEL, tn), lambda l, j: (l, 0, j)),
            pl.BlockSpec((1, 1, tn), lambda l, j: (l, 0, j)),
        ],
        out_specs=pl.BlockSpec((1, 8, tn), lambda l, j: (l, 0, j)),
        compiler_params=pltpu.CompilerParams(vmem_limit_bytes=VMEM_LIMIT),
        name="ada_mod",
    )(c_pad, w_ada, b_ada.reshape(DEPTH, 1, n))


def _proj_kernel(x_ref, mod_ref, winT_ref, wuqT_ref, gq_ref, wukT_ref, wuvT_ref, gkv_ref, tab_ref,
                 aqT_ref, ak_ref, avT_ref, bqT_ref, bk_ref, bvT_ref, cqT_ref, ck_ref, cvT_ref,
                 qn_ref, kn_ref):
    tm = x_ref.shape[0]
    sh = mod_ref[0, 0:1, :]
    sc = mod_ref[0, 1:2, :]
    h = (x_ref[...] * (1.0 + sc) + sh).astype(BF16)

    proj_t = _nt_dot(winT_ref[0], h)

    def seg(r0, r1):
        return proj_t[r0:r1]

    def first_row(rows, val):
        return jnp.where(lax.broadcasted_iota(jnp.int32, (rows, tm), 0) == 0, val, 0.0)

    def norm(xh):
        n2 = _sum_sq_rows(xh)
        return n2 * lax.rsqrt(n2 + TINY)

    one_row32 = first_row(32, 1.0)
    one_row64 = first_row(64, 1.0)
    qn_ref[0] = jnp.zeros(qn_ref.shape[1:], F32)
    kn_ref[0] = jnp.zeros(kn_ref.shape[1:], F32)

    aq = seg(OFF_AQ, OFF_AK) * (HEAD_DIM ** -0.5 * LOG2E)
    for hd in range(A_HEADS):
        qh = aq[64 * hd:64 * hd + 64]
        qb = Q_MARGIN * norm(qh)
        qn_ref[0, QN_A + hd:QN_A + hd + 1, :] = qb
        aqT_ref[0, 128 * hd:128 * hd + 64, :] = qh.astype(BF16)
        aqT_ref[0, 128 * hd + 64:128 * hd + 128, :] = first_row(64, -qb).astype(BF16)
    ak = seg(OFF_AK, OFF_AV)
    for hd in range(A_HEADS):
        kh = ak[64 * hd:64 * hd + 64]
        kn_ref[0, KN_A + hd:KN_A + hd + 1, :] = norm(kh)
        kt = jnp.concatenate([kh, one_row64], axis=0)
        ak_ref[0, hd] = kt.T.astype(BF16)
    av = seg(OFF_AV, OFF_BCQ).astype(BF16)
    for j in range(tm // A_TILE):
        avT_ref[0, j] = av[:, A_TILE * j:A_TILE * (j + 1)]

    def normrope(xh, c, s):
        r = lax.rsqrt(_sum_sq_rows(xh) * (1.0 / xh.shape[0]) + EPS)
        rot = jnp.concatenate([xh[16:32], xh[0:16], xh[48:64], xh[32:48]], axis=0)
        return (xh * c + rot * s) * r

    cqc = tab_ref[32:96, :]
    cqs = tab_ref[96:160, :]
    c_bound_rows = tab_ref[TAB_CBOUND:TAB_CBOUND + 16, :].astype(BF16)
    cq2 = seg(OFF_CQ, OFF_CK)
    for hd in range(C_Q_HEADS):
        qh = normrope(cq2[64 * hd:64 * hd + 64], cqc, cqs)
        cqT_ref[0, 128 * hd:128 * hd + 64, :] = qh.astype(BF16)
        cqT_ref[0, 128 * hd + 64:128 * hd + 80, :] = c_bound_rows
        cqT_ref[0, 128 * hd + 80:128 * hd + 128, :] = jnp.zeros((48, tm), BF16)
    ckc = tab_ref[160:224, :]
    cks = tab_ref[224:288, :]
    ck2 = seg(OFF_CK, OFF_CV)
    for hd in range(C_KV_HEADS):
        kh = normrope(ck2[64 * hd:64 * hd + 64], ckc, cks)
        kt = jnp.concatenate([kh, one_row64], axis=0)
        ck_ref[0, hd] = kt.T.astype(BF16)
    cv = seg(OFF_CV, OFF_END).astype(BF16)
    for j in range(tm // KV_TILE):
        cvT_ref[0, j] = cv[:, KV_TILE * j:KV_TILE * (j + 1)]

    cos_t = tab_ref[0:16, :]
    sin_t = tab_ref[16:32, :]
    cq = seg(OFF_BCQ, OFF_BCKV)
    rq = lax.rsqrt(_sum_sq_rows(cq) * (1.0 / B_Q_RANK) + EPS)
    wq = (wuqT_ref[...] * gq_ref[...]).astype(BF16)
    uq = jnp.dot(wq, cq.astype(BF16), preferred_element_type=F32)
    uq = uq * (rq * ((B_NOPE + B_ROPE) ** -0.5 * LOG2E))
    for hd in range(B_HEADS):
        b0 = (B_NOPE + B_ROPE) * hd
        x1 = uq[b0 + 64:b0 + 80]
        x2 = uq[b0 + 80:b0 + 96]
        qb = Q_MARGIN * norm(uq[b0:b0 + 96])
        qn_ref[0, hd:hd + 1, :] = qb
        bqT_ref[0, 128 * hd:128 * hd + 64, :] = uq[b0:b0 + 64].astype(BF16)
        bqT_ref[0, 128 * hd + 64:128 * hd + 80, :] = (x1 * cos_t - x2 * sin_t).astype(BF16)
        bqT_ref[0, 128 * hd + 80:128 * hd + 96, :] = (x1 * sin_t + x2 * cos_t).astype(BF16)
        bqT_ref[0, 128 * hd + 96:128 * hd + 128, :] = first_row(32, -qb).astype(BF16)
    ckv = seg(OFF_BCKV, OFF_BKR)
    rkv = lax.rsqrt(_sum_sq_rows(ckv) * (1.0 / B_KV_RANK) + EPS)
    ckv_b = ckv.astype(BF16)
    gkv = gkv_ref[...]
    kn = jnp.dot((wukT_ref[...] * gkv).astype(BF16), ckv_b, preferred_element_type=F32) * rkv
    vv = jnp.dot((wuvT_ref[...] * gkv).astype(BF16), ckv_b, preferred_element_type=F32) * rkv
    vv = vv.astype(BF16)
    for j in range(tm // KV_TILE):
        bvT_ref[0, j] = vv[:, KV_TILE * j:KV_TILE * (j + 1)]
    kr = seg(OFF_BKR, OFF_CQ)
    kr1 = kr[0:16]
    kr2 = kr[16:32]
    kro = jnp.concatenate([kr1 * cos_t - kr2 * sin_t, kr1 * sin_t + kr2 * cos_t,
                           one_row32], axis=0)
    kr_sq = _sum_sq_rows(kr)
    for hd in range(B_HEADS):
        knh = kn[64 * hd:64 * hd + 64]
        k_sq = _sum_sq_rows(knh) + kr_sq
        kn_ref[0, hd:hd + 1, :] = k_sq * lax.rsqrt(k_sq + TINY)
        kt = jnp.concatenate([knh, kro], axis=0)
        bk_ref[0, hd] = kt.T.astype(BF16)


def _proj_call(x2d, mod, winT_all, layer, wuqT, gq, wukT, wuvT, gkv, tab):
    tm = TOK_TILE
    T = x2d.shape[0]
    nj = SEQ // tm
    const2 = lambda i: (0, 0)
    bj3 = lambda i: (i // nj, 0, i % nj)
    bj4 = lambda i: (i // nj, 0, i % nj, 0)
    cj4 = lambda i: (i // nj, i % nj, 0, 0)
    out_shape = (
        jax.ShapeDtypeStruct((BATCH, 512, SEQ), BF16),
        jax.ShapeDtypeStruct((BATCH, A_HEADS, SEQ, 128), BF16),
        jax.ShapeDtypeStruct((BATCH, SEQ // A_TILE, 256, A_TILE), BF16),
        jax.ShapeDtypeStruct((BATCH, 512, SEQ), BF16),
        jax.ShapeDtypeStruct((BATCH, B_HEADS, SEQ, 128), BF16),
        jax.ShapeDtypeStruct((BATCH, SEQ // KV_TILE, 256, KV_TILE), BF16),
        jax.ShapeDtypeStruct((BATCH, 1024, SEQ), BF16),
        jax.ShapeDtypeStruct((BATCH, C_KV_HEADS, SEQ, 128), BF16),
        jax.ShapeDtypeStruct((BATCH, SEQ // KV_TILE, 128, KV_TILE), BF16),
        jax.ShapeDtypeStruct((BATCH, 16, SEQ), F32),
        jax.ShapeDtypeStruct((BATCH, 16, SEQ), F32),
    )
    out_specs = (
        pl.BlockSpec((1, 512, tm), bj3),
        pl.BlockSpec((1, A_HEADS, tm, 128), bj4),
        pl.BlockSpec((1, tm // A_TILE, 256, A_TILE), cj4),
        pl.BlockSpec((1, 512, tm), bj3),
        pl.BlockSpec((1, B_HEADS, tm, 128), bj4),
        pl.BlockSpec((1, tm // KV_TILE, 256, KV_TILE), cj4),
        pl.BlockSpec((1, 1024, tm), bj3),
        pl.BlockSpec((1, C_KV_HEADS, tm, 128), bj4),
        pl.BlockSpec((1, tm // KV_TILE, 128, KV_TILE), cj4),
        pl.BlockSpec((1, 16, tm), bj3),
        pl.BlockSpec((1, 16, tm), bj3),
    )
    in_specs = [
        pl.BlockSpec((tm, D_MODEL), lambda i: (i, 0)),
        pl.BlockSpec((1, 6, D_MODEL), lambda i: (i // nj, 0, 0)),
        pl.BlockSpec((1, IN_COLS, D_MODEL), lambda i: (layer, 0, 0)),
        pl.BlockSpec((B_Q_RANK, B_Q_RANK), const2),
        pl.BlockSpec((1, B_Q_RANK), const2),
        pl.BlockSpec((256, B_KV_RANK), const2),
        pl.BlockSpec((256, B_KV_RANK), const2),
        pl.BlockSpec((1, B_KV_RANK), const2),
        pl.BlockSpec((TAB_ROWS, tm), lambda i: (0, i % nj)),
    ]
    return pl.pallas_call(
        _proj_kernel,
        out_shape=out_shape,
        grid=(T // tm,),
        in_specs=in_specs,
        out_specs=out_specs,
        compiler_params=pltpu.CompilerParams(vmem_limit_bytes=VMEM_LIMIT),
        name="in_proj",
    )(x2d, mod, winT_all, wuqT, gq, wukT, wuvT, gkv, tab)


def _with_ones_rows(v):
    return jnp.concatenate([v, jnp.ones((16, v.shape[1]), BF16)], axis=0)


def _sublane_partial_sum(p):
    return functools.reduce(lambda a, b: a + b, [p[8 * i:8 * (i + 1)] for i in range(p.shape[0] // 8)])


def _sum_sq_rows(x):
    return jnp.sum(_sublane_partial_sum(x * x), axis=0, keepdims=True)


def _attn_finish(o_ref, num, den, groups, tq):
    o = num / den
    for g in range(groups):
        o_ref[0, 64 * g:64 * g + 64, :] = o[:, tq * g:tq * (g + 1)].astype(o_ref.dtype)


def _attn_online(q_strips, k_ref, vT_ref, o_ref, acc_ref, s_ref, p_ref, *, groups, tq, tk):
    nstrip = len(q_strips)
    w = q_strips[0].shape[1]
    nchunk = SEQ // tk
    rb = ROW_BLK
    nb = tk // rb
    nunit = nchunk * nstrip
    acc_ref[...] = jnp.zeros(acc_ref.shape, F32)
    m = [jnp.full((1, w), NEG_BIG, F32) for _ in range(nstrip)]

    def qk_block(u, b):
        c, j = divmod(u, nstrip)
        k = k_ref[0, 0, c * tk + rb * b:c * tk + rb * (b + 1), :]
        return jnp.dot(k, q_strips[j], preferred_element_type=F32)

    def pv_block(u, b, p_blk):
        c = u // nstrip
        v_aug = _with_ones_rows(vT_ref[0, c, :, rb * b:rb * (b + 1)])
        return jnp.dot(v_aug, p_blk, preferred_element_type=F32)

    alphas = {}
    for t in range(nunit + 2):
        pv = None
        for b in range(nb):
            rows = slice(rb * b, rb * (b + 1))
            if t < nunit:
                s_ref[t % RING, rows, :] = qk_block(t, b)
            if t >= 2:
                part = pv_block(t - 2, b, p_ref[(t - 2) % RING, rows, :])
                pv = part if pv is None else pv + part
        if 1 <= t <= nunit:
            u = t - 1
            j = u % nstrip
            s_u = s_ref[u % RING]
            m_new = jnp.maximum(m[j], jnp.max(s_u, axis=0, keepdims=True))
            alphas[u] = jnp.exp2(m[j] - m_new)
            p_ref[u % RING] = jnp.exp2(s_u - m_new).astype(BF16)
            m[j] = m_new
        if t >= 2:
            u = t - 2
            sl = slice(w * (u % nstrip), w * (u % nstrip + 1))
            acc_ref[:, sl] = alphas.pop(u) * acc_ref[:, sl] + pv
    _attn_finish(o_ref, acc_ref[0:64, :], acc_ref[64:65, :], groups, tq)


def _attn_bounded(q_strips, k_ref, vT_ref, o_ref, *, groups, tq, tk):
    nstrip = len(q_strips)
    rb = ROW_BLK
    nkb = SEQ // rb
    units = [(kb, j) for kb in range(nkb) for j in range(nstrip)]
    acc = [None] * nstrip
    den = [None] * nstrip
    probs = {}

    def pv(u):
        kb, j = units[u]
        c, b = divmod(kb, tk // rb)
        part = jnp.dot(vT_ref[0, c, :, rb * b:rb * (b + 1)], probs.pop(u), preferred_element_type=F32)
        acc[j] = part if acc[j] is None else acc[j] + part

    for t in range(len(units)):
        kb, j = units[t]
        s = jnp.dot(k_ref[0, 0, rb * kb:rb * (kb + 1), :], q_strips[j], preferred_element_type=F32)
        p = jnp.exp2(s)
        part = _sublane_partial_sum(p)
        den[j] = part if den[j] is None else den[j] + part
        probs[t] = p.astype(BF16)
        if t >= PV_LAG:
            pv(t - PV_LAG)
    for u in range(len(units) - PV_LAG, len(units)):
        pv(u)
    den = [jnp.sum(d, axis=0, keepdims=True) for d in den]
    _attn_finish(o_ref, jnp.concatenate(acc, axis=1), jnp.concatenate(den, axis=1), groups, tq)


def _full_attn_kernel(scal_ref, qT_ref, k_ref, vT_ref, o_ref, acc_ref, s_ref, p_ref,
                      *, groups, tq, tk, bound_row):
    n = groups * tq
    w = min(n, STRIP)
    nstrip = n // w
    pair = 2 * (pl.program_id(0) * pl.num_programs(1) + pl.program_id(1))
    k_bound = scal_ref[pair]
    bounded_ok = scal_ref[pair + 1] > 0.5
    cols = [qT_ref[0, 128 * g:128 * (g + 1), :] for g in range(groups)]
    q_cat = cols[0] if groups == 1 else jnp.concatenate(cols, axis=1)
    q_lo = q_cat[0:bound_row]
    q_bound = q_cat[bound_row:bound_row + 16]
    q_hi = q_cat[bound_row + 16:]

    def strips(mid):
        q = jnp.concatenate([q_lo, mid, q_hi], axis=0)
        return [q[:, w * j:w * (j + 1)] for j in range(nstrip)]

    @pl.when(bounded_ok)
    def _():
        mid = (q_bound.astype(F32) * k_bound).astype(BF16)
        _attn_bounded(strips(mid), k_ref, vT_ref, o_ref, groups=groups, tq=tq, tk=tk)

    @pl.when(jnp.logical_not(bounded_ok))
    def _():
        _attn_online(strips(jnp.zeros_like(q_bound)), k_ref, vT_ref, o_ref, acc_ref, s_ref, p_ref,
                     groups=groups, tq=tq, tk=tk)


def _bound_scalars(qn, kn, groups):
    k_bound = K_MARGIN * jnp.max(kn, axis=-1)
    q_max = jnp.max(qn, axis=-1).reshape(BATCH, kn.shape[1], groups).max(-1)
    ok = (q_max * k_bound <= BOUND_LIMIT).astype(F32)
    return jnp.stack([k_bound, ok], axis=-1).reshape(-1)


def _full_attn_call(scal, qT, k, vT, *, groups, tq, bound_row, name):
    tk = KV_TILE
    hkv = k.shape[1]
    n = groups * tq
    kern = functools.partial(_full_attn_kernel, groups=groups, tq=tq, tk=tk, bound_row=bound_row)
    return pl.pallas_call(
        kern,
        out_shape=jax.ShapeDtypeStruct((BATCH, hkv * groups * 64, SEQ), BF16),
        grid=(BATCH, hkv, SEQ // tq),
        in_specs=[
            pl.BlockSpec(memory_space=pltpu.SMEM),
            pl.BlockSpec((1, groups * 128, tq), lambda b, g, i: (b, g, i)),
            pl.BlockSpec((1, 1, SEQ, 128), lambda b, g, i: (b, g, 0, 0)),
            pl.BlockSpec((1, SEQ // tk, 64, tk), lambda b, g, i: (b, 0, g, 0)),
        ],
        out_specs=pl.BlockSpec((1, groups * 64, tq), lambda b, g, i: (b, g, i)),
        scratch_shapes=[pltpu.VMEM((80, n), F32),
                        pltpu.VMEM((RING, tk, min(n, STRIP)), F32),
                        pltpu.VMEM((RING, tk, min(n, STRIP)), BF16)],
        compiler_params=pltpu.CompilerParams(vmem_limit_bytes=VMEM_LIMIT),
        name=name,
    )(scal, qT, k, vT)


def _dil_attn_kernel(scal_ref, qT_ref, k_ref, vT_ref, bias_ref, o_ref):
    t = A_TILE
    nq = SEQ // t
    tiles = [pl.program_id(1) * A_QTILES + qt for qt in range(A_QTILES)]
    firsts = [jnp.clip(i - A_CENTER, 0, nq - A_NCHUNK) for i in tiles]
    base = A_SCAL * pl.program_id(0)
    bounded_ok = scal_ref[base] > 0.5
    top_row = lax.broadcasted_iota(jnp.int32, (16, t), 0) == 0

    def q_head(qt, h, mid_fn):
        q = qT_ref[0, 128 * h:128 * (h + 1), t * qt:t * (qt + 1)]
        return jnp.concatenate([q[0:HEAD_DIM], mid_fn(h, q[HEAD_DIM:HEAD_DIM + 16]), q[HEAD_DIM + 16:]], axis=0)

    def score_tile(qt, h, r, q):
        kt = firsts[qt] + r
        d = kt - tiles[qt] + A_CENTER
        bi = jnp.where((d >= 0) & (d < A_NCHUNK), d, A_NCHUNK)
        k = k_ref[0, h, pl.ds(pl.multiple_of(kt * t, t), t), :]
        return jnp.dot(k, q, preferred_element_type=F32) + bias_ref[h, bi]

    def pv_tile(qt, h, r, p, ones_rows):
        v = vT_ref[0, firsts[qt] + r, 64 * h:64 * (h + 1), :]
        return jnp.dot(_with_ones_rows(v) if ones_rows else v, p, preferred_element_type=F32)

    def store(qt, h, num, den):
        o_ref[0, 64 * h:64 * (h + 1), t * qt:t * (qt + 1)] = (num / den).astype(o_ref.dtype)

    @pl.when(bounded_ok)
    def _():
        def mid(h, q_bound):
            shift = q_bound.astype(F32) * scal_ref[base + 1 + h]
            return (shift - jnp.where(top_row, scal_ref[base + 1 + A_HEADS + h], 0.0)).astype(BF16)

        heads = [(qt, h) for qt in range(A_QTILES) for h in range(A_HEADS)]
        qs = {qh: q_head(qh[0], qh[1], mid) for qh in heads}
        units = [(qh, r) for r in range(A_NCHUNK) for qh in heads]
        acc = {qh: None for qh in heads}
        den = {qh: None for qh in heads}
        probs = {}

        def pv(u):
            qh, r = units[u]
            part = pv_tile(qh[0], qh[1], r, probs.pop(u), False)
            acc[qh] = part if acc[qh] is None else acc[qh] + part

        for u, (qh, r) in enumerate(units):
            p = jnp.exp2(score_tile(qh[0], qh[1], r, qs[qh]))
            part = _sublane_partial_sum(p)
            den[qh] = part if den[qh] is None else den[qh] + part
            probs[u] = p.astype(BF16)
            if u >= A_PV_LAG:
                pv(u - A_PV_LAG)
        for u in range(len(units) - A_PV_LAG, len(units)):
            pv(u)
        for qh in heads:
            store(qh[0], qh[1], acc[qh], jnp.sum(den[qh], axis=0, keepdims=True))

    @pl.when(jnp.logical_not(bounded_ok))
    def _():
        heads = [(qt, h) for qt in range(A_QTILES) for h in range(A_HEADS)]

        def scores(qh):
            q = q_head(qh[0], qh[1], lambda _, q_bound: jnp.zeros_like(q_bound))
            return [score_tile(qh[0], qh[1], r, q) for r in range(A_NCHUNK)]

        def finish(qh, s_list):
            m = jnp.max(functools.reduce(jnp.maximum, s_list), axis=0, keepdims=True)
            acc = None
            for r in range(A_NCHUNK):
                part = pv_tile(qh[0], qh[1], r, jnp.exp2(s_list[r] - m).astype(BF16), True)
                acc = part if acc is None else acc + part
            store(qh[0], qh[1], acc[0:64], acc[64:65])

        s_next = scores(heads[0])
        for n, qh in enumerate(heads):
            s_cur = s_next
            if n + 1 < len(heads):
                s_next = scores(heads[n + 1])
            finish(qh, s_cur)


def _dil_scalars(qn, kn, b_max, b_min):
    k_bound = K_MARGIN * jnp.max(kn, axis=-1)
    q_max = jnp.max(qn, axis=-1)
    spread = 2.0 * q_max * k_bound + (b_max - b_min)[None, :]
    ok = jnp.all(spread <= 2.0 * BOUND_LIMIT, axis=-1, keepdims=True).astype(F32)
    pad = jnp.zeros((BATCH, A_SCAL - 1 - 2 * A_HEADS), F32)
    return jnp.concatenate([ok, k_bound, jnp.broadcast_to(b_max[None, :], (BATCH, A_HEADS)), pad], axis=1).reshape(-1)


def _dil_attn_call(scal, qT, k, vT, bias):
    t = A_TILE
    return pl.pallas_call(
        _dil_attn_kernel,
        out_shape=jax.ShapeDtypeStruct((BATCH, A_HEADS * 64, SEQ), BF16),
        grid=(BATCH, SEQ // (t * A_QTILES)),
        in_specs=[
            pl.BlockSpec(memory_space=pltpu.SMEM),
            pl.BlockSpec((1, A_HEADS * 128, t * A_QTILES), lambda b, i: (b, 0, i)),
            pl.BlockSpec((1, A_HEADS, SEQ, 128), lambda b, i: (b, 0, 0, 0)),
            pl.BlockSpec((1, SEQ // t, A_HEADS * 64, t), lambda b, i: (b, 0, 0, 0)),
            pl.BlockSpec((A_HEADS, A_NCHUNK + 1, t, t), lambda b, i: (0, 0, 0, 0),
                         pipeline_mode=pl.Buffered(1)),
        ],
        out_specs=pl.BlockSpec((1, A_HEADS * 64, t * A_QTILES), lambda b, i: (b, 0, i)),
        compiler_params=pltpu.CompilerParams(vmem_limit_bytes=VMEM_LIMIT),
        name="dilated_attn",
    )(scal, qT, k, vT, bias)


def _layer_norm(z, g, b):
    mu = jnp.mean(z, axis=-1, keepdims=True)
    d = z - mu
    var = jnp.mean(d * d, axis=-1, keepdims=True)
    return d * lax.rsqrt(var + EPS) * g + b


def _mlp_kernel(x_ref, mod_ref, aoT_ref, boT_ref, coT_ref, wo_ref, ln1g_ref, ln1b_ref,
                w1_ref, w2_ref, ln2g_ref, ln2b_ref, o_ref):
    g_a = mod_ref[0, 2:3, :]
    sh_m = mod_ref[0, 3:4, :]
    sc_m = mod_ref[0, 4:5, :]
    g_m = mod_ref[0, 5:6, :]
    tm = x_ref.shape[0]
    hr = tm // MLP_PARTS
    fc = 1024
    nfc = D_FF // fc

    def out_proj(part):
        cols = slice(hr * part, hr * (part + 1))
        catT = jnp.concatenate([aoT_ref[0, :, cols], boT_ref[0, :, cols], coT_ref[0, :, cols]], axis=0)
        cat = catT.astype(F32).T.astype(BF16)
        return jnp.dot(cat, wo_ref[0], preferred_element_type=F32)

    ys = [out_proj(part) for part in range(MLP_PARTS)]
    prev_out = None
    for part in range(MLP_PARTS):
        rows = slice(hr * part, hr * (part + 1))
        x1 = _layer_norm(ALPHA * x_ref[rows, :] + (1.0 + g_a) * ys[part], ln1g_ref[...], ln1b_ref[...])
        h = (x1 * (1.0 + sc_m) + sh_m).astype(BF16)
        y2 = None
        for j in range(nfc):
            u = jnp.dot(h, w1_ref[0, :, fc * j:fc * (j + 1)], preferred_element_type=F32)
            u = jnp.maximum(u, 0.0)
            u = (u * u).astype(BF16)
            if prev_out is not None and j == nfc - 1:
                folded = _sublane_partial_sum(prev_out)
                folded = functools.reduce(lambda a, b: a + b,
                                          [folded[:, LANES * i:LANES * (i + 1)] for i in range(D_MODEL // LANES)])
                zero = _exact_zero_of(jnp.concatenate([folded, folded], axis=0))
                top = jnp.concatenate([u[0:16, 0:LANES] + zero, u[0:16, LANES:]], axis=1)
                u = jnp.concatenate([top, u[16:]], axis=0)
            term = jnp.dot(u, w2_ref[0, fc * j:fc * (j + 1), :], preferred_element_type=F32)
            y2 = term if y2 is None else y2 + term
        prev_out = _layer_norm(ALPHA * x1 + (1.0 + g_m) * y2, ln2g_ref[...], ln2b_ref[...])
        o_ref[rows, :] = prev_out


def _mlp_call(x2d, mod, aoT, boT, coT, layer, wo_all, ln1g, ln1b, w1_all, w2_all, ln2g, ln2b):
    tm = MLP_TILE
    T = x2d.shape[0]
    nj = SEQ // tm
    const2 = lambda i: (0, 0)
    bj3 = lambda i: (i // nj, 0, i % nj)
    one = pl.Buffered(1)
    in_specs = [
        pl.BlockSpec((tm, D_MODEL), lambda i: (i, 0)),
        pl.BlockSpec((1, 6, D_MODEL), lambda i: (i // nj, 0, 0)),
        pl.BlockSpec((1, 256, tm), bj3),
        pl.BlockSpec((1, 256, tm), bj3),
        pl.BlockSpec((1, 512, tm), bj3),
        pl.BlockSpec((1, D_MODEL, D_MODEL), lambda i: (layer, 0, 0), pipeline_mode=one),
        pl.BlockSpec((1, D_MODEL), const2),
        pl.BlockSpec((1, D_MODEL), const2),
        pl.BlockSpec((1, D_MODEL, D_FF), lambda i: (layer, 0, 0), pipeline_mode=one),
        pl.BlockSpec((1, D_FF, D_MODEL), lambda i: (layer, 0, 0), pipeline_mode=one),
        pl.BlockSpec((1, D_MODEL), const2),
        pl.BlockSpec((1, D_MODEL), const2),
    ]
    return pl.pallas_call(
        _mlp_kernel,
        out_shape=jax.ShapeDtypeStruct((T, D_MODEL), F32),
        grid=(T // tm,),
        in_specs=in_specs,
        out_specs=pl.BlockSpec((tm, D_MODEL), lambda i: (i, 0)),
        compiler_params=pltpu.CompilerParams(vmem_limit_bytes=VMEM_LIMIT),
        name="out_mlp",
    )(x2d, mod, aoT, boT, coT, wo_all, ln1g, ln1b, w1_all, w2_all, ln2g, ln2b)


def _rope_angles(pos, dim):
    inv = ROPE_THETA ** (-jnp.arange(0, dim, 2, dtype=F32) / dim)
    return pos.astype(F32)[:, None] * inv[None, :]


def _t5_bucket(rel):
    nb = REL_BUCKETS // 2
    max_exact = nb // 2
    sign = jnp.where(rel > 0, nb, 0)
    n = jnp.abs(rel)
    nf = jnp.maximum(n, 1).astype(F32)
    large = max_exact + (jnp.log(nf / max_exact) / math.log(REL_MAX_DIST / max_exact)
                         * (nb - max_exact)).astype(jnp.int32)
    large = jnp.minimum(large, nb - 1)
    return sign + jnp.where(n < max_exact, n, large)


def _dilated_bias_tiles(rel_bias):
    t = A_TILE
    span = (A_CENTER + 1) * t
    deltas = np.arange(-span + 1, span)
    mult = np.zeros(deltas.shape, np.int32)
    for (w, d) in A_PATTERNS:
        half = w // (2 * d)
        mult += ((deltas % d == 0) & (np.abs(deltas) <= half * d)).astype(np.int32)
    bucket = _t5_bucket(jnp.asarray(deltas, jnp.int32))
    logm = jnp.asarray(np.log(np.maximum(mult, 1)), F32)
    tab = jnp.where(jnp.asarray(mult > 0)[:, None],
                    (rel_bias[bucket] + logm[:, None]) * LOG2E, NEG_BIG)
    valid = jnp.asarray(mult > 0)[:, None]
    b_max = jnp.max(jnp.where(valid, tab, -jnp.inf), axis=0)
    b_min = jnp.min(jnp.where(valid, tab, jnp.inf), axis=0)
    x = np.arange(2 * t)
    d = np.where(x <= t - 1, -x, 2 * t - x)
    d[t] = 0
    idx = (np.arange(A_NCHUNK)[:, None] - A_CENTER) * t + d[None, :] + span - 1
    u = jnp.moveaxis(tab[jnp.asarray(idx, jnp.int32)], -1, 0)
    u = jnp.concatenate([u, jnp.full((A_HEADS, 1, 2 * t), NEG_BIG, F32)], axis=1)
    return _toeplitz_call(u.reshape(A_HEADS * (A_NCHUNK + 1), 1, 2 * t)), b_max, b_min


def _toeplitz_kernel(u_ref, o_ref):
    t = A_TILE
    for c in range(u_ref.shape[0]):
        rows = jnp.broadcast_to(u_ref[c], (t, 2 * t))
        o_ref[c] = pltpu.roll(rows, 0, 1, stride=1, stride_axis=0)[:, :t]


def _toeplitz_call(u):
    t = A_TILE
    n = u.shape[0]
    per = A_NCHUNK + 1
    tiles = pl.pallas_call(
        _toeplitz_kernel,
        out_shape=jax.ShapeDtypeStruct((n, t, t), F32),
        grid=(n // per,),
        in_specs=[pl.BlockSpec((per, 1, 2 * t), lambda i: (i, 0, 0))],
        out_specs=pl.BlockSpec((per, t, t), lambda i: (i, 0, 0)),
        name="bias_tiles",
    )(u)
    return tiles.reshape(A_HEADS, A_NCHUNK + 1, t, t)


def _rope_tables(gq, gk):
    t = jnp.arange(SEQ)
    ang_t = _rope_angles(t, B_ROPE)
    ang_row = _rope_angles(t // GRID_W, HEAD_DIM // 2)
    ang_col = _rope_angles(t % GRID_W, HEAD_DIM // 2)
    cos64 = jnp.concatenate([jnp.cos(ang_row)] * 2 + [jnp.cos(ang_col)] * 2, axis=1)
    sin_r, sin_c = jnp.sin(ang_row), jnp.sin(ang_col)
    sin64 = jnp.concatenate([-sin_r, sin_r, -sin_c, sin_c], axis=1)
    perm = np.concatenate([np.arange(16, 32), np.arange(0, 16), np.arange(48, 64), np.arange(32, 48)])

    def with_gain(g, scale):
        return (cos64 * g[None, :] * scale).T, (sin64 * g[perm][None, :] * scale).T

    qc, qs = with_gain(gq, HEAD_DIM ** -0.5 * LOG2E)
    kc, ks = with_gain(gk, 1.0)
    q_bound, _ = _gqa_bounds(gq, gk)
    bound_rows = jnp.zeros((16, SEQ), F32).at[0].set(-q_bound)
    return jnp.concatenate([jnp.cos(ang_t).T, jnp.sin(ang_t).T, qc, qs, kc, ks, bound_rows], axis=0)


def _gqa_bounds(gq, gk):
    root = math.sqrt(HEAD_DIM)
    q_bound = Q_MARGIN * root * (HEAD_DIM ** -0.5 * LOG2E) * jnp.max(jnp.abs(gq))
    k_bound = root * jnp.max(jnp.abs(gk))
    return q_bound, k_bound


def kernel(x, c, w_ada, b_ada, w_in, mla_q_norm, mla_w_uq, mla_kv_norm, mla_w_ukv,
           gqa_q_norm, gqa_k_norm, rel_bias, w_o, ln1_g, ln1_b, w1, w2, ln2_g, ln2_b):
    B, S, D = x.shape
    c_pad = jnp.zeros((8, D), F32).at[:B].set(c)
    mod_all = _ada_call(c_pad, w_ada, b_ada)[:, :B].reshape(DEPTH, B, 6, D)
    bias, b_max, b_min = _dilated_bias_tiles(rel_bias)

    winT_all = jnp.swapaxes(w_in, 1, 2).astype(BF16)
    wo_all, w1_all, w2_all = w_o.astype(BF16), w1.astype(BF16), w2.astype(BF16)
    x2d = x.reshape(B * S, D)
    for l in range(DEPTH):
        wuqT = mla_w_uq[l].T
        wukv = mla_w_ukv[l].reshape(B_KV_RANK, B_HEADS, B_NOPE + B_V)
        wukT = wukv[:, :, :B_NOPE].reshape(B_KV_RANK, B_HEADS * B_NOPE).T
        wuvT = wukv[:, :, B_NOPE:].reshape(B_KV_RANK, B_HEADS * B_V).T
        tab = _rope_tables(gqa_q_norm[l], gqa_k_norm[l])
        aqT, ak, avT, bqT, bk, bvT, cqT, ck, cvT, qn, kn = _proj_call(
            x2d, mod_all[l], winT_all, l, wuqT, mla_q_norm[l][None, :], wukT, wuvT,
            mla_kv_norm[l][None, :], tab)
        a_scal = _dil_scalars(qn[:, QN_A:QN_A + A_HEADS], kn[:, KN_A:KN_A + A_HEADS], b_max, b_min)
        aoT = _dil_attn_call(a_scal, aqT, ak, avT, bias)
        cg = C_Q_HEADS // C_KV_HEADS
        b_scal = _bound_scalars(qn[:, :B_HEADS], kn[:, :B_HEADS], 1)
        cq_bound, ck_bound = _gqa_bounds(gqa_q_norm[l], gqa_k_norm[l])
        c_scal = _bound_scalars(jnp.full((B, C_Q_HEADS, 1), cq_bound), jnp.full((B, C_KV_HEADS, 1), ck_bound), cg)
        boT = _full_attn_call(b_scal, bqT, bk, bvT, groups=1, tq=2048, bound_row=B_NOPE + B_ROPE,
                              name="mla_attn")
        coT = _full_attn_call(c_scal, cqT, ck, cvT, groups=cg, tq=512, bound_row=HEAD_DIM, name="gqa_attn")
        x2d = _mlp_call(x2d, mod_all[l], aoT, boT, coT, l, wo_all,
                        ln1_g[l][None, :], ln1_b[l][None, :], w1_all, w2_all,
                        ln2_g[l][None, :], ln2_b[l][None, :])
    return x2d.reshape(B, S, D)
```
